```python
import jax, jax.numpy as jnp
from jax import lax
import numpy as np

D_MODEL = 1024
BATCH = 4
SEQ = 8192
DEPTH = 4

HEAD_DIM = 64
RW_HEADS = 6
SW_HEADS = 6
SW_KV_HEADS = 2
ML_HEADS = 4
RW_DIM = RW_HEADS * HEAD_DIM
SW_DIM = SW_HEADS * HEAD_DIM
SW_KV_DIM = SW_KV_HEADS * HEAD_DIM
ML_DIM = ML_HEADS * HEAD_DIM
D_MIX = RW_DIM + SW_DIM + ML_DIM
RW_LORA_W = 64
RW_LORA_A = 64
RW_LORA_G = 128
RW_COLS = 3 * RW_DIM + RW_LORA_W + RW_LORA_A + RW_LORA_G
SW_COLS = SW_DIM + 2 * SW_KV_DIM
ML_COLS = 4 * ML_DIM + 2 * ML_HEADS
C_IN = RW_COLS + SW_COLS + ML_COLS
WINDOW = 128
ROPE_THETA = 10000.0
ML_CHUNK = 64
CONV_WIDTH = 4
N_EXPERTS = 32
TOP_K = 4
D_EXPERT = 1024
SWIGLU_ALPHA = 1.702
SWIGLU_LIMIT = 7.0
MOE_BLOCK = 128
LN_EPS = 1e-5
RW_GN_EPS = 64e-5
ML_GN_EPS = 1e-6
DN_ALPHA = (2 * DEPTH) ** 0.25
DN_BETA = (8 * DEPTH) ** -0.25

kernel_name = 'hybrid_rwkv7_swa_mlstm_moe_deepnorm'

F32 = jnp.float32


def layer_norm(x, w, b):
    xf = x.astype(F32)
    mu = xf.mean(-1, keepdims=True)
    var = jnp.mean(jnp.square(xf - mu), -1, keepdims=True)
    return ((xf - mu) * lax.rsqrt(var + LN_EPS) * w + b).astype(x.dtype)


def head_norm(x, n_heads, eps):
    B, S, _ = x.shape
    xh = x.astype(F32).reshape(B, S, n_heads, -1)
    mu = xh.mean(-1, keepdims=True)
    var = jnp.mean(jnp.square(xh - mu), -1, keepdims=True)
    return ((xh - mu) * lax.rsqrt(var + eps)).reshape(B, S, -1)


def split_heads(t, n_heads):
    B, S, _ = t.shape
    return t.reshape(B, S, n_heads, -1)


def rope(x):
    S, d = x.shape[1], x.shape[-1]
    half = d // 2
    inv = ROPE_THETA ** (-jnp.arange(half, dtype=F32) / half)
    ang = jnp.arange(S, dtype=F32)[:, None] * inv[None, :]
    cos = jnp.cos(ang)[None, :, None, :]
    sin = jnp.sin(ang)[None, :, None, :]
    xf = x.astype(F32)
    x1, x2 = xf[..., :half], xf[..., half:]
    return jnp.concatenate([x1 * cos - x2 * sin, x2 * cos + x1 * sin], -1)


def causal_depthwise_conv(x, w, b):
    K = w.shape[0]
    xp = jnp.pad(x, ((0, 0), (K - 1, 0), (0, 0)))
    y = lax.conv_general_dilated(xp, w[:, None, :].astype(x.dtype), window_strides=(1,), padding='VALID',
                                 dimension_numbers=('NWC', 'WIO', 'NWC'), feature_group_count=x.shape[-1])
    return y + b


def rwkv7_recurrence(r, w, k, v, a, b):
    B, S, H, N = r.shape

    def step(state, inp):
        r_t, w_t, k_t, v_t, a_t, b_t = inp
        sa = jnp.einsum('bhvk,bhk->bhv', state, a_t)
        state = state * w_t[:, :, None, :] + sa[..., None] * b_t[:, :, None, :] + v_t[..., None] * k_t[:, :, None, :]
        return state, jnp.einsum('bhvk,bhk->bhv', state, r_t)

    xs = tuple(jnp.moveaxis(t, 1, 0) for t in (r, w, k, v, a, b))
    _, ys = lax.scan(step, jnp.zeros((B, H, N, N), F32), xs)
    return jnp.moveaxis(ys, 0, 1)


def rwkv7_mixer(p, shift_mu, w_up, w0, a_up, a0, g_up, k_k, k_a, r_k, ln_w, ln_b):
    B, S, _ = p.shape
    p_prev = jnp.pad(p, ((0, 0), (1, 0), (0, 0)))[:, :-1]
    p = p + (p_prev - p) * shift_mu
    r, k, v, lw, la, lg = jnp.split(p, [RW_DIM, 2 * RW_DIM, 3 * RW_DIM, 3 * RW_DIM + RW_LORA_W,
                                        3 * RW_DIM + RW_LORA_W + RW_LORA_A], axis=-1)
    w_log = -jax.nn.softplus(-(w0 + jnp.tanh(lw) @ w_up).astype(F32)) - 0.5
    decay = jnp.exp(-jnp.exp(w_log))
    a = jax.nn.sigmoid((a0 + la @ a_up).astype(F32))
    g = (jax.nn.sigmoid(lg) @ g_up).astype(F32)
    hd = lambda t: split_heads(t.astype(F32), RW_HEADS)
    kk = hd(k * k_k)
    kk = kk / jnp.maximum(jnp.sqrt(jnp.sum(kk * kk, -1, keepdims=True)), 1e-12)
    k_mod = k.astype(F32) * (1.0 + (a - 1.0) * k_a)
    r_h, k_h, v_h = hd(r), hd(k_mod), hd(v)
    y = rwkv7_recurrence(r_h, hd(decay), k_h, v_h, -kk, kk * hd(a))
    y = head_norm(y.reshape(B, S, RW_DIM), RW_HEADS, RW_GN_EPS) * ln_w + ln_b
    bonus = jnp.sum(r_h * k_h * hd(r_k[None, None, :]), -1, keepdims=True) * v_h
    return ((y + bonus.reshape(B, S, RW_DIM)) * g).astype(p.dtype)


def sliding_window_attention(q, k, v, sinks):
    B, S, Hq, d = q.shape
    Hkv = k.shape[2]
    G = Hq // Hkv
    W = WINDOW
    nb = S // W
    qb = q.reshape(B, nb, W, Hkv, G, d)
    kb = k.astype(F32).reshape(B, nb, W, Hkv, d)
    vb = v.astype(F32).reshape(B, nb, W, Hkv, d)
    pad = ((0, 0), (1, 0), (0, 0), (0, 0), (0, 0))
    kw = jnp.concatenate([jnp.pad(kb, pad)[:, :-1], kb], axis=2)
    vw = jnp.concatenate([jnp.pad(vb, pad)[:, :-1], vb], axis=2)
    scores = jnp.einsum('bnqhgd,bnkhd->bnhgqk', qb, kw) * (d ** -0.5)
    qi = jnp.arange(W)[:, None]
    kj = jnp.arange(2 * W)[None, :]
    band = (kj > qi) & (kj <= qi + W)
    first = (jnp.arange(nb)[:, None, None] == 0) & (kj[None] < W)
    valid = band[None] & ~first
    scores = jnp.where(valid[None, :, None, None], scores, -jnp.inf)
    sink = sinks.astype(F32).reshape(Hkv, G)[None, None, :, :, None, None]
    m = jnp.maximum(scores.max(-1, keepdims=True), sink)
    pr = jnp.exp(scores - m)
    probs = pr / (pr.sum(-1, keepdims=True) + jnp.exp(sink - m))
    out = jnp.einsum('bnhgqk,bnkhd->bnqhgd', probs, vw)
    return out.reshape(B, S, Hq, d)


def swa_mixer(p, sinks):
    B, S, _ = p.shape
    q, k, v = jnp.split(p, [SW_DIM, SW_DIM + SW_KV_DIM], axis=-1)
    q = rope(split_heads(q, SW_HEADS))
    k = rope(split_heads(k, SW_KV_HEADS))
    v = split_heads(v, SW_KV_HEADS)
    return sliding_window_attention(q, k, v, sinks).reshape(B, S, SW_DIM).astype(p.dtype)


def mlstm_chunkwise(q, k, v, i_pre, f_pre):
    B, S, H, d = q.shape
    L = ML_CHUNK
    n = S // L
    chunk4 = lambda t: t.astype(F32).reshape(B, n, L, H, d).transpose(1, 0, 3, 2, 4)
    chunk3 = lambda t: t.astype(F32).reshape(B, n, L, H).transpose(1, 0, 3, 2)
    qc, kc, vc = chunk4(q), chunk4(k * (d ** -0.5)), chunk4(v)
    ic = chunk3(i_pre)
    lfc = chunk3(jax.nn.log_sigmoid(f_pre.astype(F32)))
    causal = jnp.tril(jnp.ones((L, L), bool))

    def step(carry, inp):
        C, nv, m = carry
        q_, k_, v_, i_, lf = inp
        bc = jnp.cumsum(lf, axis=-1)
        D = jnp.where(causal, bc[..., :, None] - bc[..., None, :] + i_[..., None, :], -jnp.inf)
        m_inter = bc + m[..., None]
        m_t = jnp.maximum(m_inter, D.max(-1))
        inter = jnp.exp(m_inter - m_t)
        Sm = jnp.einsum('bhld,bhsd->bhls', q_, k_) * jnp.exp(D - m_t[..., None])
        num = jnp.einsum('bhls,bhsv->bhlv', Sm, v_) + inter[..., None] * jnp.einsum('bhld,bhdv->bhlv', q_, C)
        nq = Sm.sum(-1) + inter * jnp.einsum('bhld,bhd->bhl', q_, nv)
        h = num / jnp.maximum(jnp.abs(nq), jnp.exp(-m_t))[..., None]
        m_new = m_t[..., -1]
        wst = jnp.exp(bc[..., -1:] - bc + i_ - m_new[..., None])
        dec = jnp.exp(bc[..., -1] + m - m_new)
        C = dec[..., None, None] * C + jnp.einsum('bhs,bhsd,bhsv->bhdv', wst, k_, v_)
        nv = dec[..., None] * nv + jnp.einsum('bhs,bhsd->bhd', wst, k_)
        return (C, nv, m_new), h

    init = (jnp.zeros((B, H, d, d), F32), jnp.zeros((B, H, d), F32), jnp.zeros((B, H), F32))
    _, hs = lax.scan(step, init, (qc, kc, vc, ic, lfc))
    return hs.transpose(1, 0, 3, 2, 4).reshape(B, S, H, d)


def mlstm_mixer(p, conv_w, conv_b, norm_w):
    B, S, _ = p.shape
    qk, v, o, i_pre, f_pre = jnp.split(p, [2 * ML_DIM, 3 * ML_DIM, 4 * ML_DIM, 4 * ML_DIM + ML_HEADS], axis=-1)
    qk = jax.nn.silu(causal_depthwise_conv(qk, conv_w, conv_b))
    q, k = jnp.split(qk, 2, axis=-1)
    h = mlstm_chunkwise(split_heads(q, ML_HEADS), split_heads(k, ML_HEADS), split_heads(v, ML_HEADS), i_pre, f_pre)
    h = head_norm(h.reshape(B, S, ML_DIM), ML_HEADS, ML_GN_EPS) * norm_w
    return (jax.nn.sigmoid(o.astype(F32)) * h).astype(p.dtype)


def moe_ffn(x2d, router_w, router_b, w_up, b_up, w_down, b_down):
    M, D = x2d.shape
    logits = (x2d @ router_w + router_b).astype(F32)
    top_val, top_idx = lax.top_k(logits, TOP_K)
    gates = jax.nn.softmax(top_val, axis=-1)
    A = M * TOP_K
    e_flat = top_idx.reshape(A).astype(jnp.int32)
    tok_flat = jnp.arange(A, dtype=jnp.int32) // TOP_K
    g_flat = gates.reshape(A)
    order = jnp.argsort(e_flat)
    e_s, tok_s, g_s = e_flat[order], tok_flat[order], g_flat[order]
    counts = jnp.zeros((N_EXPERTS,), jnp.int32).at[e_flat].add(1)
    padded = (counts + MOE_BLOCK - 1) // MOE_BLOCK * MOE_BLOCK
    start = jnp.cumsum(counts) - counts
    pend = jnp.cumsum(padded)
    pstart = pend - padded
    dest = pstart[e_s] + (jnp.arange(A, dtype=jnp.int32) - start[e_s])
    n_blocks = -(-(A + N_EXPERTS * (MOE_BLOCK - 1)) // MOE_BLOCK)
    P = n_blocks * MOE_BLOCK
    tok_buf = jnp.zeros((P,), jnp.int32).at[dest].set(tok_s)
    g_buf = jnp.zeros((P,), F32).at[dest].set(g_s)
    block_e = jnp.minimum(jnp.searchsorted(pend, jnp.arange(n_blocks, dtype=jnp.int32) * MOE_BLOCK, side='right'),
                          N_EXPERTS - 1).astype(jnp.int32)
    xb = x2d[tok_buf].reshape(n_blocks, MOE_BLOCK, D)

    def expert_block(args):
        xblk, e = args
        h = xblk @ w_up[e] + b_up[e]
        h_glu = jnp.minimum(h[..., ::2], SWIGLU_LIMIT)
        h_lin = jnp.clip(h[..., 1::2], -SWIGLU_LIMIT, SWIGLU_LIMIT)
        act = h_glu * jax.nn.sigmoid(SWIGLU_ALPHA * h_glu) * (h_lin + 1.0)
        return act @ w_down[e] + b_down[e]

    yb = lax.map(expert_block, (xb, block_e)).reshape(P, D)
    y = jnp.zeros((M, D), F32).at[tok_buf].add(yb.astype(F32) * g_buf[:, None])
    return y.astype(x2d.dtype)


def hybrid_mixer(x, w_in, b_in, rw_shift_mu, rw_w_up, rw_w0, rw_a_up, rw_a0, rw_g_up, rw_k_k, rw_k_a, rw_r_k,
                 rw_ln_w, rw_ln_b, sw_sinks, ml_conv_w, ml_conv_b, ml_norm_w, w_out):
    p = x @ w_in + b_in
    p_rw, p_sw, p_ml = jnp.split(p, [RW_COLS, RW_COLS + SW_COLS], axis=-1)
    y = jnp.concatenate([
        rwkv7_mixer(p_rw, rw_shift_mu, rw_w_up, rw_w0, rw_a_up, rw_a0, rw_g_up, rw_k_k, rw_k_a, rw_r_k, rw_ln_w, rw_ln_b),
        swa_mixer(p_sw, sw_sinks),
        mlstm_mixer(p_ml, ml_conv_w, ml_conv_b, ml_norm_w),
    ], axis=-1)
    return y @ w_out


def setup_inputs(seed: int = 0) -> dict:
    key = jax.random.key(seed)
    ks = iter(jax.random.split(key, 40))
    nrm = lambda shape, scale: jax.random.normal(next(ks), shape, F32) * scale
    L = DEPTH
    x = nrm((BATCH, SEQ, D_MODEL), 1.0)
    w_in = nrm((L, D_MODEL, C_IN), D_MODEL ** -0.5)
    b_in = nrm((L, C_IN), 0.02)
    b_in = b_in.at[:, C_IN - ML_HEADS:].add(jnp.linspace(3.0, 6.0, ML_HEADS))
    rw_shift_mu = jax.random.uniform(next(ks), (L, RW_COLS), F32)
    rw_w_up = nrm((L, RW_LORA_W, RW_DIM), RW_LORA_W ** -0.5)
    rw_w0 = jax.random.uniform(next(ks), (L, RW_DIM), F32, -6.0, -0.5)
    rw_a_up = nrm((L, RW_LORA_A, RW_DIM), 0.5 * RW_LORA_A ** -0.5)
    rw_a0 = nrm((L, RW_DIM), 0.1)
    rw_g_up = nrm((L, RW_LORA_G, RW_DIM), 2.0 * RW_LORA_G ** -0.5)
    rw_k_k = 0.85 + nrm((L, RW_DIM), 0.05)
    rw_k_a = 1.0 + nrm((L, RW_DIM), 0.05)
    rw_r_k = nrm((L, RW_DIM), 0.1)
    rw_ln_w = 1.0 + nrm((L, RW_DIM), 0.05)
    rw_ln_b = nrm((L, RW_DIM), 0.02)
    sw_sinks = nrm((L, SW_HEADS), 1.0)
    ml_conv_w = nrm((L, CONV_WIDTH, 2 * ML_DIM), CONV_WIDTH ** -0.5)
    ml_conv_b = nrm((L, 2 * ML_DIM), 0.02)
    ml_norm_w = 1.0 + nrm((L, ML_DIM), 0.05)
    w_out = nrm((L, D_MIX, D_MODEL), DN_BETA * D_MIX ** -0.5)
    ln1_w = 1.0 + nrm((L, D_MODEL), 0.05)
    ln1_b = nrm((L, D_MODEL), 0.02)
    router_w = nrm((L, D_MODEL, N_EXPERTS), D_MODEL ** -0.5)
    router_b = nrm((L, N_EXPERTS), 0.01)
    exp_w_up = nrm((L, N_EXPERTS, D_MODEL, 2 * D_EXPERT), D_MODEL ** -0.5)
    exp_b_up = nrm((L, N_EXPERTS, 2 * D_EXPERT), 0.02)
    exp_w_down = nrm((L, N_EXPERTS, D_EXPERT, D_MODEL), DN_BETA * D_EXPERT ** -0.5)
    exp_b_down = nrm((L, N_EXPERTS, D_MODEL), 0.02)
    ln2_w = 1.0 + nrm((L, D_MODEL), 0.05)
    ln2_b = nrm((L, D_MODEL), 0.02)
    return {'x': x, 'w_in': w_in, 'b_in': b_in, 'rw_shift_mu': rw_shift_mu, 'rw_w_up': rw_w_up, 'rw_w0': rw_w0,
            'rw_a_up': rw_a_up, 'rw_a0': rw_a0, 'rw_g_up': rw_g_up, 'rw_k_k': rw_k_k, 'rw_k_a': rw_k_a,
            'rw_r_k': rw_r_k, 'rw_ln_w': rw_ln_w, 'rw_ln_b': rw_ln_b, 'sw_sinks': sw_sinks, 'ml_conv_w': ml_conv_w,
            'ml_conv_b': ml_conv_b, 'ml_norm_w': ml_norm_w, 'w_out': w_out, 'ln1_w': ln1_w, 'ln1_b': ln1_b,
            'router_w': router_w, 'router_b': router_b, 'exp_w_up': exp_w_up, 'exp_b_up': exp_b_up,
            'exp_w_down': exp_w_down, 'exp_b_down': exp_b_down, 'ln2_w': ln2_w, 'ln2_b': ln2_b}


def reference(x, w_in, b_in, rw_shift_mu, rw_w_up, rw_w0, rw_a_up, rw_a0, rw_g_up, rw_k_k, rw_k_a, rw_r_k,
              rw_ln_w, rw_ln_b, sw_sinks, ml_conv_w, ml_conv_b, ml_norm_w, w_out, ln1_w, ln1_b, router_w, router_b,
              exp_w_up, exp_b_up, exp_w_down, exp_b_down, ln2_w, ln2_b):
    h = x
    B, S, D = x.shape
    for l in range(DEPTH):
        mix = hybrid_mixer(h, w_in[l], b_in[l], rw_shift_mu[l], rw_w_up[l], rw_w0[l], rw_a_up[l], rw_a0[l],
                           rw_g_up[l], rw_k_k[l], rw_k_a[l], rw_r_k[l], rw_ln_w[l], rw_ln_b[l], sw_sinks[l],
                           ml_conv_w[l], ml_conv_b[l], ml_norm_w[l], w_out[l])
        h = layer_norm(DN_ALPHA * h + mix, ln1_w[l], ln1_b[l])
        ffn = moe_ffn(h.reshape(B * S, D), router_w[l], router_b[l], exp_w_up[l], exp_b_up[l],
                      exp_w_down[l], exp_b_down[l]).reshape(B, S, D)
        h = layer_norm(DN_ALPHA * h + ffn, ln2_w[l], ln2_b[l])
    return h
```

```python
import functools

import numpy as np
import jax
import jax.numpy as jnp
from jax import lax
from jax.experimental import pallas as pl
from jax.experimental.pallas import tpu as pltpu

F32 = jnp.float32
BF16 = jnp.bfloat16

D_MODEL = 1024
HEAD_DIM = 64
RW_HEADS = 6
SW_HEADS = 6
SW_KV_HEADS = 2
ML_HEADS = 4
RW_DIM = RW_HEADS * HEAD_DIM
SW_DIM = SW_HEADS * HEAD_DIM
SW_KV_DIM = SW_KV_HEADS * HEAD_DIM
ML_DIM = ML_HEADS * HEAD_DIM
RW_LORA_W = 64
RW_LORA_A = 64
RW_LORA_G = 128
RW_COLS = 3 * RW_DIM + RW_LORA_W + RW_LORA_A + RW_LORA_G
SW_COLS = SW_DIM + 2 * SW_KV_DIM
ML_COLS = 4 * ML_DIM + 2 * ML_HEADS
C_IN = RW_COLS + SW_COLS + ML_COLS
WINDOW = 128
ROPE_THETA = 10000.0
CONV_WIDTH = 4
N_EXPERTS = 32
TOP_K = 4
D_EXPERT = 1024
SWIGLU_ALPHA = 1.702
SWIGLU_LIMIT = 7.0
LN_EPS = 1e-5
RW_GN_EPS = 64e-5
ML_GN_EPS = 1e-6
DEPTH = 4
DN_ALPHA = (2 * DEPTH) ** 0.25

LANES = 128
SUBLANES = 8
ML_COLS_PAD = 1152
C_IN_PAD = RW_COLS + SW_COLS + ML_COLS_PAD
ROW_TILE = 512
RW_CHUNK = 64
RW_PAIR = 2 * HEAD_DIM
RW_NPAIR = RW_HEADS // 2
ML_CHUNK = 256
MOE_BLOCK = 512
COPY_ROWS = 512
NEG_BIG = -1e30
VMEM_LIMIT = 56 * 1024 * 1024


def _cparams(*sem):
    return pltpu.CompilerParams(dimension_semantics=sem, vmem_limit_bytes=VMEM_LIMIT)


def _dot(a, b):
    return jnp.dot(a.astype(BF16), b.astype(BF16), preferred_element_type=F32)


def _dot_nt(a, b):
    return lax.dot_general(a.astype(BF16), b.astype(BF16), (((1,), (1,)), ((), ())),
                           preferred_element_type=F32)


def _split3(x):
    hi = x.astype(BF16)
    r1 = x - hi.astype(F32)
    mid = r1.astype(BF16)
    lo = (r1 - mid.astype(F32)).astype(BF16)
    return hi, mid, lo


def _dot_xl(x, w):
    hi, mid, lo = _split3(x)
    return (jnp.dot(hi, w, preferred_element_type=F32) + jnp.dot(mid, w, preferred_element_type=F32)
            + jnp.dot(lo, w, preferred_element_type=F32))


def _dot_lx(w, x):
    hi, mid, lo = _split3(x)
    return (jnp.dot(w, hi, preferred_element_type=F32) + jnp.dot(w, mid, preferred_element_type=F32)
            + jnp.dot(w, lo, preferred_element_type=F32))


def _softplus(z):
    return jnp.maximum(z, 0.0) + jnp.log(1.0 + jnp.exp(-jnp.abs(z)))


def _sigmoid(z):
    return 1.0 / (1.0 + jnp.exp(-z))


def _shift_rows(x, prev8, j):
    cat = jnp.concatenate([prev8, x], axis=0)
    return pltpu.roll(cat, j, 0)[SUBLANES:, :]


def _layer_norm(x, w, b):
    mu = jnp.mean(x, axis=-1, keepdims=True)
    d = x - mu
    var = jnp.mean(d * d, axis=-1, keepdims=True)
    return d * lax.rsqrt(var + LN_EPS) * w + b


def _head_norm(y, ones_bd, eps):
    inv = 1.0 / HEAD_DIM
    mu = _dot_xl(y, ones_bd) * inv
    d = y - mu
    var = _dot_xl(d * d, ones_bd) * inv
    return d * lax.rsqrt(var + eps)


def _inproj_kernel(x_ref, w_ref, b_ref, rw_ref, sw_ref, ml_ref):
    x = x_ref[...].astype(BF16)
    c1, c2 = RW_COLS, RW_COLS + SW_COLS
    rw_ref[...] = jnp.dot(x, w_ref[:, 0:c1], preferred_element_type=F32) + b_ref[:, 0:c1]
    sw_ref[...] = jnp.dot(x, w_ref[:, c1:c2], preferred_element_type=F32) + b_ref[:, c1:c2]
    ml_ref[...] = jnp.dot(x, w_ref[:, c2:C_IN_PAD], preferred_element_type=F32) + b_ref[:, c2:C_IN_PAD]


def _in_proj(h2d, w, b):
    m = h2d.shape[0]
    tm = ROW_TILE
    row = lambda i: (i, 0)
    fixed = lambda i: (0, 0)
    return pl.pallas_call(
        _inproj_kernel,
        grid=(m // tm,),
        in_specs=[pl.BlockSpec((tm, D_MODEL), row), pl.BlockSpec((D_MODEL, C_IN_PAD), fixed),
                  pl.BlockSpec((1, C_IN_PAD), fixed)],
        out_specs=[pl.BlockSpec((tm, RW_COLS), row), pl.BlockSpec((tm, SW_COLS), row),
                   pl.BlockSpec((tm, ML_COLS_PAD), row)],
        out_shape=[jax.ShapeDtypeStruct((m, RW_COLS), F32), jax.ShapeDtypeStruct((m, SW_COLS), F32),
                   jax.ShapeDtypeStruct((m, ML_COLS_PAD), F32)],
        compiler_params=_cparams("parallel"),
        name="in_proj",
    )(h2d, w, b)


def _rw_prep_kernel(tiles_per_seq, p_ref, pprev_ref, mu_ref, wcomb_ref, gup_ref, w0_ref, a0_ref, kk_ref,
                    ka_ref, ones_ref, r_o, lw_o, k_o, v_o, a_o, b_o, g_o):
    i = pl.program_id(0)
    x = p_ref[...]
    first = (i % tiles_per_seq) == 0
    prev8 = jnp.where(first, 0.0, pprev_ref[...])
    xs = x + (_shift_rows(x, prev8, 1) - x) * mu_ref[...]
    d = RW_DIM
    r = xs[:, 0:d]
    k = xs[:, d:2 * d]
    v = xs[:, 2 * d:3 * d]
    slab = xs[:, 3 * d:3 * d + LANES]
    lane = lax.broadcasted_iota(jnp.int32, slab.shape, 1)
    slab = jnp.where(lane < RW_LORA_W, jnp.tanh(slab), slab)
    wa = _dot(slab, wcomb_ref[...])
    w_pre = w0_ref[...] + wa[:, 0:d]
    a_pre = a0_ref[...] + wa[:, d:2 * d]
    w_log = -_softplus(-w_pre) - 0.5
    log_decay = -jnp.exp(w_log)
    a_sig = _sigmoid(a_pre)
    g = _dot(_sigmoid(xs[:, 3 * d + LANES:3 * d + 2 * LANES]), gup_ref[...])
    kk = k * kk_ref[...]
    ss = _dot_xl(kk * kk, ones_ref[...])
    kk = kk / jnp.maximum(jnp.sqrt(ss), 1e-12)
    r_o[...] = r
    lw_o[...] = log_decay
    k_o[...] = k * (1.0 + (a_sig - 1.0) * ka_ref[...])
    v_o[...] = v
    a_o[...] = -kk
    b_o[...] = kk * a_sig
    g_o[...] = g


def _rw_prep(p_rw, seq, mu, wcomb, gup, w0, a0, k_k, k_a, ones_bd):
    m = p_rw.shape[0]
    tm = ROW_TILE
    row = lambda i: (i, 0)
    fixed = lambda i: (0, 0)
    prev = lambda i: (jnp.maximum(i * (tm // SUBLANES) - 1, 0), 0)
    vec = pl.BlockSpec((1, RW_DIM), fixed)
    out = jax.ShapeDtypeStruct((m, RW_DIM), F32)
    return pl.pallas_call(
        functools.partial(_rw_prep_kernel, seq // tm),
        grid=(m // tm,),
        in_specs=[pl.BlockSpec((tm, RW_COLS), row), pl.BlockSpec((SUBLANES, RW_COLS), prev),
                  pl.BlockSpec((1, RW_COLS), fixed), pl.BlockSpec((LANES, 2 * RW_DIM), fixed),
                  pl.BlockSpec((RW_LORA_G, RW_DIM), fixed), vec, vec, vec, vec,
                  pl.BlockSpec((RW_DIM, RW_DIM), fixed)],
        out_specs=[pl.BlockSpec((tm, RW_DIM), row)] * 7,
        out_shape=[out] * 7,
        compiler_params=_cparams("parallel"),
        name="rw_prep",
    )(p_rw, p_rw, mu, wcomb, gup, w0, a0, k_k, k_a, ones_bd)


def _rw_chunk_kernel(n_chunk, r_ref, lw_ref, k_ref, v_ref, a_ref, b_ref, m_o, yh_o, g_o, y0_o):
    L = RW_CHUNK
    ti = lax.broadcasted_iota(jnp.int32, (L, L), 0)
    si = lax.broadcasted_iota(jnp.int32, (L, L), 1)
    tri = jnp.where(ti >= si, 1.0, 0.0).astype(BF16)
    n2 = 2 * L
    ri = lax.broadcasted_iota(jnp.int32, (n2, n2), 0)
    ci = lax.broadcasted_iota(jnp.int32, (n2, n2), 1)
    rr = jnp.where(ri >= L, ri - L, ri)
    cc = jnp.where(ci >= L, ci - L, ci)
    strict = rr > cc
    incl = rr >= cc
    eye = ri == ci
    eye_f = jnp.where(eye, 1.0, 0.0)
    lane = lax.broadcasted_iota(jnp.int32, (L, RW_PAIR), 1)
    head0 = lane < HEAD_DIM

    def stack(x):
        return jnp.concatenate([jnp.where(head0, x, 0.0), jnp.where(head0, 0.0, x)], axis=0)

    def body(c, carry):
        rows = pl.ds(pl.multiple_of(c * L, L), L)
        lc_all = _dot_lx(tri, lw_ref[rows, :])
        for p in range(RW_NPAIR):
            cols = slice(p * RW_PAIR, (p + 1) * RW_PAIR)
            lw = lw_ref[rows, cols]
            lc = lc_all[:, cols]
            lcl = lc[L - 1:L, :]
            gam = jnp.exp(lc)
            gam_prev = jnp.exp(lc - lw)
            inv = jnp.exp(-lc)
            to_end = jnp.exp(lcl - lc)
            r = r_ref[rows, cols]
            k = k_ref[rows, cols]
            v = v_ref[rows, cols]
            a = a_ref[rows, cols]
            b = b_ref[rows, cols]
            x1 = stack(a * gam_prev)
            r1 = stack(r * gam)
            x2 = stack(b * inv)
            k2 = stack(k * inv)
            vs = stack(v)
            bs = stack(b * to_end)
            ks = stack(k * to_end)
            aa = _dot_nt(jnp.concatenate([x1, r1], axis=0), jnp.concatenate([x2, k2], axis=0))
            a_ab = jnp.where(strict, aa[0:n2, 0:n2], 0.0)
            a_ak = jnp.where(strict, aa[0:n2, n2:2 * n2], 0.0)
            a_rb = jnp.where(incl, aa[n2:2 * n2, 0:n2], 0.0)
            a_rk = jnp.where(incl, aa[n2:2 * n2, n2:2 * n2], 0.0)
            apow = a_ab
            t_inv = eye_f + a_ab
            for _ in range(5):
                apow = _dot(apow, apow)
                t_inv = t_inv + _dot(t_inv, apow)
            akv = _dot(a_ak, vs)
            pw = _dot(t_inv, jnp.concatenate([x1, akv], axis=1))
            pmat = pw[:, 0:RW_PAIR]
            wmat = pw[:, RW_PAIR:2 * RW_PAIR]
            gl_row = jnp.exp(lcl)
            m_mat = _dot(bs.T, pmat) + jnp.where(eye, gl_row, 0.0)
            g_mat = _dot(jnp.concatenate([bs, ks], axis=0).T, jnp.concatenate([wmat, vs], axis=0))
            yy = _dot(a_rb, pw)
            yh = r1 + yy[:, 0:RW_PAIR]
            y0 = yy[:, RW_PAIR:2 * RW_PAIR] + _dot(a_rk, vs)
            m_o[c, p] = m_mat.astype(BF16)
            yh_o[c, p] = yh.astype(BF16)
            g_o[c, p] = g_mat
            y0_o[c, p] = y0
        return carry

    lax.fori_loop(0, n_chunk, body, 0)


def _rw_chunks(r, lw, k, v, a, b):
    m = r.shape[0]
    tm = ROW_TILE
    nc = tm // RW_CHUNK
    row = lambda i: (i, 0)
    blk = lambda i: (i, 0, 0, 0)
    n2 = 2 * RW_CHUNK
    n_tot = m // RW_CHUNK
    ospec = pl.BlockSpec((nc, RW_NPAIR, n2, RW_PAIR), blk)
    return pl.pallas_call(
        functools.partial(_rw_chunk_kernel, nc),
        grid=(m // tm,),
        in_specs=[pl.BlockSpec((tm, RW_DIM), row)] * 6,
        out_specs=[ospec] * 4,
        out_shape=[jax.ShapeDtypeStruct((n_tot, RW_NPAIR, n2, RW_PAIR), BF16),
                   jax.ShapeDtypeStruct((n_tot, RW_NPAIR, n2, RW_PAIR), BF16),
                   jax.ShapeDtypeStruct((n_tot, RW_NPAIR, n2, RW_PAIR), F32),
                   jax.ShapeDtypeStruct((n_tot, RW_NPAIR, n2, RW_PAIR), F32)],
        compiler_params=_cparams("parallel"),
        name="rw_chunks",
    )(r, lw, k, v, a, b)


def _rw_scan_kernel(batch, m_ref, yh_ref, g_ref, y0_ref, y_o, h_ref):
    c = pl.program_id(0)

    @pl.when(c == 0)
    def _():
        h_ref[...] = jnp.zeros_like(h_ref)

    L = RW_CHUNK
    for bi in range(batch):
        for p in range(RW_NPAIR):
            hb = h_ref[bi, p].astype(BF16)
            yrows = jnp.dot(yh_ref[bi, 0, p], hb, preferred_element_type=F32) + y0_ref[bi, 0, p]
            y_o[bi, :, p * RW_PAIR:(p + 1) * RW_PAIR] = yrows[0:L, :] + yrows[L:2 * L, :]
            h_ref[bi, p] = jnp.dot(m_ref[bi, 0, p], hb, preferred_element_type=F32) + g_ref[bi, 0, p]


def _rw_scan(batch, seq, m_mat, yh, g_mat, y0):
    nc = seq // RW_CHUNK
    n2 = 2 * RW_CHUNK
    shp = (batch, nc, RW_NPAIR, n2, RW_PAIR)
    args = [t.reshape(shp) for t in (m_mat, yh, g_mat, y0)]
    ispec = pl.BlockSpec((batch, 1, RW_NPAIR, n2, RW_PAIR), lambda c: (0, c, 0, 0, 0))
    return pl.pallas_call(
        functools.partial(_rw_scan_kernel, batch),
        grid=(nc,),
        in_specs=[ispec] * 4,
        out_specs=pl.BlockSpec((batch, RW_CHUNK, RW_DIM), lambda c: (0, c, 0)),
        out_shape=jax.ShapeDtypeStruct((batch, seq, RW_DIM), F32),
        scratch_shapes=[pltpu.VMEM((batch, RW_NPAIR, RW_PAIR, RW_PAIR), F32)],
        compiler_params=_cparams("arbitrary"),
        name="rw_scan",
    )(*args)


def _rw_post_kernel(y_ref, r_ref, k_ref, v_ref, g_ref, rk_ref, lnw_ref, lnb_ref, ones_ref, o_ref):
    ones_bd = ones_ref[...]
    y = _head_norm(y_ref[...], ones_bd, RW_GN_EPS) * lnw_ref[...] + lnb_ref[...]
    bonus = _dot_xl(r_ref[...] * k_ref[...] * rk_ref[...], ones_bd) * v_ref[...]
    o_ref[...] = (y + bonus) * g_ref[...]


def _rw_post(y, r, k, v, g, r_k, ln_w, ln_b, ones_bd):
    m = y.shape[0]
    tm = ROW_TILE
    row = lambda i: (i, 0)
    fixed = lambda i: (0, 0)
    vec = pl.BlockSpec((1, RW_DIM), fixed)
    return pl.pallas_call(
        _rw_post_kernel,
        grid=(m // tm,),
        in_specs=[pl.BlockSpec((tm, RW_DIM), row)] * 5 + [vec, vec, vec, pl.BlockSpec((RW_DIM, RW_DIM), fixed)],
        out_specs=pl.BlockSpec((tm, RW_DIM), row),
        out_shape=jax.ShapeDtypeStruct((m, RW_DIM), F32),
        compiler_params=_cparams("parallel"),
        name="rw_post",
    )(y, r, k, v, g, r_k, ln_w, ln_b, ones_bd)


def _rope(x, cos, sin_signed):
    lane = lax.broadcasted_iota(jnp.int32, x.shape, 1)
    half = HEAD_DIM // 2
    first_half = (lane % HEAD_DIM) < half
    rot = jnp.where(first_half, pltpu.roll(x, LANES - half, 1), pltpu.roll(x, half, 1))
    return x * cos + rot * sin_signed


def _swa_kernel(sink_ref, q_ref, kc_ref, vc_ref, kp_ref, vp_ref, cosc_ref, sinc_ref, cosp_ref, sinp_ref, o_ref):
    j = pl.program_id(1)
    W = WINDOW
    cos_c, sin_c = cosc_ref[...], sinc_ref[...]
    k_cat = jnp.concatenate([_rope(kp_ref[0], cosp_ref[...], sinp_ref[...]), _rope(kc_ref[0], cos_c, sin_c)], axis=0)
    v_cat = jnp.concatenate([vp_ref[0], vc_ref[0]], axis=0).astype(BF16)
    k_cat = k_cat.astype(BF16)
    qi = lax.broadcasted_iota(jnp.int32, (W, 2 * W), 0)
    kj = lax.broadcasted_iota(jnp.int32, (W, 2 * W), 1)
    lo = jnp.where(j > 0, qi, jnp.maximum(qi, W - 1))
    bias = jnp.where((kj > lo) & (kj <= qi + W), 0.0, NEG_BIG)
    lane = lax.broadcasted_iota(jnp.int32, (W, LANES), 1)
    kv0 = lane < HEAD_DIM
    scale = HEAD_DIM ** -0.5
    group = SW_HEADS // SW_KV_HEADS
    for g in range(group):
        q = _rope(q_ref[0, :, g * LANES:(g + 1) * LANES], cos_c, sin_c) * scale
        outs = []
        for kv in range(SW_KV_HEADS):
            qm = jnp.where(kv0, q, 0.0) if kv == 0 else jnp.where(kv0, 0.0, q)
            s = _dot_nt(qm, k_cat) + bias
            sink = sink_ref[kv * group + g]
            mx = jnp.maximum(jnp.max(s, axis=-1, keepdims=True), sink)
            pr = jnp.exp(s - mx)
            den = jnp.sum(pr, axis=-1, keepdims=True) + jnp.exp(sink - mx)
            outs.append(jnp.dot(pr.astype(BF16), v_cat, preferred_element_type=F32) / den)
        o_ref[0, :, g * LANES:(g + 1) * LANES] = jnp.where(kv0, outs[0], outs[1])


def _swa(p_sw3, sinks, cos_t, sin_t):
    batch, seq, _ = p_sw3.shape
    W = WINDOW
    nb = seq // W
    kcol = SW_DIM // LANES
    cur = lambda c: (lambda b, j: (b, j, c))
    prv = lambda c: (lambda b, j: (b, jnp.maximum(j - 1, 0), c))
    tab_c = pl.BlockSpec((W, LANES), lambda b, j: (j, 0))
    tab_p = pl.BlockSpec((W, LANES), lambda b, j: (jnp.maximum(j - 1, 0), 0))
    kv_blk = lambda f: pl.BlockSpec((1, W, LANES), f)
    return pl.pallas_call(
        _swa_kernel,
        grid=(batch, nb),
        in_specs=[pl.BlockSpec(memory_space=pltpu.SMEM),
                  pl.BlockSpec((1, W, SW_DIM), cur(0)),
                  kv_blk(cur(kcol)), kv_blk(cur(kcol + 1)), kv_blk(prv(kcol)), kv_blk(prv(kcol + 1)),
                  tab_c, tab_c, tab_p, tab_p],
        out_specs=pl.BlockSpec((1, W, SW_DIM), cur(0)),
        out_shape=jax.ShapeDtypeStruct((batch, seq, SW_DIM), F32),
        compiler_params=_cparams("parallel", "parallel"),
        name="swa",
    )(sinks, p_sw3, p_sw3, p_sw3, p_sw3, p_sw3, cos_t, sin_t, cos_t, sin_t)


def _mlstm_kernel(p_ref, pprev_ref, cw_ref, cb_ref, nw_ref, expand_ref, ones_ref, o_ref, c_ref, nv_ref, m_ref):
    j = pl.program_id(1)
    L = ML_CHUNK
    d = ML_DIM

    @pl.when(j == 0)
    def _():
        c_ref[...] = jnp.zeros_like(c_ref)
        nv_ref[...] = jnp.zeros_like(nv_ref)
        m_ref[...] = jnp.zeros_like(m_ref)

    x = p_ref[0]
    qk_pre = x[:, 0:2 * d]
    prev8 = jnp.where(j == 0, 0.0, pprev_ref[0][:, 0:2 * d])
    conv = cb_ref[...] + cw_ref[CONV_WIDTH - 1:CONV_WIDTH, :] * qk_pre
    for s in range(1, CONV_WIDTH):
        conv = conv + cw_ref[CONV_WIDTH - 1 - s:CONV_WIDTH - s, :] * _shift_rows(qk_pre, prev8, s)
    qk = conv * _sigmoid(conv)
    q = qk[:, 0:d]
    k = qk[:, d:2 * d] * (HEAD_DIM ** -0.5)
    v = x[:, 2 * d:3 * d]
    o_gate = x[:, 3 * d:4 * d]
    gates = _dot_xl(x[:, 4 * d:4 * d + LANES], expand_ref[...])
    i_full = gates[:, 0:d]
    f_full = gates[:, d:2 * d]
    lf_full = jnp.minimum(f_full, 0.0) - jnp.log(1.0 + jnp.exp(-jnp.abs(f_full)))
    ti = lax.broadcasted_iota(jnp.int32, (L, L), 0)
    si = lax.broadcasted_iota(jnp.int32, (L, L), 1)
    causal = ti >= si
    tri = jnp.where(causal, 1.0, 0.0).astype(BF16)
    bc_full = _dot_lx(tri, lf_full)
    u_t = (i_full - bc_full).T
    lane = lax.broadcasted_iota(jnp.int32, (L, d), 1)
    lane_row = lax.broadcasted_iota(jnp.int32, (1, d), 1)
    q_b = q.astype(BF16)
    k_b = k.astype(BF16)
    m_prev_row = m_ref[...]
    nv_row = nv_ref[...]
    qn = q * nv_row
    num = jnp.zeros((L, d), F32)
    inter_full = jnp.zeros((L, d), F32)
    den_full = jnp.zeros((L, d), F32)
    mnew_row = jnp.zeros((1, d), F32)
    for h in range(ML_HEADS):
        c0 = h * HEAD_DIM
        in_head = (lane >= c0) & (lane < c0 + HEAD_DIM)
        in_head_row = (lane_row >= c0) & (lane_row < c0 + HEAD_DIM)
        bc_col = bc_full[:, c0:c0 + 1]
        dmat = jnp.where(causal, bc_col + u_t[c0:c0 + 1, :], NEG_BIG)
        m_inter = bc_col + m_prev_row[:, c0:c0 + 1]
        m_t = jnp.maximum(m_inter, jnp.max(dmat, axis=-1, keepdims=True))
        inter = jnp.exp(m_inter - m_t)
        e = jnp.exp(dmat - m_t)
        sm = _dot_nt(jnp.where(in_head, q, 0.0), k_b) * e
        num = num + _dot(sm, jnp.where(in_head, v, 0.0))
        nq = jnp.sum(sm, axis=-1, keepdims=True) + inter * jnp.sum(jnp.where(in_head, qn, 0.0), axis=-1, keepdims=True)
        den = jnp.maximum(jnp.abs(nq), jnp.exp(-m_t))
        inter_full = jnp.where(in_head, inter, inter_full)
        den_full = jnp.where(in_head, den, den_full)
        mnew_row = jnp.where(in_head_row, m_t[L - 1:L, :], mnew_row)
    c_mat = c_ref[...]
    num = num + inter_full * jnp.dot(q_b, c_mat.astype(BF16), preferred_element_type=F32)
    hout = num / den_full
    hn = _head_norm(hout, ones_ref[...], ML_GN_EPS) * nw_ref[...]
    o_ref[0] = _sigmoid(o_gate) * hn
    bcl_row = bc_full[L - 1:L, :]
    wst = jnp.exp(bcl_row - bc_full + i_full - mnew_row)
    dec_row = jnp.exp(bcl_row + m_prev_row - mnew_row)
    kw = k * wst
    ri = lax.broadcasted_iota(jnp.int32, (d, d), 0) // HEAD_DIM
    ci = lax.broadcasted_iota(jnp.int32, (d, d), 1) // HEAD_DIM
    c_ref[...] = dec_row * c_mat + jnp.where(ri == ci, _dot(kw.T, v), 0.0)
    nv_ref[...] = dec_row * nv_row + jnp.sum(kw, axis=0, keepdims=True)
    m_ref[...] = mnew_row


def _mlstm(p_ml3, conv_w, conv_b, norm_w, expand, ones_bd):
    batch, seq, _ = p_ml3.shape
    L = ML_CHUNK
    fixed = lambda b, j: (0, 0)
    return pl.pallas_call(
        _mlstm_kernel,
        grid=(batch, seq // L),
        in_specs=[pl.BlockSpec((1, L, ML_COLS_PAD), lambda b, j: (b, j, 0)),
                  pl.BlockSpec((1, SUBLANES, ML_COLS_PAD), lambda b, j: (b, jnp.maximum(j * (L // SUBLANES) - 1, 0), 0)),
                  pl.BlockSpec((CONV_WIDTH, 2 * ML_DIM), fixed), pl.BlockSpec((1, 2 * ML_DIM), fixed),
                  pl.BlockSpec((1, ML_DIM), fixed), pl.BlockSpec((LANES, 2 * ML_DIM), fixed),
                  pl.BlockSpec((ML_DIM, ML_DIM), fixed)],
        out_specs=pl.BlockSpec((1, L, ML_DIM), lambda b, j: (b, j, 0)),
        out_shape=jax.ShapeDtypeStruct((batch, seq, ML_DIM), F32),
        scratch_shapes=[pltpu.VMEM((ML_DIM, ML_DIM), F32), pltpu.VMEM((1, ML_DIM), F32),
                        pltpu.VMEM((1, ML_DIM), F32)],
        compiler_params=_cparams("parallel", "arbitrary"),
        name="mlstm",
    )(p_ml3, p_ml3, conv_w, conv_b, norm_w, expand, ones_bd)


def _outproj_kernel(rw_ref, sw_ref, ml_ref, h_ref, wrw_ref, wsw_ref, wml_ref, lnw_ref, lnb_ref,
                    rwh_ref, rwm_ref, rwl_ref, rb_ref, h1_o, idx_o, gate_o):
    mix = (_dot(rw_ref[...], wrw_ref[...]) + _dot(sw_ref[...], wsw_ref[...]) + _dot(ml_ref[...], wml_ref[...]))
    h1 = _layer_norm(DN_ALPHA * h_ref[...] + mix, lnw_ref[...], lnb_ref[...])
    h1_o[...] = h1
    xh, xm, xl = _split3(h1)
    wh, wm, wl = rwh_ref[...], rwm_ref[...], rwl_ref[...]
    dd = lambda a, b: jnp.dot(a, b, preferred_element_type=F32)
    logits = (dd(xh, wh) + (dd(xh, wm) + dd(xm, wh)) + (dd(xh, wl) + dd(xm, wm) + dd(xl, wh))) + rb_ref[...]
    lane = lax.broadcasted_iota(jnp.int32, logits.shape, 1).astype(F32)
    vals, idxs = [], []
    cur = logits
    for _ in range(TOP_K):
        mx = jnp.max(cur, axis=-1, keepdims=True)
        ix = jnp.min(jnp.where(cur == mx, lane, float(LANES)), axis=-1, keepdims=True)
        vals.append(mx)
        idxs.append(ix)
        cur = jnp.where(lane == ix, NEG_BIG * 2.0, cur)
    es = [jnp.exp(vv - vals[0]) for vv in vals]
    den = es[0] + es[1] + es[2] + es[3]
    idx_out = jnp.zeros_like(logits)
    gate_out = jnp.zeros_like(logits)
    for kk in range(TOP_K):
        idx_out = jnp.where(lane == float(kk), idxs[kk], idx_out)
        gate_out = jnp.where(lane == float(kk), es[kk] / den, gate_out)
    idx_o[...] = idx_out.astype(jnp.int32)
    gate_o[...] = gate_out


def _out_proj(rw_out, sw_out, ml_out, h2d, w_rw, w_sw, w_ml, ln_w, ln_b, rw_parts, rb):
    m = h2d.shape[0]
    tm = ROW_TILE
    row = lambda i: (i, 0)
    fixed = lambda i: (0, 0)
    full = lambda a: pl.BlockSpec(a.shape, fixed)
    return pl.pallas_call(
        _outproj_kernel,
        grid=(m // tm,),
        in_specs=[pl.BlockSpec((tm, RW_DIM), row), pl.BlockSpec((tm, SW_DIM), row), pl.BlockSpec((tm, ML_DIM), row),
                  pl.BlockSpec((tm, D_MODEL), row), full(w_rw), full(w_sw), full(w_ml), full(ln_w), full(ln_b),
                  full(rw_parts[0]), full(rw_parts[1]), full(rw_parts[2]), full(rb)],
        out_specs=[pl.BlockSpec((tm, D_MODEL), row), pl.BlockSpec((tm, LANES), row), pl.BlockSpec((tm, LANES), row)],
        out_shape=[jax.ShapeDtypeStruct((m, D_MODEL), F32), jax.ShapeDtypeStruct((m, LANES), jnp.int32),
                   jax.ShapeDtypeStruct((m, LANES), F32)],
        compiler_params=_cparams("parallel"),
        name="out_proj_router",
    )(rw_out, sw_out, ml_out, h2d, w_rw, w_sw, w_ml, ln_w, ln_b, *rw_parts, rb)


def _row_copy_kernel(si_ref, di_ref, src_ref, dst_ref, sem):
    n = si_ref.shape[0]

    def row_dma(r):
        return pltpu.make_async_copy(src_ref.at[pl.ds(si_ref[r], 1)], dst_ref.at[pl.ds(di_ref[r], 1)], sem)

    def start(r, c):
        row_dma(r).start()
        return c

    def wait(r, c):
        row_dma(r).wait()
        return c

    lax.fori_loop(0, n, start, 0, unroll=8)
    lax.fori_loop(0, n, wait, 0, unroll=8)


def _row_copy(src, src_idx, dst_idx, n_dst):
    n = src_idx.shape[0]
    rb = COPY_ROWS
    smem_blk = pl.BlockSpec((rb,), lambda i: (i,), memory_space=pltpu.SMEM)
    return pl.pallas_call(
        _row_copy_kernel,
        grid=(n // rb,),
        in_specs=[smem_blk, smem_blk, pl.BlockSpec(memory_space=pl.ANY)],
        out_specs=pl.BlockSpec(memory_space=pl.ANY),
        out_shape=jax.ShapeDtypeStruct((n_dst, src.shape[1]), src.dtype),
        scratch_shapes=[pltpu.SemaphoreType.DMA(())],
        compiler_params=pltpu.CompilerParams(dimension_semantics=("arbitrary",), has_side_effects=True),
        name="row_copy",
    )(src_idx, dst_idx, src)


def _expert_kernel(be_ref, x_ref, wg_ref, wl_ref, bg_ref, bl_ref, wd_ref, bd_ref, o_ref):
    del be_ref
    x = x_ref[...].astype(BF16)
    hg = jnp.dot(x, wg_ref[0], preferred_element_type=F32) + bg_ref[0]
    hl = jnp.dot(x, wl_ref[0], preferred_element_type=F32) + bl_ref[0]
    hg = jnp.minimum(hg, SWIGLU_LIMIT)
    hl = jnp.clip(hl, -SWIGLU_LIMIT, SWIGLU_LIMIT)
    act = hg * _sigmoid(SWIGLU_ALPHA * hg) * (hl + 1.0)
    o_ref[...] = jnp.dot(act.astype(BF16), wd_ref[0], preferred_element_type=F32) + bd_ref[0]


def _experts(block_e, xs, w_glu, w_lin, b_glu, b_lin, w_down, b_down):
    p_rows = xs.shape[0]
    bm = MOE_BLOCK
    by_e3 = lambda i, be: (be[i], 0, 0)
    grid_spec = pltpu.PrefetchScalarGridSpec(
        num_scalar_prefetch=1,
        grid=(p_rows // bm,),
        in_specs=[pl.BlockSpec((bm, D_MODEL), lambda i, be: (i, 0)),
                  pl.BlockSpec((1, D_MODEL, D_EXPERT), by_e3), pl.BlockSpec((1, D_MODEL, D_EXPERT), by_e3),
                  pl.BlockSpec((1, 1, D_EXPERT), by_e3), pl.BlockSpec((1, 1, D_EXPERT), by_e3),
                  pl.BlockSpec((1, D_EXPERT, D_MODEL), by_e3), pl.BlockSpec((1, 1, D_MODEL), by_e3)],
        out_specs=pl.BlockSpec((bm, D_MODEL), lambda i, be: (i, 0)),
    )
    return pl.pallas_call(
        _expert_kernel,
        grid_spec=grid_spec,
        out_shape=jax.ShapeDtypeStruct((p_rows, D_MODEL), F32),
        compiler_params=_cparams("arbitrary"),
        name="experts",
    )(block_e, xs, w_glu, w_lin, b_glu, b_lin, w_down, b_down)


def _combine_kernel(z_ref, gate_ref, h_ref, lnw_ref, lnb_ref, o_ref):
    gates = gate_ref[...]
    y = gates[:, 0:1] * z_ref[:, 0:D_MODEL]
    for kk in range(1, TOP_K):
        y = y + gates[:, kk:kk + 1] * z_ref[:, kk * D_MODEL:(kk + 1) * D_MODEL]
    o_ref[...] = _layer_norm(DN_ALPHA * h_ref[...] + y, lnw_ref[...], lnb_ref[...])


def _combine(z4, gates, h1, ln_w, ln_b):
    m = h1.shape[0]
    tm = ROW_TILE // 2
    row = lambda i: (i, 0)
    fixed = lambda i: (0, 0)
    return pl.pallas_call(
        _combine_kernel,
        grid=(m // tm,),
        in_specs=[pl.BlockSpec((tm, TOP_K * D_MODEL), row), pl.BlockSpec((tm, LANES), row),
                  pl.BlockSpec((tm, D_MODEL), row), pl.BlockSpec((1, D_MODEL), fixed), pl.BlockSpec((1, D_MODEL), fixed)],
        out_specs=pl.BlockSpec((tm, D_MODEL), row),
        out_shape=jax.ShapeDtypeStruct((m, D_MODEL), F32),
        compiler_params=_cparams("parallel"),
        name="combine_ln",
    )(z4, gates, h1, ln_w, ln_b)


def _routing_tables(top_idx):
    m = top_idx.shape[0]
    n_assign = m * TOP_K
    bm = MOE_BLOCK
    n_blocks = -(-(n_assign + N_EXPERTS * (bm - 1)) // bm)
    p_rows = n_blocks * bm
    e_flat = top_idx.reshape(n_assign)
    order = jnp.argsort(e_flat, stable=True).astype(jnp.int32)
    e_s = e_flat[order]
    counts = jnp.zeros((N_EXPERTS,), jnp.int32).at[e_flat].add(1)
    padded = (counts + bm - 1) // bm * bm
    start = jnp.cumsum(counts) - counts
    pend = jnp.cumsum(padded)
    pstart = pend - padded
    dest = pstart[e_s] + (jnp.arange(n_assign, dtype=jnp.int32) - start[e_s])
    src_assign = jnp.full((p_rows,), -1, jnp.int32).at[dest].set(order)
    is_pad = src_assign < 0
    tok = jnp.where(is_pad, 0, src_assign // TOP_K)
    pad_rank = jnp.cumsum(is_pad.astype(jnp.int32)) - 1
    slot = jnp.where(is_pad, n_assign + pad_rank, src_assign)
    block_e = jnp.minimum(jnp.searchsorted(pend, jnp.arange(n_blocks, dtype=jnp.int32) * bm, side='right'),
                          N_EXPERTS - 1).astype(jnp.int32)
    return tok, slot, block_e, p_rows


def _block_ones(n):
    g = np.arange(n) // HEAD_DIM
    return jnp.asarray(g[:, None] == g[None, :], BF16)


def _sw_head_order():
    group = SW_HEADS // SW_KV_HEADS
    heads = [kv * group + g for g in range(group) for kv in range(SW_KV_HEADS)]
    return np.concatenate([np.arange(h * HEAD_DIM, (h + 1) * HEAD_DIM) for h in heads])


def _rope_tables(seq):
    half = HEAD_DIM // 2
    inv = ROPE_THETA ** (-jnp.arange(half, dtype=F32) / half)
    ang = jnp.arange(seq, dtype=F32)[:, None] * inv[None, :]
    cos, sin = jnp.cos(ang), jnp.sin(ang)
    reps = LANES // HEAD_DIM
    cos_t = jnp.tile(jnp.concatenate([cos, cos], axis=-1), (1, reps))
    sin_t = jnp.tile(jnp.concatenate([-sin, sin], axis=-1), (1, reps))
    return cos_t, sin_t


def kernel(x, w_in, b_in, rw_shift_mu, rw_w_up, rw_w0, rw_a_up, rw_a0, rw_g_up, rw_k_k, rw_k_a, rw_r_k, rw_ln_w, rw_ln_b, sw_sinks, ml_conv_w, ml_conv_b, ml_norm_w, w_out, ln1_w, ln1_b, router_w, router_b, exp_w_up, exp_b_up, exp_w_down, exp_b_down, ln2_w, ln2_b):
    batch, seq, d_model = x.shape
    assert d_model == D_MODEL and seq % ROW_TILE == 0 and seq % ML_CHUNK == 0 and seq % WINDOW == 0
    m = batch * seq
    depth = w_in.shape[0]
    row2 = lambda t: t.reshape(1, -1)

    sw_perm = _sw_head_order()
    in_cols = np.concatenate([np.arange(RW_COLS), RW_COLS + sw_perm, np.arange(RW_COLS + SW_DIM, C_IN)])
    ones_rw = _block_ones(RW_DIM)
    ones_ml = _block_ones(ML_DIM)
    cos_t, sin_t = _rope_tables(seq)
    src = np.arange(LANES)[:, None]
    dst = np.arange(2 * ML_DIM)[None, :]
    expand = jnp.asarray(src == (dst // ML_DIM) * ML_HEADS + (dst % ML_DIM) // HEAD_DIM, BF16)

    h = x.reshape(m, D_MODEL)
    for l in range(depth):
        w_l = jnp.pad(w_in[l][:, in_cols], ((0, 0), (0, C_IN_PAD - C_IN))).astype(BF16)
        b_l = jnp.pad(b_in[l][in_cols], (0, C_IN_PAD - C_IN)).reshape(1, C_IN_PAD)
        zero = jnp.zeros((RW_LORA_W, RW_DIM), F32)
        wcomb = jnp.concatenate([jnp.concatenate([rw_w_up[l], zero], axis=1),
                                 jnp.concatenate([zero, rw_a_up[l]], axis=1)], axis=0).astype(BF16)
        w_o = w_out[l]
        w_o_rw = w_o[0:RW_DIM].astype(BF16)
        w_o_sw = w_o[RW_DIM + sw_perm].astype(BF16)
        w_o_ml = w_o[RW_DIM + SW_DIM:].astype(BF16)
        rw_pad = jnp.pad(router_w[l], ((0, 0), (0, LANES - N_EXPERTS)))
        rw_hi = rw_pad.astype(BF16)
        rw_r1 = rw_pad - rw_hi.astype(F32)
        rw_mid = rw_r1.astype(BF16)
        rw_lo = (rw_r1 - rw_mid.astype(F32)).astype(BF16)
        rb_pad = jnp.pad(router_b[l], (0, LANES - N_EXPERTS), constant_values=NEG_BIG).reshape(1, LANES)
        w_glu = exp_w_up[l][:, :, 0::2].astype(BF16)
        w_lin = exp_w_up[l][:, :, 1::2].astype(BF16)
        b_glu = exp_b_up[l][:, 0::2].reshape(N_EXPERTS, 1, D_EXPERT)
        b_lin = exp_b_up[l][:, 1::2].reshape(N_EXPERTS, 1, D_EXPERT)
        w_dn = exp_w_down[l].astype(BF16)
        b_dn = exp_b_down[l].reshape(N_EXPERTS, 1, D_MODEL)

        p_rw, p_sw, p_ml = _in_proj(h, w_l, b_l)
        r, lw, k, v, a, b, g = _rw_prep(p_rw, seq, row2(rw_shift_mu[l]), wcomb, rw_g_up[l].astype(BF16),
                                        row2(rw_w0[l]), row2(rw_a0[l]), row2(rw_k_k[l]), row2(rw_k_a[l]), ones_rw)
        m_mat, yh, g_mat, y0 = _rw_chunks(r, lw, k, v, a, b)
        y = _rw_scan(batch, seq, m_mat, yh, g_mat, y0).reshape(m, RW_DIM)
        rw_out = _rw_post(y, r, k, v, g, row2(rw_r_k[l]), row2(rw_ln_w[l]), row2(rw_ln_b[l]), ones_rw)
        sw_out = _swa(p_sw.reshape(batch, seq, SW_COLS), sw_sinks[l], cos_t, sin_t).reshape(m, SW_DIM)
        ml_out = _mlstm(p_ml.reshape(batch, seq, ML_COLS_PAD), ml_conv_w[l], row2(ml_conv_b[l]),
                        row2(ml_norm_w[l]), expand, ones_ml).reshape(m, ML_DIM)
        h1, idx_pad, gates = _out_proj(rw_out, sw_out, ml_out, h, w_o_rw, w_o_sw, w_o_ml, row2(ln1_w[l]),
                                       row2(ln1_b[l]), (rw_hi, rw_mid, rw_lo), rb_pad)

        tok, slot, block_e, p_rows = _routing_tables(idx_pad[:, 0:TOP_K])
        ar = jnp.arange(p_rows, dtype=jnp.int32)
        xs = _row_copy(h1, tok, ar, p_rows)
        yb = _experts(block_e, xs, w_glu, w_lin, b_glu, b_lin, w_dn, b_dn)
        z = _row_copy(yb, ar, slot, p_rows)
        h = _combine(z.reshape(p_rows // TOP_K, TOP_K * D_MODEL), gates, h1, row2(ln2_w[l]), row2(ln2_b[l]))
    return h.reshape(batch, seq, D_MODEL)
```

```python
import functools

import numpy as np
import jax
import jax.numpy as jnp
from jax import lax
from jax.experimental import pallas as pl
from jax.experimental.pallas import tpu as pltpu

F32 = jnp.float32
BF16 = jnp.bfloat16

D_MODEL = 1024
HEAD_DIM = 64
RW_HEADS = 6
SW_HEADS = 6
SW_KV_HEADS = 2
ML_HEADS = 4
RW_DIM = RW_HEADS * HEAD_DIM
SW_DIM = SW_HEADS * HEAD_DIM
SW_KV_DIM = SW_KV_HEADS * HEAD_DIM
ML_DIM = ML_HEADS * HEAD_DIM
RW_LORA_W = 64
RW_LORA_A = 64
RW_LORA_G = 128
RW_COLS = 3 * RW_DIM + RW_LORA_W + RW_LORA_A + RW_LORA_G
SW_COLS = SW_DIM + 2 * SW_KV_DIM
ML_COLS = 4 * ML_DIM + 2 * ML_HEADS
C_IN = RW_COLS + SW_COLS + ML_COLS
WINDOW = 128
ROPE_THETA = 10000.0
CONV_WIDTH = 4
N_EXPERTS = 32
TOP_K = 4
D_EXPERT = 1024
SWIGLU_ALPHA = 1.702
SWIGLU_LIMIT = 7.0
LN_EPS = 1e-5
RW_GN_EPS = 64e-5
ML_GN_EPS = 1e-6
DEPTH = 4
DN_ALPHA = (2 * DEPTH) ** 0.25

LANES = 128
SUBLANES = 8
ML_COLS_PAD = 1152
C_IN_PAD = RW_COLS + SW_COLS + ML_COLS_PAD
ROW_TILE = 512
RW_CHUNK = 64
RW_PAIR = 2 * HEAD_DIM
RW_NPAIR = RW_HEADS // 2
ML_CHUNK = 256
MOE_BLOCK = 512
NEG_BIG = -1e30
VMEM_LIMIT = 56 * 1024 * 1024


def _cparams(*sem):
    return pltpu.CompilerParams(dimension_semantics=sem, vmem_limit_bytes=VMEM_LIMIT)


def _dot(a, b):
    return jnp.dot(a.astype(BF16), b.astype(BF16), preferred_element_type=F32)


def _dot_nt(a, b):
    return lax.dot_general(a.astype(BF16), b.astype(BF16), (((1,), (1,)), ((), ())),
                           preferred_element_type=F32)


def _split3(x):
    hi = x.astype(BF16)
    r1 = x - hi.astype(F32)
    mid = r1.astype(BF16)
    lo = (r1 - mid.astype(F32)).astype(BF16)
    return hi, mid, lo


def _dot_xl(x, w):
    hi, mid, lo = _split3(x)
    return (jnp.dot(hi, w, preferred_element_type=F32) + jnp.dot(mid, w, preferred_element_type=F32)
            + jnp.dot(lo, w, preferred_element_type=F32))


def _dot_lx(w, x):
    hi, mid, lo = _split3(x)
    return (jnp.dot(w, hi, preferred_element_type=F32) + jnp.dot(w, mid, preferred_element_type=F32)
            + jnp.dot(w, lo, preferred_element_type=F32))


def _softplus(z):
    return jnp.maximum(z, 0.0) + jnp.log(1.0 + jnp.exp(-jnp.abs(z)))


def _sigmoid(z):
    return 1.0 / (1.0 + jnp.exp(-z))


def _shift_rows(x, prev8, j):
    cat = jnp.concatenate([prev8, x], axis=0)
    return pltpu.roll(cat, j, 0)[SUBLANES:, :]


def _layer_norm(x, w, b):
    mu = jnp.mean(x, axis=-1, keepdims=True)
    d = x - mu
    var = jnp.mean(d * d, axis=-1, keepdims=True)
    return d * lax.rsqrt(var + LN_EPS) * w + b


def _head_norm(y, ones_bd, eps):
    inv = 1.0 / HEAD_DIM
    mu = _dot_xl(y, ones_bd) * inv
    d = y - mu
    var = _dot_xl(d * d, ones_bd) * inv
    return d * lax.rsqrt(var + eps)


def _inproj_kernel(x_ref, w_ref, b_ref, rw_ref, sw_ref, ml_ref):
    x = x_ref[...].astype(BF16)
    c1, c2 = RW_COLS, RW_COLS + SW_COLS
    rw_ref[...] = jnp.dot(x, w_ref[:, 0:c1], preferred_element_type=F32) + b_ref[:, 0:c1]
    sw_ref[...] = jnp.dot(x, w_ref[:, c1:c2], preferred_element_type=F32) + b_ref[:, c1:c2]
    ml_ref[...] = jnp.dot(x, w_ref[:, c2:C_IN_PAD], preferred_element_type=F32) + b_ref[:, c2:C_IN_PAD]


def _in_proj(h2d, w, b):
    m = h2d.shape[0]
    tm = ROW_TILE
    row = lambda i: (i, 0)
    fixed = lambda i: (0, 0)
    return pl.pallas_call(
        _inproj_kernel,
        grid=(m // tm,),
        in_specs=[pl.BlockSpec((tm, D_MODEL), row), pl.BlockSpec((D_MODEL, C_IN_PAD), fixed),
                  pl.BlockSpec((1, C_IN_PAD), fixed)],
        out_specs=[pl.BlockSpec((tm, RW_COLS), row), pl.BlockSpec((tm, SW_COLS), row),
                   pl.BlockSpec((tm, ML_COLS_PAD), row)],
        out_shape=[jax.ShapeDtypeStruct((m, RW_COLS), F32), jax.ShapeDtypeStruct((m, SW_COLS), F32),
                   jax.ShapeDtypeStruct((m, ML_COLS_PAD), F32)],
        compiler_params=_cparams("parallel"),
        name="in_proj",
    )(h2d, w, b)


def _rw_prep_kernel(tiles_per_seq, p_ref, pprev_ref, mu_ref, wcomb_ref, gup_ref, w0_ref, a0_ref, kk_ref,
                    ka_ref, ones_ref, r_o, lw_o, k_o, v_o, a_o, b_o, g_o):
    i = pl.program_id(0)
    x = p_ref[...]
    first = (i % tiles_per_seq) == 0
    prev8 = jnp.where(first, 0.0, pprev_ref[...])
    xs = x + (_shift_rows(x, prev8, 1) - x) * mu_ref[...]
    d = RW_DIM
    r = xs[:, 0:d]
    k = xs[:, d:2 * d]
    v = xs[:, 2 * d:3 * d]
    slab = xs[:, 3 * d:3 * d + LANES]
    lane = lax.broadcasted_iota(jnp.int32, slab.shape, 1)
    slab = jnp.where(lane < RW_LORA_W, jnp.tanh(slab), slab)
    wa = _dot(slab, wcomb_ref[...])
    w_pre = w0_ref[...] + wa[:, 0:d]
    a_pre = a0_ref[...] + wa[:, d:2 * d]
    w_log = -_softplus(-w_pre) - 0.5
    log_decay = -jnp.exp(w_log)
    a_sig = _sigmoid(a_pre)
    g = _dot(_sigmoid(xs[:, 3 * d + LANES:3 * d + 2 * LANES]), gup_ref[...])
    kk = k * kk_ref[...]
    ss = _dot_xl(kk * kk, ones_ref[...])
    kk = kk / jnp.maximum(jnp.sqrt(ss), 1e-12)
    r_o[...] = r
    lw_o[...] = log_decay
    k_o[...] = k * (1.0 + (a_sig - 1.0) * ka_ref[...])
    v_o[...] = v
    a_o[...] = -kk
    b_o[...] = kk * a_sig
    g_o[...] = g


def _rw_prep(p_rw, seq, mu, wcomb, gup, w0, a0, k_k, k_a, ones_bd):
    m = p_rw.shape[0]
    tm = ROW_TILE
    row = lambda i: (i, 0)
    fixed = lambda i: (0, 0)
    prev = lambda i: (jnp.maximum(i * (tm // SUBLANES) - 1, 0), 0)
    vec = pl.BlockSpec((1, RW_DIM), fixed)
    out = jax.ShapeDtypeStruct((m, RW_DIM), F32)
    return pl.pallas_call(
        functools.partial(_rw_prep_kernel, seq // tm),
        grid=(m // tm,),
        in_specs=[pl.BlockSpec((tm, RW_COLS), row), pl.BlockSpec((SUBLANES, RW_COLS), prev),
                  pl.BlockSpec((1, RW_COLS), fixed), pl.BlockSpec((LANES, 2 * RW_DIM), fixed),
                  pl.BlockSpec((RW_LORA_G, RW_DIM), fixed), vec, vec, vec, vec,
                  pl.BlockSpec((RW_DIM, RW_DIM), fixed)],
        out_specs=[pl.BlockSpec((tm, RW_DIM), row)] * 7,
        out_shape=[out] * 7,
        compiler_params=_cparams("parallel"),
        name="rw_prep",
    )(p_rw, p_rw, mu, wcomb, gup, w0, a0, k_k, k_a, ones_bd)


def _rw_chunk_kernel(n_chunk, r_ref, lw_ref, k_ref, v_ref, a_ref, b_ref, m_o, yh_o, g_o, y0_o):
    L = RW_CHUNK
    ti = lax.broadcasted_iota(jnp.int32, (L, L), 0)
    si = lax.broadcasted_iota(jnp.int32, (L, L), 1)
    tri = jnp.where(ti >= si, 1.0, 0.0).astype(BF16)
    n2 = 2 * L
    ri = lax.broadcasted_iota(jnp.int32, (n2, n2), 0)
    ci = lax.broadcasted_iota(jnp.int32, (n2, n2), 1)
    rr = jnp.where(ri >= L, ri - L, ri)
    cc = jnp.where(ci >= L, ci - L, ci)
    strict = rr > cc
    incl = rr >= cc
    eye = ri == ci
    eye_f = jnp.where(eye, 1.0, 0.0)
    lane = lax.broadcasted_iota(jnp.int32, (L, RW_PAIR), 1)
    head0 = lane < HEAD_DIM

    def stack(x):
        return jnp.concatenate([jnp.where(head0, x, 0.0), jnp.where(head0, 0.0, x)], axis=0)

    def body(c, carry):
        rows = pl.ds(pl.multiple_of(c * L, L), L)
        lc_all = _dot_lx(tri, lw_ref[rows, :])
        for p in range(RW_NPAIR):
            cols = slice(p * RW_PAIR, (p + 1) * RW_PAIR)
            lw = lw_ref[rows, cols]
            lc = lc_all[:, cols]
            lcl = lc[L - 1:L, :]
            gam = jnp.exp(lc)
            gam_prev = jnp.exp(lc - lw)
            inv = jnp.exp(-lc)
            to_end = jnp.exp(lcl - lc)
            r = r_ref[rows, cols]
            k = k_ref[rows, cols]
            v = v_ref[rows, cols]
            a = a_ref[rows, cols]
            b = b_ref[rows, cols]
            x1 = stack(a * gam_prev)
            r1 = stack(r * gam)
            x2 = stack(b * inv)
            k2 = stack(k * inv)
            vs = stack(v)
            bs = stack(b * to_end)
            ks = stack(k * to_end)
            aa = _dot_nt(jnp.concatenate([x1, r1], axis=0), jnp.concatenate([x2, k2], axis=0))
            a_ab = jnp.where(strict, aa[0:n2, 0:n2], 0.0)
            a_ak = jnp.where(strict, aa[0:n2, n2:2 * n2], 0.0)
            a_rb = jnp.where(incl, aa[n2:2 * n2, 0:n2], 0.0)
            a_rk = jnp.where(incl, aa[n2:2 * n2, n2:2 * n2], 0.0)
            apow = a_ab
            t_inv = eye_f + a_ab
            for _ in range(5):
                apow = _dot(apow, apow)
                t_inv = t_inv + _dot(t_inv, apow)
            akv = _dot(a_ak, vs)
            pw = _dot(t_inv, jnp.concatenate([x1, akv], axis=1))
            pmat = pw[:, 0:RW_PAIR]
            wmat = pw[:, RW_PAIR:2 * RW_PAIR]
            gl_row = jnp.exp(lcl)
            m_mat = _dot(bs.T, pmat) + jnp.where(eye, gl_row, 0.0)
            g_mat = _dot(jnp.concatenate([bs, ks], axis=0).T, jnp.concatenate([wmat, vs], axis=0))
            yy = _dot(a_rb, pw)
            yh = r1 + yy[:, 0:RW_PAIR]
            y0 = yy[:, RW_PAIR:2 * RW_PAIR] + _dot(a_rk, vs)
            m_o[c, p] = m_mat.astype(BF16)
            yh_o[c, p] = yh.astype(BF16)
            g_o[c, p] = g_mat
            y0_o[c, p] = y0
        return carry

    lax.fori_loop(0, n_chunk, body, 0, unroll=2)


def _rw_chunks(r, lw, k, v, a, b):
    m = r.shape[0]
    tm = ROW_TILE
    nc = tm // RW_CHUNK
    row = lambda i: (i, 0)
    blk = lambda i: (i, 0, 0, 0)
    n2 = 2 * RW_CHUNK
    n_tot = m // RW_CHUNK
    ospec = pl.BlockSpec((nc, RW_NPAIR, n2, RW_PAIR), blk)
    return pl.pallas_call(
        functools.partial(_rw_chunk_kernel, nc),
        grid=(m // tm,),
        in_specs=[pl.BlockSpec((tm, RW_DIM), row)] * 6,
        out_specs=[ospec] * 4,
        out_shape=[jax.ShapeDtypeStruct((n_tot, RW_NPAIR, n2, RW_PAIR), BF16),
                   jax.ShapeDtypeStruct((n_tot, RW_NPAIR, n2, RW_PAIR), BF16),
                   jax.ShapeDtypeStruct((n_tot, RW_NPAIR, n2, RW_PAIR), F32),
                   jax.ShapeDtypeStruct((n_tot, RW_NPAIR, n2, RW_PAIR), F32)],
        compiler_params=_cparams("parallel"),
        name="rw_chunks",
    )(r, lw, k, v, a, b)


def _rw_scan_kernel(batch, m_ref, yh_ref, g_ref, y0_ref, y_o, h_ref):
    c = pl.program_id(0)

    @pl.when(c == 0)
    def _():
        h_ref[...] = jnp.zeros_like(h_ref)

    L = RW_CHUNK
    for bi in range(batch):
        for p in range(RW_NPAIR):
            hb = h_ref[bi, p].astype(BF16)
            yrows = jnp.dot(yh_ref[bi, 0, p], hb, preferred_element_type=F32) + y0_ref[bi, 0, p]
            y_o[bi, :, p * RW_PAIR:(p + 1) * RW_PAIR] = yrows[0:L, :] + yrows[L:2 * L, :]
            h_ref[bi, p] = jnp.dot(m_ref[bi, 0, p], hb, preferred_element_type=F32) + g_ref[bi, 0, p]


def _rw_scan(batch, seq, m_mat, yh, g_mat, y0):
    nc = seq // RW_CHUNK
    n2 = 2 * RW_CHUNK
    shp = (batch, nc, RW_NPAIR, n2, RW_PAIR)
    args = [t.reshape(shp) for t in (m_mat, yh, g_mat, y0)]
    ispec = pl.BlockSpec((batch, 1, RW_NPAIR, n2, RW_PAIR), lambda c: (0, c, 0, 0, 0))
    return pl.pallas_call(
        functools.partial(_rw_scan_kernel, batch),
        grid=(nc,),
        in_specs=[ispec] * 4,
        out_specs=pl.BlockSpec((batch, RW_CHUNK, RW_DIM), lambda c: (0, c, 0)),
        out_shape=jax.ShapeDtypeStruct((batch, seq, RW_DIM), F32),
        scratch_shapes=[pltpu.VMEM((batch, RW_NPAIR, RW_PAIR, RW_PAIR), F32)],
        compiler_params=_cparams("arbitrary"),
        name="rw_scan",
    )(*args)


def _rw_post_kernel(y_ref, r_ref, k_ref, v_ref, g_ref, rk_ref, lnw_ref, lnb_ref, ones_ref, o_ref):
    ones_bd = ones_ref[...]
    y = _head_norm(y_ref[...], ones_bd, RW_GN_EPS) * lnw_ref[...] + lnb_ref[...]
    bonus = _dot_xl(r_ref[...] * k_ref[...] * rk_ref[...], ones_bd) * v_ref[...]
    o_ref[...] = (y + bonus) * g_ref[...]


def _rw_post(y, r, k, v, g, r_k, ln_w, ln_b, ones_bd):
    m = y.shape[0]
    tm = ROW_TILE
    row = lambda i: (i, 0)
    fixed = lambda i: (0, 0)
    vec = pl.BlockSpec((1, RW_DIM), fixed)
    return pl.pallas_call(
        _rw_post_kernel,
        grid=(m // tm,),
        in_specs=[pl.BlockSpec((tm, RW_DIM), row)] * 5 + [vec, vec, vec, pl.BlockSpec((RW_DIM, RW_DIM), fixed)],
        out_specs=pl.BlockSpec((tm, RW_DIM), row),
        out_shape=jax.ShapeDtypeStruct((m, RW_DIM), F32),
        compiler_params=_cparams("parallel"),
        name="rw_post",
    )(y, r, k, v, g, r_k, ln_w, ln_b, ones_bd)


def _rope(x, cos, sin_signed):
    lane = lax.broadcasted_iota(jnp.int32, x.shape, 1)
    half = HEAD_DIM // 2
    first_half = (lane % HEAD_DIM) < half
    rot = jnp.where(first_half, pltpu.roll(x, LANES - half, 1), pltpu.roll(x, half, 1))
    return x * cos + rot * sin_signed


def _swa_kernel(sink_ref, q_ref, kc_ref, vc_ref, kp_ref, vp_ref, cosc_ref, sinc_ref, cosp_ref, sinp_ref, o_ref):
    j = pl.program_id(1)
    W = WINDOW
    cos_c, sin_c = cosc_ref[...], sinc_ref[...]
    k_cat = jnp.concatenate([_rope(kp_ref[0], cosp_ref[...], sinp_ref[...]), _rope(kc_ref[0], cos_c, sin_c)], axis=0)
    v_cat = jnp.concatenate([vp_ref[0], vc_ref[0]], axis=0).astype(BF16)
    k_cat = k_cat.astype(BF16)
    qi = lax.broadcasted_iota(jnp.int32, (W, 2 * W), 0)
    kj = lax.broadcasted_iota(jnp.int32, (W, 2 * W), 1)
    lo = jnp.where(j > 0, qi, jnp.maximum(qi, W - 1))
    bias = jnp.where((kj > lo) & (kj <= qi + W), 0.0, NEG_BIG)
    lane = lax.broadcasted_iota(jnp.int32, (W, LANES), 1)
    kv0 = lane < HEAD_DIM
    scale = HEAD_DIM ** -0.5
    group = SW_HEADS // SW_KV_HEADS
    for g in range(group):
        q = _rope(q_ref[0, :, g * LANES:(g + 1) * LANES], cos_c, sin_c) * scale
        outs = []
        for kv in range(SW_KV_HEADS):
            qm = jnp.where(kv0, q, 0.0) if kv == 0 else jnp.where(kv0, 0.0, q)
            s = _dot_nt(qm, k_cat) + bias
            sink = sink_ref[kv * group + g]
            mx = jnp.maximum(jnp.max(s, axis=-1, keepdims=True), sink)
            pr = jnp.exp(s - mx)
            den = jnp.sum(pr, axis=-1, keepdims=True) + jnp.exp(sink - mx)
            outs.append(jnp.dot(pr.astype(BF16), v_cat, preferred_element_type=F32) / den)
        o_ref[0, :, g * LANES:(g + 1) * LANES] = jnp.where(kv0, outs[0], outs[1])


def _swa(p_sw3, sinks, cos_t, sin_t):
    batch, seq, _ = p_sw3.shape
    W = WINDOW
    nb = seq // W
    kcol = SW_DIM // LANES
    cur = lambda c: (lambda b, j: (b, j, c))
    prv = lambda c: (lambda b, j: (b, jnp.maximum(j - 1, 0), c))
    tab_c = pl.BlockSpec((W, LANES), lambda b, j: (j, 0))
    tab_p = pl.BlockSpec((W, LANES), lambda b, j: (jnp.maximum(j - 1, 0), 0))
    kv_blk = lambda f: pl.BlockSpec((1, W, LANES), f)
    return pl.pallas_call(
        _swa_kernel,
        grid=(batch, nb),
        in_specs=[pl.BlockSpec(memory_space=pltpu.SMEM),
                  pl.BlockSpec((1, W, SW_DIM), cur(0)),
                  kv_blk(cur(kcol)), kv_blk(cur(kcol + 1)), kv_blk(prv(kcol)), kv_blk(prv(kcol + 1)),
                  tab_c, tab_c, tab_p, tab_p],
        out_specs=pl.BlockSpec((1, W, SW_DIM), cur(0)),
        out_shape=jax.ShapeDtypeStruct((batch, seq, SW_DIM), F32),
        compiler_params=_cparams("parallel", "parallel"),
        name="swa",
    )(sinks, p_sw3, p_sw3, p_sw3, p_sw3, p_sw3, cos_t, sin_t, cos_t, sin_t)


def _mlstm_kernel(p_ref, pprev_ref, cw_ref, cb_ref, nw_ref, expand_ref, ones_ref, o_ref, c_ref, nv_ref, m_ref):
    j = pl.program_id(1)
    L = ML_CHUNK
    d = ML_DIM

    @pl.when(j == 0)
    def _():
        c_ref[...] = jnp.zeros_like(c_ref)
        nv_ref[...] = jnp.zeros_like(nv_ref)
        m_ref[...] = jnp.zeros_like(m_ref)

    x = p_ref[0]
    qk_pre = x[:, 0:2 * d]
    prev8 = jnp.where(j == 0, 0.0, pprev_ref[0][:, 0:2 * d])
    conv = cb_ref[...] + cw_ref[CONV_WIDTH - 1:CONV_WIDTH, :] * qk_pre
    for s in range(1, CONV_WIDTH):
        conv = conv + cw_ref[CONV_WIDTH - 1 - s:CONV_WIDTH - s, :] * _shift_rows(qk_pre, prev8, s)
    qk = conv * _sigmoid(conv)
    q = qk[:, 0:d]
    k = qk[:, d:2 * d] * (HEAD_DIM ** -0.5)
    v = x[:, 2 * d:3 * d]
    o_gate = x[:, 3 * d:4 * d]
    gates = _dot_xl(x[:, 4 * d:4 * d + LANES], expand_ref[...])
    i_full = gates[:, 0:d]
    f_full = gates[:, d:2 * d]
    lf_full = jnp.minimum(f_full, 0.0) - jnp.log(1.0 + jnp.exp(-jnp.abs(f_full)))
    ti = lax.broadcasted_iota(jnp.int32, (L, L), 0)
    si = lax.broadcasted_iota(jnp.int32, (L, L), 1)
    causal = ti >= si
    tri = jnp.where(causal, 1.0, 0.0).astype(BF16)
    bc_full = _dot_lx(tri, lf_full)
    u_t = (i_full - bc_full).T
    lane = lax.broadcasted_iota(jnp.int32, (L, d), 1)
    lane_row = lax.broadcasted_iota(jnp.int32, (1, d), 1)
    q_b = q.astype(BF16)
    k_b = k.astype(BF16)
    m_prev_row = m_ref[...]
    nv_row = nv_ref[...]
    qn = q * nv_row
    num = jnp.zeros((L, d), F32)
    inter_full = jnp.zeros((L, d), F32)
    den_full = jnp.zeros((L, d), F32)
    mnew_row = jnp.zeros((1, d), F32)
    for h in range(ML_HEADS):
        c0 = h * HEAD_DIM
        in_head = (lane >= c0) & (lane < c0 + HEAD_DIM)
        in_head_row = (lane_row >= c0) & (lane_row < c0 + HEAD_DIM)
        bc_col = bc_full[:, c0:c0 + 1]
        dmat = jnp.where(causal, bc_col + u_t[c0:c0 + 1, :], NEG_BIG)
        m_inter = bc_col + m_prev_row[:, c0:c0 + 1]
        m_t = jnp.maximum(m_inter, jnp.max(dmat, axis=-1, keepdims=True))
        inter = jnp.exp(m_inter - m_t)
        e = jnp.exp(dmat - m_t)
        sm = _dot_nt(jnp.where(in_head, q, 0.0), k_b) * e
        num = num + _dot(sm, jnp.where(in_head, v, 0.0))
        nq = jnp.sum(sm, axis=-1, keepdims=True) + inter * jnp.sum(jnp.where(in_head, qn, 0.0), axis=-1, keepdims=True)
        den = jnp.maximum(jnp.abs(nq), jnp.exp(-m_t))
        inter_full = jnp.where(in_head, inter, inter_full)
        den_full = jnp.where(in_head, den, den_full)
        mnew_row = jnp.where(in_head_row, m_t[L - 1:L, :], mnew_row)
    c_mat = c_ref[...]
    num = num + inter_full * jnp.dot(q_b, c_mat.astype(BF16), preferred_element_type=F32)
    hout = num / den_full
    hn = _head_norm(hout, ones_ref[...], ML_GN_EPS) * nw_ref[...]
    o_ref[0] = _sigmoid(o_gate) * hn
    bcl_row = bc_full[L - 1:L, :]
    wst = jnp.exp(bcl_row - bc_full + i_full - mnew_row)
    dec_row = jnp.exp(bcl_row + m_prev_row - mnew_row)
    kw = k * wst
    ri = lax.broadcasted_iota(jnp.int32, (d, d), 0) // HEAD_DIM
    ci = lax.broadcasted_iota(jnp.int32, (d, d), 1) // HEAD_DIM
    c_ref[...] = dec_row * c_mat + jnp.where(ri == ci, _dot(kw.T, v), 0.0)
    nv_ref[...] = dec_row * nv_row + jnp.sum(kw, axis=0, keepdims=True)
    m_ref[...] = mnew_row


def _mlstm(p_ml3, conv_w, conv_b, norm_w, expand, ones_bd):
    batch, seq, _ = p_ml3.shape
    L = ML_CHUNK
    fixed = lambda b, j: (0, 0)
    return pl.pallas_call(
        _mlstm_kernel,
        grid=(batch, seq // L),
        in_specs=[pl.BlockSpec((1, L, ML_COLS_PAD), lambda b, j: (b, j, 0)),
                  pl.BlockSpec((1, SUBLANES, ML_COLS_PAD), lambda b, j: (b, jnp.maximum(j * (L // SUBLANES) - 1, 0), 0)),
                  pl.BlockSpec((CONV_WIDTH, 2 * ML_DIM), fixed), pl.BlockSpec((1, 2 * ML_DIM), fixed),
                  pl.BlockSpec((1, ML_DIM), fixed), pl.BlockSpec((LANES, 2 * ML_DIM), fixed),
                  pl.BlockSpec((ML_DIM, ML_DIM), fixed)],
        out_specs=pl.BlockSpec((1, L, ML_DIM), lambda b, j: (b, j, 0)),
        out_shape=jax.ShapeDtypeStruct((batch, seq, ML_DIM), F32),
        scratch_shapes=[pltpu.VMEM((ML_DIM, ML_DIM), F32), pltpu.VMEM((1, ML_DIM), F32),
                        pltpu.VMEM((1, ML_DIM), F32)],
        compiler_params=_cparams("parallel", "arbitrary"),
        name="mlstm",
    )(p_ml3, p_ml3, conv_w, conv_b, norm_w, expand, ones_bd)


def _outproj_kernel(rw_ref, sw_ref, ml_ref, h_ref, wrw_ref, wsw_ref, wml_ref, lnw_ref, lnb_ref,
                    rwh_ref, rwm_ref, rwl_ref, rb_ref, h1_o, idx_o, gate_o):
    mix = (_dot(rw_ref[...], wrw_ref[...]) + _dot(sw_ref[...], wsw_ref[...]) + _dot(ml_ref[...], wml_ref[...]))
    h1 = _layer_norm(DN_ALPHA * h_ref[...] + mix, lnw_ref[...], lnb_ref[...])
    h1_o[...] = h1
    xh, xm, xl = _split3(h1)
    wh, wm, wl = rwh_ref[...], rwm_ref[...], rwl_ref[...]
    dd = lambda a, b: jnp.dot(a, b, preferred_element_type=F32)
    logits = (dd(xh, wh) + (dd(xh, wm) + dd(xm, wh)) + (dd(xh, wl) + dd(xm, wm) + dd(xl, wh))) + rb_ref[...]
    lane = lax.broadcasted_iota(jnp.int32, logits.shape, 1).astype(F32)
    vals, idxs = [], []
    cur = logits
    for _ in range(TOP_K):
        mx = jnp.max(cur, axis=-1, keepdims=True)
        ix = jnp.min(jnp.where(cur == mx, lane, float(LANES)), axis=-1, keepdims=True)
        vals.append(mx)
        idxs.append(ix)
        cur = jnp.where(lane == ix, NEG_BIG * 2.0, cur)
    es = [jnp.exp(vv - vals[0]) for vv in vals]
    den = es[0] + es[1] + es[2] + es[3]
    idx_out = jnp.zeros_like(logits)
    gate_out = jnp.zeros_like(logits)
    for kk in range(TOP_K):
        idx_out = jnp.where(lane == float(kk), idxs[kk], idx_out)
        gate_out = jnp.where(lane == float(kk), es[kk] / den, gate_out)
    idx_o[...] = idx_out.astype(jnp.int32)
    gate_o[...] = gate_out


def _out_proj(rw_out, sw_out, ml_out, h2d, w_rw, w_sw, w_ml, ln_w, ln_b, rw_parts, rb):
    m = h2d.shape[0]
    tm = ROW_TILE
    row = lambda i: (i, 0)
    fixed = lambda i: (0, 0)
    full = lambda a: pl.BlockSpec(a.shape, fixed)
    return pl.pallas_call(
        _outproj_kernel,
        grid=(m // tm,),
        in_specs=[pl.BlockSpec((tm, RW_DIM), row), pl.BlockSpec((tm, SW_DIM), row), pl.BlockSpec((tm, ML_DIM), row),
                  pl.BlockSpec((tm, D_MODEL), row), full(w_rw), full(w_sw), full(w_ml), full(ln_w), full(ln_b),
                  full(rw_parts[0]), full(rw_parts[1]), full(rw_parts[2]), full(rb)],
        out_specs=[pl.BlockSpec((tm, D_MODEL), row), pl.BlockSpec((tm, LANES), row), pl.BlockSpec((tm, LANES), row)],
        out_shape=[jax.ShapeDtypeStruct((m, D_MODEL), F32), jax.ShapeDtypeStruct((m, LANES), jnp.int32),
                   jax.ShapeDtypeStruct((m, LANES), F32)],
        compiler_params=_cparams("parallel"),
        name="out_proj_router",
    )(rw_out, sw_out, ml_out, h2d, w_rw, w_sw, w_ml, ln_w, ln_b, *rw_parts, rb)


def _expert_kernel(n_blocks, layer, be_ref, tokc_ref, tokn_ref, slot_ref, h_hbm, wup_ref, bg_ref, bl_ref,
                   wdn_ref, bd_ref, perm_ref, z_hbm, xbuf, ybuf, wg_s, wl_s, wd_s, gsem, ssem):
    del layer
    i = pl.program_id(0)
    s = i % 2
    bm = MOE_BLOCK

    def gather_row(tok, buf, r):
        return pltpu.make_async_copy(h_hbm.at[pl.ds(tok, 1)], xbuf.at[buf, pl.ds(r, 1)], gsem.at[buf])

    def scatter_row(dst, buf, r):
        return pltpu.make_async_copy(ybuf.at[buf, pl.ds(r, 1)], z_hbm.at[pl.ds(dst, 1)], ssem.at[buf])

    def wait_gather(buf):
        pltpu.make_async_copy(h_hbm.at[pl.ds(0, bm)], xbuf.at[buf], gsem.at[buf]).wait()

    def wait_scatter(buf):
        pltpu.make_async_copy(ybuf.at[buf], z_hbm.at[pl.ds(0, bm)], ssem.at[buf]).wait()

    @pl.when(i == 0)
    def _():
        ybuf[1] = jnp.zeros((bm, D_MODEL), F32)

        def prime(r, c):
            gather_row(tokc_ref[r], 0, r).start()
            return c

        lax.fori_loop(0, bm, prime, 0, unroll=8)

    wait_gather(s)

    @pl.when(i > 0)
    def _():
        wait_scatter(s)

    blk = jnp.minimum(i, n_blocks - 1)
    e_now = be_ref[blk]
    e_prev = be_ref[jnp.maximum(blk - 1, 0)]

    @pl.when((i == 0) | (e_now != e_prev))
    def _():
        perm = perm_ref[...]
        for c in range(D_EXPERT // LANES):
            t = jnp.dot(wup_ref[0, 0, :, c * 2 * LANES:(c + 1) * 2 * LANES].astype(BF16), perm,
                        preferred_element_type=F32)
            wg_s[:, c * LANES:(c + 1) * LANES] = t[:, 0:LANES].astype(BF16)
            wl_s[:, c * LANES:(c + 1) * LANES] = t[:, LANES:2 * LANES].astype(BF16)
        wd_s[...] = wdn_ref[0, 0].astype(BF16)

    for r in range(bm):
        gather_row(tokn_ref[r], 1 - s, r).start()
        scatter_row(slot_ref[r], 1 - s, r).start()
    x = xbuf[s].astype(BF16)
    hg = jnp.dot(x, wg_s[...], preferred_element_type=F32) + bg_ref[0, 0]
    hl = jnp.dot(x, wl_s[...], preferred_element_type=F32) + bl_ref[0, 0]
    hg = jnp.minimum(hg, SWIGLU_LIMIT)
    hl = jnp.clip(hl, -SWIGLU_LIMIT, SWIGLU_LIMIT)
    act = hg * _sigmoid(SWIGLU_ALPHA * hg) * (hl + 1.0)
    ybuf[s] = jnp.dot(act.astype(BF16), wd_s[...], preferred_element_type=F32) + bd_ref[0, 0]

    @pl.when(i == n_blocks)
    def _():
        wait_gather(1 - s)
        wait_scatter(1 - s)


def _experts(layer, block_e, tok, slot_ext, h1, w_up, b_glu, b_lin, w_down, b_down, perm, z_rows):
    bm = MOE_BLOCK
    n_blocks = tok.shape[0] // bm
    last = n_blocks - 1
    by_e = lambda i, be: (layer, be[jnp.minimum(i, last)], 0, 0)
    smem = lambda f: pl.BlockSpec((bm,), f, memory_space=pltpu.SMEM)
    grid_spec = pltpu.PrefetchScalarGridSpec(
        num_scalar_prefetch=1,
        grid=(n_blocks + 1,),
        in_specs=[smem(lambda i, be: (jnp.minimum(i, last),)), smem(lambda i, be: (jnp.minimum(i + 1, last),)),
                  smem(lambda i, be: (i,)),
                  pl.BlockSpec(memory_space=pl.ANY),
                  pl.BlockSpec((1, 1, D_MODEL, 2 * D_EXPERT), by_e),
                  pl.BlockSpec((1, 1, 1, D_EXPERT), by_e), pl.BlockSpec((1, 1, 1, D_EXPERT), by_e),
                  pl.BlockSpec((1, 1, D_EXPERT, D_MODEL), by_e), pl.BlockSpec((1, 1, 1, D_MODEL), by_e),
                  pl.BlockSpec((2 * LANES, 2 * LANES), lambda i, be: (0, 0))],
        out_specs=pl.BlockSpec(memory_space=pl.ANY),
        scratch_shapes=[pltpu.VMEM((2, bm, D_MODEL), F32), pltpu.VMEM((2, bm, D_MODEL), F32),
                        pltpu.VMEM((D_MODEL, D_EXPERT), BF16), pltpu.VMEM((D_MODEL, D_EXPERT), BF16),
                        pltpu.VMEM((D_EXPERT, D_MODEL), BF16),
                        pltpu.SemaphoreType.DMA((2,)), pltpu.SemaphoreType.DMA((2,))],
    )
    return pl.pallas_call(
        functools.partial(_expert_kernel, n_blocks, layer),
        grid_spec=grid_spec,
        out_shape=jax.ShapeDtypeStruct((z_rows, D_MODEL), F32),
        compiler_params=pltpu.CompilerParams(dimension_semantics=("arbitrary",), vmem_limit_bytes=VMEM_LIMIT,
                                             disable_bounds_checks=True),
        name="experts",
    )(block_e, tok, tok, slot_ext, h1, w_up, b_glu, b_lin, w_down, b_down, perm)


def _combine_kernel(z0_ref, z1_ref, z2_ref, z3_ref, gate_ref, h_ref, lnw_ref, lnb_ref, o_ref):
    gates = gate_ref[...]
    y = gates[:, 0:1] * z0_ref[...]
    for kk, z_ref in enumerate((z1_ref, z2_ref, z3_ref), start=1):
        y = y + gates[:, kk:kk + 1] * z_ref[...]
    o_ref[...] = _layer_norm(DN_ALPHA * h_ref[...] + y, lnw_ref[...], lnb_ref[...])


def _combine(z, gates, h1, ln_w, ln_b):
    m = h1.shape[0]
    tm = ROW_TILE
    n_tiles = m // tm
    row = lambda i: (i, 0)
    fixed = lambda i: (0, 0)
    choice = lambda kk: pl.BlockSpec((tm, D_MODEL), lambda i: (kk * n_tiles + i, 0))
    return pl.pallas_call(
        _combine_kernel,
        grid=(n_tiles,),
        in_specs=[choice(0), choice(1), choice(2), choice(3), pl.BlockSpec((tm, LANES), row),
                  pl.BlockSpec((tm, D_MODEL), row), pl.BlockSpec((1, D_MODEL), fixed), pl.BlockSpec((1, D_MODEL), fixed)],
        out_specs=pl.BlockSpec((tm, D_MODEL), row),
        out_shape=jax.ShapeDtypeStruct((m, D_MODEL), F32),
        compiler_params=_cparams("parallel"),
        name="combine_ln",
    )(z, z, z, z, gates, h1, ln_w, ln_b)


def _routing_tables(top_idx):
    m = top_idx.shape[0]
    n_assign = m * TOP_K
    bm = MOE_BLOCK
    n_blocks = -(-(n_assign + N_EXPERTS * (bm - 1)) // bm)
    p_rows = n_blocks * bm
    e_flat = top_idx.reshape(n_assign)
    order = jnp.argsort(e_flat, stable=True).astype(jnp.int32)
    counts = jnp.zeros((N_EXPERTS,), jnp.int32).at[e_flat].add(1)
    padded = (counts + bm - 1) // bm * bm
    start = jnp.cumsum(counts) - counts
    pend = jnp.cumsum(padded)
    pstart = pend - padded
    block_e = jnp.minimum(jnp.searchsorted(pend, jnp.arange(n_blocks, dtype=jnp.int32) * bm, side='right'),
                          N_EXPERTS - 1).astype(jnp.int32)
    row_e = jnp.repeat(block_e, bm)
    pos = jnp.arange(p_rows, dtype=jnp.int32) - pstart[row_e]
    is_pad = pos >= counts[row_e]
    src_assign = order[jnp.clip(start[row_e] + pos, 0, n_assign - 1)]
    tok = jnp.where(is_pad, 0, src_assign // TOP_K)
    pad_rank = (pstart[row_e] - start[row_e]) + (pos - counts[row_e])
    slot = jnp.where(is_pad, n_assign + pad_rank, (src_assign % TOP_K) * m + src_assign // TOP_K)
    slot_ext = jnp.concatenate([p_rows + jnp.arange(bm, dtype=jnp.int32), slot])
    return tok, slot_ext, block_e, p_rows + bm


def _block_ones(n):
    g = np.arange(n) // HEAD_DIM
    return jnp.asarray(g[:, None] == g[None, :], BF16)


def _sw_head_order():
    group = SW_HEADS // SW_KV_HEADS
    heads = [kv * group + g for g in range(group) for kv in range(SW_KV_HEADS)]
    return np.concatenate([np.arange(h * HEAD_DIM, (h + 1) * HEAD_DIM) for h in heads])


def _rope_tables(seq):
    half = HEAD_DIM // 2
    inv = ROPE_THETA ** (-jnp.arange(half, dtype=F32) / half)
    ang = jnp.arange(seq, dtype=F32)[:, None] * inv[None, :]
    cos, sin = jnp.cos(ang), jnp.sin(ang)
    reps = LANES // HEAD_DIM
    cos_t = jnp.tile(jnp.concatenate([cos, cos], axis=-1), (1, reps))
    sin_t = jnp.tile(jnp.concatenate([-sin, sin], axis=-1), (1, reps))
    return cos_t, sin_t


def kernel(x, w_in, b_in, rw_shift_mu, rw_w_up, rw_w0, rw_a_up, rw_a0, rw_g_up, rw_k_k, rw_k_a, rw_r_k, rw_ln_w, rw_ln_b, sw_sinks, ml_conv_w, ml_conv_b, ml_norm_w, w_out, ln1_w, ln1_b, router_w, router_b, exp_w_up, exp_b_up, exp_w_down, exp_b_down, ln2_w, ln2_b):
    batch, seq, d_model = x.shape
    assert d_model == D_MODEL and seq % ROW_TILE == 0 and seq % ML_CHUNK == 0 and seq % WINDOW == 0
    m = batch * seq
    depth = w_in.shape[0]
    row2 = lambda t: t.reshape(1, -1)

    sw_perm = _sw_head_order()
    in_cols = np.concatenate([np.arange(RW_COLS), RW_COLS + sw_perm, np.arange(RW_COLS + SW_DIM, C_IN)])
    ones_rw = _block_ones(RW_DIM)
    ones_ml = _block_ones(ML_DIM)
    cos_t, sin_t = _rope_tables(seq)
    src = np.arange(LANES)[:, None]
    dst = np.arange(2 * ML_DIM)[None, :]
    expand = jnp.asarray(src == (dst // ML_DIM) * ML_HEADS + (dst % ML_DIM) // HEAD_DIM, BF16)
    pj = np.arange(2 * LANES)[:, None]
    pc = np.arange(2 * LANES)[None, :]
    perm = jnp.asarray(pj == 2 * (pc % LANES) + pc // LANES, BF16)
    n_layers = exp_b_up.shape[0]
    b_glu = exp_b_up[:, :, 0::2].reshape(n_layers, N_EXPERTS, 1, D_EXPERT)
    b_lin = exp_b_up[:, :, 1::2].reshape(n_layers, N_EXPERTS, 1, D_EXPERT)
    b_dn = exp_b_down.reshape(n_layers, N_EXPERTS, 1, D_MODEL)

    h = x.reshape(m, D_MODEL)
    for l in range(depth):
        w_l = jnp.pad(w_in[l][:, in_cols], ((0, 0), (0, C_IN_PAD - C_IN))).astype(BF16)
        b_l = jnp.pad(b_in[l][in_cols], (0, C_IN_PAD - C_IN)).reshape(1, C_IN_PAD)
        zero = jnp.zeros((RW_LORA_W, RW_DIM), F32)
        wcomb = jnp.concatenate([jnp.concatenate([rw_w_up[l], zero], axis=1),
                                 jnp.concatenate([zero, rw_a_up[l]], axis=1)], axis=0).astype(BF16)
        w_o = w_out[l]
        w_o_rw = w_o[0:RW_DIM].astype(BF16)
        w_o_sw = w_o[RW_DIM + sw_perm].astype(BF16)
        w_o_ml = w_o[RW_DIM + SW_DIM:].astype(BF16)
        rw_pad = jnp.pad(router_w[l], ((0, 0), (0, LANES - N_EXPERTS)))
        rw_hi = rw_pad.astype(BF16)
        rw_r1 = rw_pad - rw_hi.astype(F32)
        rw_mid = rw_r1.astype(BF16)
        rw_lo = (rw_r1 - rw_mid.astype(F32)).astype(BF16)
        rb_pad = jnp.pad(router_b[l], (0, LANES - N_EXPERTS), constant_values=NEG_BIG).reshape(1, LANES)

        p_rw, p_sw, p_ml = _in_proj(h, w_l, b_l)
        r, lw, k, v, a, b, g = _rw_prep(p_rw, seq, row2(rw_shift_mu[l]), wcomb, rw_g_up[l].astype(BF16),
                                        row2(rw_w0[l]), row2(rw_a0[l]), row2(rw_k_k[l]), row2(rw_k_a[l]), ones_rw)
        m_mat, yh, g_mat, y0 = _rw_chunks(r, lw, k, v, a, b)
        y = _rw_scan(batch, seq, m_mat, yh, g_mat, y0).reshape(m, RW_DIM)
        rw_out = _rw_post(y, r, k, v, g, row2(rw_r_k[l]), row2(rw_ln_w[l]), row2(rw_ln_b[l]), ones_rw)
        sw_out = _swa(p_sw.reshape(batch, seq, SW_COLS), sw_sinks[l], cos_t, sin_t).reshape(m, SW_DIM)
        ml_out = _mlstm(p_ml.reshape(batch, seq, ML_COLS_PAD), ml_conv_w[l], row2(ml_conv_b[l]),
                        row2(ml_norm_w[l]), expand, ones_ml).reshape(m, ML_DIM)
        h1, idx_pad, gates = _out_proj(rw_out, sw_out, ml_out, h, w_o_rw, w_o_sw, w_o_ml, row2(ln1_w[l]),
                                       row2(ln1_b[l]), (rw_hi, rw_mid, rw_lo), rb_pad)

        tok, slot_ext, block_e, z_rows = _routing_tables(idx_pad[:, 0:TOP_K])
        z = _experts(l, block_e, tok, slot_ext, h1, exp_w_up, b_glu, b_lin, exp_w_down, b_dn, perm, z_rows)
        h = _combine(z, gates, h1, row2(ln2_w[l]), row2(ln2_b[l]))
    return h.reshape(batch, seq, D_MODEL)
```

```python
import functools

import numpy as np
import jax
import jax.numpy as jnp
from jax import lax
from jax.experimental import pallas as pl
from jax.experimental.pallas import tpu as pltpu

F32 = jnp.float32
BF16 = jnp.bfloat16

D_MODEL = 1024
HEAD_DIM = 64
RW_HEADS = 6
SW_HEADS = 6
SW_KV_HEADS = 2
ML_HEADS = 4
RW_DIM = RW_HEADS * HEAD_DIM
SW_DIM = SW_HEADS * HEAD_DIM
SW_KV_DIM = SW_KV_HEADS * HEAD_DIM
ML_DIM = ML_HEADS * HEAD_DIM
RW_LORA_W = 64
RW_LORA_A = 64
RW_LORA_G = 128
RW_COLS = 3 * RW_DIM + RW_LORA_W + RW_LORA_A + RW_LORA_G
SW_COLS = SW_DIM + 2 * SW_KV_DIM
ML_COLS = 4 * ML_DIM + 2 * ML_HEADS
C_IN = RW_COLS + SW_COLS + ML_COLS
WINDOW = 128
ROPE_THETA = 10000.0
CONV_WIDTH = 4
N_EXPERTS = 32
TOP_K = 4
D_EXPERT = 1024
SWIGLU_ALPHA = 1.702
SWIGLU_LIMIT = 7.0
LN_EPS = 1e-5
RW_GN_EPS = 64e-5
ML_GN_EPS = 1e-6
DEPTH = 4
DN_ALPHA = (2 * DEPTH) ** 0.25

LANES = 128
SUBLANES = 8
ML_COLS_PAD = 1152
C_IN_PAD = RW_COLS + SW_COLS + ML_COLS_PAD
ROW_TILE = 512
RW_CHUNK = 64
RW_PAIR = 2 * HEAD_DIM
RW_NPAIR = RW_HEADS // 2
ML_CHUNK = 256
MOE_BLOCK = 512
NEG_BIG = -1e30
VMEM_LIMIT = 56 * 1024 * 1024


def _cparams(*sem):
    return pltpu.CompilerParams(dimension_semantics=sem, vmem_limit_bytes=VMEM_LIMIT)


def _dot(a, b):
    return jnp.dot(a.astype(BF16), b.astype(BF16), preferred_element_type=F32)


def _dot_nt(a, b):
    return lax.dot_general(a.astype(BF16), b.astype(BF16), (((1,), (1,)), ((), ())),
                           preferred_element_type=F32)


def _split3(x):
    hi = x.astype(BF16)
    r1 = x - hi.astype(F32)
    mid = r1.astype(BF16)
    lo = (r1 - mid.astype(F32)).astype(BF16)
    return hi, mid, lo


def _dot_xl(x, w):
    hi, mid, lo = _split3(x)
    return (jnp.dot(hi, w, preferred_element_type=F32) + jnp.dot(mid, w, preferred_element_type=F32)
            + jnp.dot(lo, w, preferred_element_type=F32))


def _dot_lx(w, x):
    hi, mid, lo = _split3(x)
    return (jnp.dot(w, hi, preferred_element_type=F32) + jnp.dot(w, mid, preferred_element_type=F32)
            + jnp.dot(w, lo, preferred_element_type=F32))


def _softplus(z):
    return jnp.maximum(z, 0.0) + jnp.log(1.0 + jnp.exp(-jnp.abs(z)))


def _sigmoid(z):
    return 1.0 / (1.0 + jnp.exp(-z))


def _shift_rows(x, prev8, j):
    cat = jnp.concatenate([prev8, x], axis=0)
    return pltpu.roll(cat, j, 0)[SUBLANES:, :]


def _layer_norm(x, w, b):
    mu = jnp.mean(x, axis=-1, keepdims=True)
    d = x - mu
    var = jnp.mean(d * d, axis=-1, keepdims=True)
    return d * lax.rsqrt(var + LN_EPS) * w + b


def _to_row_tiles(ref, x):
    for j in range(x.shape[1] // LANES):
        ref[:, j, :] = x[:, j * LANES:(j + 1) * LANES]


def _from_row_tiles(ref):
    return jnp.concatenate([ref[:, j, :] for j in range(ref.shape[1])], axis=1)


def _head_norm(y, ones_bd, eps):
    inv = 1.0 / HEAD_DIM
    mu = _dot_xl(y, ones_bd) * inv
    d = y - mu
    var = _dot_xl(d * d, ones_bd) * inv
    return d * lax.rsqrt(var + eps)


def _inproj_kernel(x_ref, w_ref, b_ref, rw_ref, sw_ref, ml_ref):
    x = x_ref[...].astype(BF16)
    c1, c2 = RW_COLS, RW_COLS + SW_COLS
    rw_ref[...] = jnp.dot(x, w_ref[:, 0:c1], preferred_element_type=F32) + b_ref[:, 0:c1]
    sw_ref[...] = jnp.dot(x, w_ref[:, c1:c2], preferred_element_type=F32) + b_ref[:, c1:c2]
    ml_ref[...] = jnp.dot(x, w_ref[:, c2:C_IN_PAD], preferred_element_type=F32) + b_ref[:, c2:C_IN_PAD]


def _in_proj(h2d, w, b):
    m = h2d.shape[0]
    tm = ROW_TILE
    row = lambda i: (i, 0)
    fixed = lambda i: (0, 0)
    return pl.pallas_call(
        _inproj_kernel,
        grid=(m // tm,),
        in_specs=[pl.BlockSpec((tm, D_MODEL), row), pl.BlockSpec((D_MODEL, C_IN_PAD), fixed),
                  pl.BlockSpec((1, C_IN_PAD), fixed)],
        out_specs=[pl.BlockSpec((tm, RW_COLS), row), pl.BlockSpec((tm, SW_COLS), row),
                   pl.BlockSpec((tm, ML_COLS_PAD), row)],
        out_shape=[jax.ShapeDtypeStruct((m, RW_COLS), F32), jax.ShapeDtypeStruct((m, SW_COLS), F32),
                   jax.ShapeDtypeStruct((m, ML_COLS_PAD), F32)],
        compiler_params=_cparams("parallel"),
        name="in_proj",
    )(h2d, w, b)


def _rw_prep_kernel(tiles_per_seq, p_ref, pprev_ref, mu_ref, wcomb_ref, gup_ref, w0_ref, a0_ref, kk_ref,
                    ka_ref, ones_ref, r_o, lw_o, k_o, v_o, a_o, b_o, g_o):
    i = pl.program_id(0)
    x = p_ref[...]
    first = (i % tiles_per_seq) == 0
    prev8 = jnp.where(first, 0.0, pprev_ref[...])
    xs = x + (_shift_rows(x, prev8, 1) - x) * mu_ref[...]
    d = RW_DIM
    r = xs[:, 0:d]
    k = xs[:, d:2 * d]
    v = xs[:, 2 * d:3 * d]
    slab = xs[:, 3 * d:3 * d + LANES]
    lane = lax.broadcasted_iota(jnp.int32, slab.shape, 1)
    slab = jnp.where(lane < RW_LORA_W, jnp.tanh(slab), slab)
    wa = _dot(slab, wcomb_ref[...])
    w_pre = w0_ref[...] + wa[:, 0:d]
    a_pre = a0_ref[...] + wa[:, d:2 * d]
    w_log = -_softplus(-w_pre) - 0.5
    log_decay = -jnp.exp(w_log)
    a_sig = _sigmoid(a_pre)
    g = _dot(_sigmoid(xs[:, 3 * d + LANES:3 * d + 2 * LANES]), gup_ref[...])
    kk = k * kk_ref[...]
    ss = _dot_xl(kk * kk, ones_ref[...])
    kk = kk / jnp.maximum(jnp.sqrt(ss), 1e-12)
    r_o[...] = r
    lw_o[...] = log_decay
    k_o[...] = k * (1.0 + (a_sig - 1.0) * ka_ref[...])
    v_o[...] = v
    a_o[...] = -kk
    b_o[...] = kk * a_sig
    g_o[...] = g


def _rw_prep(p_rw, seq, mu, wcomb, gup, w0, a0, k_k, k_a, ones_bd):
    m = p_rw.shape[0]
    tm = ROW_TILE
    row = lambda i: (i, 0)
    fixed = lambda i: (0, 0)
    prev = lambda i: (jnp.maximum(i * (tm // SUBLANES) - 1, 0), 0)
    vec = pl.BlockSpec((1, RW_DIM), fixed)
    out = jax.ShapeDtypeStruct((m, RW_DIM), F32)
    return pl.pallas_call(
        functools.partial(_rw_prep_kernel, seq // tm),
        grid=(m // tm,),
        in_specs=[pl.BlockSpec((tm, RW_COLS), row), pl.BlockSpec((SUBLANES, RW_COLS), prev),
                  pl.BlockSpec((1, RW_COLS), fixed), pl.BlockSpec((LANES, 2 * RW_DIM), fixed),
                  pl.BlockSpec((RW_LORA_G, RW_DIM), fixed), vec, vec, vec, vec,
                  pl.BlockSpec((RW_DIM, RW_DIM), fixed)],
        out_specs=[pl.BlockSpec((tm, RW_DIM), row)] * 7,
        out_shape=[out] * 7,
        compiler_params=_cparams("parallel"),
        name="rw_prep",
    )(p_rw, p_rw, mu, wcomb, gup, w0, a0, k_k, k_a, ones_bd)


def _rw_chunk_kernel(n_chunk, r_ref, lw_ref, k_ref, v_ref, a_ref, b_ref, m_o, yh_o, g_o, y0_o):
    L = RW_CHUNK
    ti = lax.broadcasted_iota(jnp.int32, (L, L), 0)
    si = lax.broadcasted_iota(jnp.int32, (L, L), 1)
    tri = jnp.where(ti >= si, 1.0, 0.0).astype(BF16)
    n2 = 2 * L
    ri = lax.broadcasted_iota(jnp.int32, (n2, n2), 0)
    ci = lax.broadcasted_iota(jnp.int32, (n2, n2), 1)
    rr = jnp.where(ri >= L, ri - L, ri)
    cc = jnp.where(ci >= L, ci - L, ci)
    strict = rr > cc
    incl = rr >= cc
    eye = ri == ci
    eye_f = jnp.where(eye, 1.0, 0.0)
    lane = lax.broadcasted_iota(jnp.int32, (L, RW_PAIR), 1)
    head0 = lane < HEAD_DIM

    def bmm(x, y):
        return jnp.einsum('bmk,bkn->bmn', x.astype(BF16), y.astype(BF16), preferred_element_type=F32)

    def bmm_nt(x, y):
        return jnp.einsum('bmk,bnk->bmn', x.astype(BF16), y.astype(BF16), preferred_element_type=F32)

    def bmm_tn(x, y):
        return bmm(jnp.swapaxes(x, 1, 2), y)

    def stack(x):
        return jnp.concatenate([jnp.where(head0, x, 0.0), jnp.where(head0, 0.0, x)], axis=1)

    def chunked(ref, cols):
        return ref[:, cols].reshape(n_chunk, L, RW_PAIR)

    tri_b = jnp.broadcast_to(tri, (n_chunk, L, L))
    for p in range(RW_NPAIR):
        cols = slice(p * RW_PAIR, (p + 1) * RW_PAIR)
        lw = chunked(lw_ref, cols)
        hi, mid, lo = _split3(lw)
        tsum = lambda part: jnp.einsum('bts,bsn->btn', tri_b, part, preferred_element_type=F32)
        lc = tsum(hi) + tsum(mid) + tsum(lo)
        lcl = lc[:, L - 1:L, :]
        gam = jnp.exp(lc)
        gam_prev = jnp.exp(lc - lw)
        inv = jnp.exp(-lc)
        to_end = jnp.exp(lcl - lc)
        r = chunked(r_ref, cols)
        k = chunked(k_ref, cols)
        v = chunked(v_ref, cols)
        a = chunked(a_ref, cols)
        b = chunked(b_ref, cols)
        x1 = stack(a * gam_prev)
        r1 = stack(r * gam)
        x2 = stack(b * inv)
        k2 = stack(k * inv)
        vs = stack(v)
        bs = stack(b * to_end)
        ks = stack(k * to_end)
        aa = bmm_nt(jnp.concatenate([x1, r1], axis=1), jnp.concatenate([x2, k2], axis=1))
        a_ab = jnp.where(strict, aa[:, 0:n2, 0:n2], 0.0)
        a_ak = jnp.where(strict, aa[:, 0:n2, n2:2 * n2], 0.0)
        a_rb = jnp.where(incl, aa[:, n2:2 * n2, 0:n2], 0.0)
        a_rk = jnp.where(incl, aa[:, n2:2 * n2, n2:2 * n2], 0.0)
        apow = a_ab
        t_inv = eye_f + a_ab
        for _ in range(5):
            apow = bmm(apow, apow)
            t_inv = t_inv + bmm(t_inv, apow)
        akv = bmm(a_ak, vs)
        pw = bmm(t_inv, jnp.concatenate([x1, akv], axis=2))
        pmat = pw[:, :, 0:RW_PAIR]
        wmat = pw[:, :, RW_PAIR:2 * RW_PAIR]
        gl_row = jnp.exp(lcl)
        m_mat = bmm_tn(bs, pmat) + jnp.where(eye, gl_row, 0.0)
        g_mat = bmm_tn(jnp.concatenate([bs, ks], axis=1), jnp.concatenate([wmat, vs], axis=1))
        yy = bmm(a_rb, pw)
        yh = r1 + yy[:, :, 0:RW_PAIR]
        y0 = yy[:, :, RW_PAIR:2 * RW_PAIR] + bmm(a_rk, vs)
        m_o[:, p] = m_mat.astype(BF16)
        yh_o[:, p] = yh.astype(BF16)
        g_o[:, p] = g_mat
        y0_o[:, p] = y0


def _rw_chunks(r, lw, k, v, a, b):
    m = r.shape[0]
    tm = ROW_TILE
    nc = tm // RW_CHUNK
    row = lambda i: (i, 0)
    blk = lambda i: (i, 0, 0, 0)
    n2 = 2 * RW_CHUNK
    n_tot = m // RW_CHUNK
    ospec = pl.BlockSpec((nc, RW_NPAIR, n2, RW_PAIR), blk)
    return pl.pallas_call(
        functools.partial(_rw_chunk_kernel, nc),
        grid=(m // tm,),
        in_specs=[pl.BlockSpec((tm, RW_DIM), row)] * 6,
        out_specs=[ospec] * 4,
        out_shape=[jax.ShapeDtypeStruct((n_tot, RW_NPAIR, n2, RW_PAIR), BF16),
                   jax.ShapeDtypeStruct((n_tot, RW_NPAIR, n2, RW_PAIR), BF16),
                   jax.ShapeDtypeStruct((n_tot, RW_NPAIR, n2, RW_PAIR), F32),
                   jax.ShapeDtypeStruct((n_tot, RW_NPAIR, n2, RW_PAIR), F32)],
        compiler_params=_cparams("parallel"),
        name="rw_chunks",
    )(r, lw, k, v, a, b)


def _rw_scan_kernel(batch, m_ref, yh_ref, g_ref, y0_ref, y_o, h_ref):
    c = pl.program_id(0)

    @pl.when(c == 0)
    def _():
        h_ref[...] = jnp.zeros_like(h_ref)

    L = RW_CHUNK
    for bi in range(batch):
        for p in range(RW_NPAIR):
            hb = h_ref[bi, p].astype(BF16)
            yrows = jnp.dot(yh_ref[bi, 0, p], hb, preferred_element_type=F32) + y0_ref[bi, 0, p]
            y_o[bi, :, p * RW_PAIR:(p + 1) * RW_PAIR] = yrows[0:L, :] + yrows[L:2 * L, :]
            h_ref[bi, p] = jnp.dot(m_ref[bi, 0, p], hb, preferred_element_type=F32) + g_ref[bi, 0, p]


def _rw_scan(batch, seq, m_mat, yh, g_mat, y0):
    nc = seq // RW_CHUNK
    n2 = 2 * RW_CHUNK
    shp = (batch, nc, RW_NPAIR, n2, RW_PAIR)
    args = [t.reshape(shp) for t in (m_mat, yh, g_mat, y0)]
    ispec = pl.BlockSpec((batch, 1, RW_NPAIR, n2, RW_PAIR), lambda c: (0, c, 0, 0, 0))
    return pl.pallas_call(
        functools.partial(_rw_scan_kernel, batch),
        grid=(nc,),
        in_specs=[ispec] * 4,
        out_specs=pl.BlockSpec((batch, RW_CHUNK, RW_DIM), lambda c: (0, c, 0)),
        out_shape=jax.ShapeDtypeStruct((batch, seq, RW_DIM), F32),
        scratch_shapes=[pltpu.VMEM((batch, RW_NPAIR, RW_PAIR, RW_PAIR), F32)],
        compiler_params=_cparams("arbitrary"),
        name="rw_scan",
    )(*args)


def _rw_post_kernel(y_ref, r_ref, k_ref, v_ref, g_ref, rk_ref, lnw_ref, lnb_ref, ones_ref, o_ref):
    ones_bd = ones_ref[...]
    y = _head_norm(y_ref[...], ones_bd, RW_GN_EPS) * lnw_ref[...] + lnb_ref[...]
    bonus = _dot_xl(r_ref[...] * k_ref[...] * rk_ref[...], ones_bd) * v_ref[...]
    o_ref[...] = (y + bonus) * g_ref[...]


def _rw_post(y, r, k, v, g, r_k, ln_w, ln_b, ones_bd):
    m = y.shape[0]
    tm = ROW_TILE
    row = lambda i: (i, 0)
    fixed = lambda i: (0, 0)
    vec = pl.BlockSpec((1, RW_DIM), fixed)
    return pl.pallas_call(
        _rw_post_kernel,
        grid=(m // tm,),
        in_specs=[pl.BlockSpec((tm, RW_DIM), row)] * 5 + [vec, vec, vec, pl.BlockSpec((RW_DIM, RW_DIM), fixed)],
        out_specs=pl.BlockSpec((tm, RW_DIM), row),
        out_shape=jax.ShapeDtypeStruct((m, RW_DIM), F32),
        compiler_params=_cparams("parallel"),
        name="rw_post",
    )(y, r, k, v, g, r_k, ln_w, ln_b, ones_bd)


def _rope(x, cos, sin_signed):
    lane = lax.broadcasted_iota(jnp.int32, x.shape, 1)
    half = HEAD_DIM // 2
    first_half = (lane % HEAD_DIM) < half
    rot = jnp.where(first_half, pltpu.roll(x, LANES - half, 1), pltpu.roll(x, half, 1))
    return x * cos + rot * sin_signed


def _swa_kernel(sink_ref, q_ref, kc_ref, vc_ref, kp_ref, vp_ref, cosc_ref, sinc_ref, cosp_ref, sinp_ref, o_ref):
    j = pl.program_id(1)
    W = WINDOW
    cos_c, sin_c = cosc_ref[...], sinc_ref[...]
    k_cat = jnp.concatenate([_rope(kp_ref[0], cosp_ref[...], sinp_ref[...]), _rope(kc_ref[0], cos_c, sin_c)], axis=0)
    v_cat = jnp.concatenate([vp_ref[0], vc_ref[0]], axis=0).astype(BF16)
    k_cat = k_cat.astype(BF16)
    qi = lax.broadcasted_iota(jnp.int32, (W, 2 * W), 0)
    kj = lax.broadcasted_iota(jnp.int32, (W, 2 * W), 1)
    lo = jnp.where(j > 0, qi, jnp.maximum(qi, W - 1))
    bias = jnp.where((kj > lo) & (kj <= qi + W), 0.0, NEG_BIG)
    lane = lax.broadcasted_iota(jnp.int32, (W, LANES), 1)
    kv0 = lane < HEAD_DIM
    scale = HEAD_DIM ** -0.5
    group = SW_HEADS // SW_KV_HEADS
    for g in range(group):
        q = _rope(q_ref[0, :, g * LANES:(g + 1) * LANES], cos_c, sin_c) * scale
        outs = []
        for kv in range(SW_KV_HEADS):
            qm = jnp.where(kv0, q, 0.0) if kv == 0 else jnp.where(kv0, 0.0, q)
            s = _dot_nt(qm, k_cat) + bias
            sink = sink_ref[kv * group + g]
            mx = jnp.maximum(jnp.max(s, axis=-1, keepdims=True), sink)
            pr = jnp.exp(s - mx)
            den = jnp.sum(pr, axis=-1, keepdims=True) + jnp.exp(sink - mx)
            outs.append(jnp.dot(pr.astype(BF16), v_cat, preferred_element_type=F32) / den)
        o_ref[0, :, g * LANES:(g + 1) * LANES] = jnp.where(kv0, outs[0], outs[1])


def _swa(p_sw3, sinks, cos_t, sin_t):
    batch, seq, _ = p_sw3.shape
    W = WINDOW
    nb = seq // W
    kcol = SW_DIM // LANES
    cur = lambda c: (lambda b, j: (b, j, c))
    prv = lambda c: (lambda b, j: (b, jnp.maximum(j - 1, 0), c))
    tab_c = pl.BlockSpec((W, LANES), lambda b, j: (j, 0))
    tab_p = pl.BlockSpec((W, LANES), lambda b, j: (jnp.maximum(j - 1, 0), 0))
    kv_blk = lambda f: pl.BlockSpec((1, W, LANES), f)
    return pl.pallas_call(
        _swa_kernel,
        grid=(batch, nb),
        in_specs=[pl.BlockSpec(memory_space=pltpu.SMEM),
                  pl.BlockSpec((1, W, SW_DIM), cur(0)),
                  kv_blk(cur(kcol)), kv_blk(cur(kcol + 1)), kv_blk(prv(kcol)), kv_blk(prv(kcol + 1)),
                  tab_c, tab_c, tab_p, tab_p],
        out_specs=pl.BlockSpec((1, W, SW_DIM), cur(0)),
        out_shape=jax.ShapeDtypeStruct((batch, seq, SW_DIM), F32),
        compiler_params=_cparams("parallel", "parallel"),
        name="swa",
    )(sinks, p_sw3, p_sw3, p_sw3, p_sw3, p_sw3, cos_t, sin_t, cos_t, sin_t)


def _mlstm_kernel(p_ref, pprev_ref, cw_ref, cb_ref, nw_ref, expand_ref, ones_ref, o_ref, c_ref, nv_ref, m_ref):
    j = pl.program_id(1)
    L = ML_CHUNK
    d = ML_DIM

    @pl.when(j == 0)
    def _():
        c_ref[...] = jnp.zeros_like(c_ref)
        nv_ref[...] = jnp.zeros_like(nv_ref)
        m_ref[...] = jnp.zeros_like(m_ref)

    x = p_ref[0]
    qk_pre = x[:, 0:2 * d]
    prev8 = jnp.where(j == 0, 0.0, pprev_ref[0][:, 0:2 * d])
    conv = cb_ref[...] + cw_ref[CONV_WIDTH - 1:CONV_WIDTH, :] * qk_pre
    for s in range(1, CONV_WIDTH):
        conv = conv + cw_ref[CONV_WIDTH - 1 - s:CONV_WIDTH - s, :] * _shift_rows(qk_pre, prev8, s)
    qk = conv * _sigmoid(conv)
    q = qk[:, 0:d]
    k = qk[:, d:2 * d] * (HEAD_DIM ** -0.5)
    v = x[:, 2 * d:3 * d]
    o_gate = x[:, 3 * d:4 * d]
    gates = _dot_xl(x[:, 4 * d:4 * d + LANES], expand_ref[...])
    i_full = gates[:, 0:d]
    f_full = gates[:, d:2 * d]
    lf_full = jnp.minimum(f_full, 0.0) - jnp.log(1.0 + jnp.exp(-jnp.abs(f_full)))
    ti = lax.broadcasted_iota(jnp.int32, (L, L), 0)
    si = lax.broadcasted_iota(jnp.int32, (L, L), 1)
    causal = ti >= si
    tri = jnp.where(causal, 1.0, 0.0).astype(BF16)
    bc_full = _dot_lx(tri, lf_full)
    u_t = (i_full - bc_full).T
    lane = lax.broadcasted_iota(jnp.int32, (L, d), 1)
    lane_row = lax.broadcasted_iota(jnp.int32, (1, d), 1)
    q_b = q.astype(BF16)
    k_b = k.astype(BF16)
    m_prev_row = m_ref[...]
    nv_row = nv_ref[...]
    qn = q * nv_row
    num = jnp.zeros((L, d), F32)
    inter_full = jnp.zeros((L, d), F32)
    den_full = jnp.zeros((L, d), F32)
    mnew_row = jnp.zeros((1, d), F32)
    for h in range(ML_HEADS):
        c0 = h * HEAD_DIM
        in_head = (lane >= c0) & (lane < c0 + HEAD_DIM)
        in_head_row = (lane_row >= c0) & (lane_row < c0 + HEAD_DIM)
        bc_col = bc_full[:, c0:c0 + 1]
        dmat = jnp.where(causal, bc_col + u_t[c0:c0 + 1, :], NEG_BIG)
        m_inter = bc_col + m_prev_row[:, c0:c0 + 1]
        m_t = jnp.maximum(m_inter, jnp.max(dmat, axis=-1, keepdims=True))
        inter = jnp.exp(m_inter - m_t)
        e = jnp.exp(dmat - m_t)
        sm = _dot_nt(jnp.where(in_head, q, 0.0), k_b) * e
        num = num + _dot(sm, jnp.where(in_head, v, 0.0))
        nq = jnp.sum(sm, axis=-1, keepdims=True) + inter * jnp.sum(jnp.where(in_head, qn, 0.0), axis=-1, keepdims=True)
        den = jnp.maximum(jnp.abs(nq), jnp.exp(-m_t))
        inter_full = jnp.where(in_head, inter, inter_full)
        den_full = jnp.where(in_head, den, den_full)
        mnew_row = jnp.where(in_head_row, m_t[L - 1:L, :], mnew_row)
    c_mat = c_ref[...]
    num = num + inter_full * jnp.dot(q_b, c_mat.astype(BF16), preferred_element_type=F32)
    hout = num / den_full
    hn = _head_norm(hout, ones_ref[...], ML_GN_EPS) * nw_ref[...]
    o_ref[0] = _sigmoid(o_gate) * hn
    bcl_row = bc_full[L - 1:L, :]
    wst = jnp.exp(bcl_row - bc_full + i_full - mnew_row)
    dec_row = jnp.exp(bcl_row + m_prev_row - mnew_row)
    kw = k * wst
    ri = lax.broadcasted_iota(jnp.int32, (d, d), 0) // HEAD_DIM
    ci = lax.broadcasted_iota(jnp.int32, (d, d), 1) // HEAD_DIM
    c_ref[...] = dec_row * c_mat + jnp.where(ri == ci, _dot(kw.T, v), 0.0)
    nv_ref[...] = dec_row * nv_row + jnp.sum(kw, axis=0, keepdims=True)
    m_ref[...] = mnew_row


def _mlstm(p_ml3, conv_w, conv_b, norm_w, expand, ones_bd):
    batch, seq, _ = p_ml3.shape
    L = ML_CHUNK
    fixed = lambda b, j: (0, 0)
    return pl.pallas_call(
        _mlstm_kernel,
        grid=(batch, seq // L),
        in_specs=[pl.BlockSpec((1, L, ML_COLS_PAD), lambda b, j: (b, j, 0)),
                  pl.BlockSpec((1, SUBLANES, ML_COLS_PAD), lambda b, j: (b, jnp.maximum(j * (L // SUBLANES) - 1, 0), 0)),
                  pl.BlockSpec((CONV_WIDTH, 2 * ML_DIM), fixed), pl.BlockSpec((1, 2 * ML_DIM), fixed),
                  pl.BlockSpec((1, ML_DIM), fixed), pl.BlockSpec((LANES, 2 * ML_DIM), fixed),
                  pl.BlockSpec((ML_DIM, ML_DIM), fixed)],
        out_specs=pl.BlockSpec((1, L, ML_DIM), lambda b, j: (b, j, 0)),
        out_shape=jax.ShapeDtypeStruct((batch, seq, ML_DIM), F32),
        scratch_shapes=[pltpu.VMEM((ML_DIM, ML_DIM), F32), pltpu.VMEM((1, ML_DIM), F32),
                        pltpu.VMEM((1, ML_DIM), F32)],
        compiler_params=_cparams("parallel", "arbitrary"),
        name="mlstm",
    )(p_ml3, p_ml3, conv_w, conv_b, norm_w, expand, ones_bd)


def _outproj_kernel(rw_ref, sw_ref, ml_ref, h_ref, wrw_ref, wsw_ref, wml_ref, lnw_ref, lnb_ref,
                    rwh_ref, rwm_ref, rwl_ref, rb_ref, h1_o, h1t_o, idx_o, gate_o):
    mix = (_dot(rw_ref[...], wrw_ref[...]) + _dot(sw_ref[...], wsw_ref[...]) + _dot(ml_ref[...], wml_ref[...]))
    h1 = _layer_norm(DN_ALPHA * h_ref[...] + mix, lnw_ref[...], lnb_ref[...])
    h1_o[...] = h1
    _to_row_tiles(h1t_o, h1)
    xh, xm, xl = _split3(h1)
    wh, wm, wl = rwh_ref[...], rwm_ref[...], rwl_ref[...]
    dd = lambda a, b: jnp.dot(a, b, preferred_element_type=F32)
    logits = (dd(xh, wh) + (dd(xh, wm) + dd(xm, wh)) + (dd(xh, wl) + dd(xm, wm) + dd(xl, wh))) + rb_ref[...]
    lane = lax.broadcasted_iota(jnp.int32, logits.shape, 1).astype(F32)
    vals, idxs = [], []
    cur = logits
    for _ in range(TOP_K):
        mx = jnp.max(cur, axis=-1, keepdims=True)
        ix = jnp.min(jnp.where(cur == mx, lane, float(LANES)), axis=-1, keepdims=True)
        vals.append(mx)
        idxs.append(ix)
        cur = jnp.where(lane == ix, NEG_BIG * 2.0, cur)
    es = [jnp.exp(vv - vals[0]) for vv in vals]
    den = es[0] + es[1] + es[2] + es[3]
    idx_out = jnp.zeros_like(logits)
    gate_out = jnp.zeros_like(logits)
    for kk in range(TOP_K):
        idx_out = jnp.where(lane == float(kk), idxs[kk], idx_out)
        gate_out = jnp.where(lane == float(kk), es[kk] / den, gate_out)
    idx_o[...] = idx_out.astype(jnp.int32)
    gate_o[...] = gate_out


def _out_proj(rw_out, sw_out, ml_out, h2d, w_rw, w_sw, w_ml, ln_w, ln_b, rw_parts, rb):
    m = h2d.shape[0]
    tm = ROW_TILE
    row = lambda i: (i, 0)
    fixed = lambda i: (0, 0)
    full = lambda a: pl.BlockSpec(a.shape, fixed)
    return pl.pallas_call(
        _outproj_kernel,
        grid=(m // tm,),
        in_specs=[pl.BlockSpec((tm, RW_DIM), row), pl.BlockSpec((tm, SW_DIM), row), pl.BlockSpec((tm, ML_DIM), row),
                  pl.BlockSpec((tm, D_MODEL), row), full(w_rw), full(w_sw), full(w_ml), full(ln_w), full(ln_b),
                  full(rw_parts[0]), full(rw_parts[1]), full(rw_parts[2]), full(rb)],
        out_specs=[pl.BlockSpec((tm, D_MODEL), row), pl.BlockSpec((tm, SUBLANES, LANES), lambda i: (i, 0, 0)),
                   pl.BlockSpec((tm, LANES), row), pl.BlockSpec((tm, LANES), row)],
        out_shape=[jax.ShapeDtypeStruct((m, D_MODEL), F32), jax.ShapeDtypeStruct((m, SUBLANES, LANES), F32),
                   jax.ShapeDtypeStruct((m, LANES), jnp.int32), jax.ShapeDtypeStruct((m, LANES), F32)],
        compiler_params=_cparams("parallel"),
        name="out_proj_router",
    )(rw_out, sw_out, ml_out, h2d, w_rw, w_sw, w_ml, ln_w, ln_b, *rw_parts, rb)


def _expert_kernel(n_blocks, layer, be_ref, tokc_ref, tokn_ref, slot_ref, h_hbm, wup_ref, bg_ref, bl_ref,
                   wdn_ref, bd_ref, perm_ref, z_hbm, xbuf, ybuf, wg_s, wl_s, wd_s, gsem, ssem):
    del layer
    i = pl.program_id(0)
    s = i % 2
    bm = MOE_BLOCK

    def gather_row(tok, buf, r):
        return pltpu.make_async_copy(h_hbm.at[tok], xbuf.at[buf, r], gsem.at[buf])

    def scatter_row(dst, buf, r):
        return pltpu.make_async_copy(ybuf.at[buf, r], z_hbm.at[dst], ssem.at[buf])

    def wait_gather(buf):
        pltpu.make_async_copy(h_hbm.at[pl.ds(0, bm)], xbuf.at[buf], gsem.at[buf]).wait()

    def wait_scatter(buf):
        pltpu.make_async_copy(ybuf.at[buf], z_hbm.at[pl.ds(0, bm)], ssem.at[buf]).wait()

    @pl.when(i == 0)
    def _():
        ybuf[1] = jnp.zeros(ybuf.shape[1:], F32)

        def prime(r, c):
            gather_row(tokc_ref[r], 0, r).start()
            return c

        lax.fori_loop(0, bm, prime, 0, unroll=8)

    wait_gather(s)

    @pl.when(i > 0)
    def _():
        wait_scatter(s)

    blk = jnp.minimum(i, n_blocks - 1)
    e_now = be_ref[blk]
    e_prev = be_ref[jnp.maximum(blk - 1, 0)]

    @pl.when((i == 0) | (e_now != e_prev))
    def _():
        perm = perm_ref[...]
        for c in range(D_EXPERT // LANES):
            t = jnp.dot(wup_ref[0, 0, :, c * 2 * LANES:(c + 1) * 2 * LANES].astype(BF16), perm,
                        preferred_element_type=F32)
            wg_s[:, c * LANES:(c + 1) * LANES] = t[:, 0:LANES].astype(BF16)
            wl_s[:, c * LANES:(c + 1) * LANES] = t[:, LANES:2 * LANES].astype(BF16)
        wd_s[...] = wdn_ref[0, 0].astype(BF16)

    for r in range(bm):
        gather_row(tokn_ref[r], 1 - s, r).start()
        scatter_row(slot_ref[r], 1 - s, r).start()
    x = _from_row_tiles(xbuf.at[s]).astype(BF16)
    hg = jnp.dot(x, wg_s[...], preferred_element_type=F32) + bg_ref[0, 0]
    hl = jnp.dot(x, wl_s[...], preferred_element_type=F32) + bl_ref[0, 0]
    hg = jnp.minimum(hg, SWIGLU_LIMIT)
    hl = jnp.clip(hl, -SWIGLU_LIMIT, SWIGLU_LIMIT)
    act = hg * _sigmoid(SWIGLU_ALPHA * hg) * (hl + 1.0)
    _to_row_tiles(ybuf.at[s], jnp.dot(act.astype(BF16), wd_s[...], preferred_element_type=F32) + bd_ref[0, 0])

    @pl.when(i == n_blocks)
    def _():
        wait_gather(1 - s)
        wait_scatter(1 - s)


def _experts(layer, block_e, tok, slot_ext, h1, w_up, b_glu, b_lin, w_down, b_down, perm, z_rows):
    bm = MOE_BLOCK
    n_blocks = tok.shape[0] // bm
    last = n_blocks - 1
    by_e = lambda i, be: (layer, be[jnp.minimum(i, last)], 0, 0)
    smem = lambda f: pl.BlockSpec((bm,), f, memory_space=pltpu.SMEM)
    grid_spec = pltpu.PrefetchScalarGridSpec(
        num_scalar_prefetch=1,
        grid=(n_blocks + 1,),
        in_specs=[smem(lambda i, be: (jnp.minimum(i, last),)), smem(lambda i, be: (jnp.minimum(i + 1, last),)),
                  smem(lambda i, be: (i,)),
                  pl.BlockSpec(memory_space=pl.ANY),
                  pl.BlockSpec((1, 1, D_MODEL, 2 * D_EXPERT), by_e),
                  pl.BlockSpec((1, 1, 1, D_EXPERT), by_e), pl.BlockSpec((1, 1, 1, D_EXPERT), by_e),
                  pl.BlockSpec((1, 1, D_EXPERT, D_MODEL), by_e), pl.BlockSpec((1, 1, 1, D_MODEL), by_e),
                  pl.BlockSpec((2 * LANES, 2 * LANES), lambda i, be: (0, 0))],
        out_specs=pl.BlockSpec(memory_space=pl.ANY),
        scratch_shapes=[pltpu.VMEM((2, bm, SUBLANES, LANES), F32), pltpu.VMEM((2, bm, SUBLANES, LANES), F32),
                        pltpu.VMEM((D_MODEL, D_EXPERT), BF16), pltpu.VMEM((D_MODEL, D_EXPERT), BF16),
                        pltpu.VMEM((D_EXPERT, D_MODEL), BF16),
                        pltpu.SemaphoreType.DMA((2,)), pltpu.SemaphoreType.DMA((2,))],
    )
    return pl.pallas_call(
        functools.partial(_expert_kernel, n_blocks, layer),
        grid_spec=grid_spec,
        out_shape=jax.ShapeDtypeStruct((z_rows, SUBLANES, LANES), F32),
        compiler_params=pltpu.CompilerParams(dimension_semantics=("arbitrary",), vmem_limit_bytes=VMEM_LIMIT,
                                             disable_bounds_checks=True),
        name="experts",
    )(block_e, tok, tok, slot_ext, h1, w_up, b_glu, b_lin, w_down, b_down, perm)


def _combine_kernel(z0_ref, z1_ref, z2_ref, z3_ref, gate_ref, h_ref, lnw_ref, lnb_ref, o_ref):
    gates = gate_ref[...]
    y = gates[:, 0:1] * _from_row_tiles(z0_ref)
    for kk, z_ref in enumerate((z1_ref, z2_ref, z3_ref), start=1):
        y = y + gates[:, kk:kk + 1] * _from_row_tiles(z_ref)
    o_ref[...] = _layer_norm(DN_ALPHA * h_ref[...] + y, lnw_ref[...], lnb_ref[...])


def _combine(z, gates, h1, ln_w, ln_b):
    m = h1.shape[0]
    tm = ROW_TILE
    n_tiles = m // tm
    row = lambda i: (i, 0)
    fixed = lambda i: (0, 0)
    choice = lambda kk: pl.BlockSpec((tm, SUBLANES, LANES), lambda i: (kk * n_tiles + i, 0, 0))
    return pl.pallas_call(
        _combine_kernel,
        grid=(n_tiles,),
        in_specs=[choice(0), choice(1), choice(2), choice(3), pl.BlockSpec((tm, LANES), row),
                  pl.BlockSpec((tm, D_MODEL), row), pl.BlockSpec((1, D_MODEL), fixed), pl.BlockSpec((1, D_MODEL), fixed)],
        out_specs=pl.BlockSpec((tm, D_MODEL), row),
        out_shape=jax.ShapeDtypeStruct((m, D_MODEL), F32),
        compiler_params=_cparams("parallel"),
        name="combine_ln",
    )(z, z, z, z, gates, h1, ln_w, ln_b)


def _routing_tables(top_idx):
    m = top_idx.shape[0]
    n_assign = m * TOP_K
    bm = MOE_BLOCK
    n_blocks = -(-(n_assign + N_EXPERTS * (bm - 1)) // bm)
    p_rows = n_blocks * bm
    e_flat = top_idx.reshape(n_assign)
    order = jnp.argsort(e_flat, stable=True).astype(jnp.int32)
    counts = jnp.zeros((N_EXPERTS,), jnp.int32).at[e_flat].add(1)
    padded = (counts + bm - 1) // bm * bm
    start = jnp.cumsum(counts) - counts
    pend = jnp.cumsum(padded)
    pstart = pend - padded
    block_e = jnp.minimum(jnp.searchsorted(pend, jnp.arange(n_blocks, dtype=jnp.int32) * bm, side='right'),
                          N_EXPERTS - 1).astype(jnp.int32)
    row_e = jnp.repeat(block_e, bm)
    pos = jnp.arange(p_rows, dtype=jnp.int32) - pstart[row_e]
    is_pad = pos >= counts[row_e]
    src_assign = order[jnp.clip(start[row_e] + pos, 0, n_assign - 1)]
    tok = jnp.where(is_pad, 0, src_assign // TOP_K)
    pad_rank = (pstart[row_e] - start[row_e]) + (pos - counts[row_e])
    slot = jnp.where(is_pad, n_assign + pad_rank, (src_assign % TOP_K) * m + src_assign // TOP_K)
    slot_ext = jnp.concatenate([p_rows + jnp.arange(bm, dtype=jnp.int32), slot])
    return tok, slot_ext, block_e, p_rows + bm


def _block_ones(n):
    g = np.arange(n) // HEAD_DIM
    return jnp.asarray(g[:, None] == g[None, :], BF16)


def _sw_head_order():
    group = SW_HEADS // SW_KV_HEADS
    heads = [kv * group + g for g in range(group) for kv in range(SW_KV_HEADS)]
    return np.concatenate([np.arange(h * HEAD_DIM, (h + 1) * HEAD_DIM) for h in heads])


def _rope_tables(seq):
    half = HEAD_DIM // 2
    inv = ROPE_THETA ** (-jnp.arange(half, dtype=F32) / half)
    ang = jnp.arange(seq, dtype=F32)[:, None] * inv[None, :]
    cos, sin = jnp.cos(ang), jnp.sin(ang)
    reps = LANES // HEAD_DIM
    cos_t = jnp.tile(jnp.concatenate([cos, cos], axis=-1), (1, reps))
    sin_t = jnp.tile(jnp.concatenate([-sin, sin], axis=-1), (1, reps))
    return cos_t, sin_t


def kernel(x, w_in, b_in, rw_shift_mu, rw_w_up, rw_w0, rw_a_up, rw_a0, rw_g_up, rw_k_k, rw_k_a, rw_r_k, rw_ln_w, rw_ln_b, sw_sinks, ml_conv_w, ml_conv_b, ml_norm_w, w_out, ln1_w, ln1_b, router_w, router_b, exp_w_up, exp_b_up, exp_w_down, exp_b_down, ln2_w, ln2_b):
    batch, seq, d_model = x.shape
    assert d_model == D_MODEL and seq % ROW_TILE == 0 and seq % ML_CHUNK == 0 and seq % WINDOW == 0
    m = batch * seq
    depth = w_in.shape[0]
    row2 = lambda t: t.reshape(1, -1)

    sw_perm = _sw_head_order()
    in_cols = np.concatenate([np.arange(RW_COLS), RW_COLS + sw_perm, np.arange(RW_COLS + SW_DIM, C_IN)])
    ones_rw = _block_ones(RW_DIM)
    ones_ml = _block_ones(ML_DIM)
    cos_t, sin_t = _rope_tables(seq)
    src = np.arange(LANES)[:, None]
    dst = np.arange(2 * ML_DIM)[None, :]
    expand = jnp.asarray(src == (dst // ML_DIM) * ML_HEADS + (dst % ML_DIM) // HEAD_DIM, BF16)
    pj = np.arange(2 * LANES)[:, None]
    pc = np.arange(2 * LANES)[None, :]
    perm = jnp.asarray(pj == 2 * (pc % LANES) + pc // LANES, BF16)
    n_layers = exp_b_up.shape[0]
    b_glu = exp_b_up[:, :, 0::2].reshape(n_layers, N_EXPERTS, 1, D_EXPERT)
    b_lin = exp_b_up[:, :, 1::2].reshape(n_layers, N_EXPERTS, 1, D_EXPERT)
    b_dn = exp_b_down.reshape(n_layers, N_EXPERTS, 1, D_MODEL)

    h = x.reshape(m, D_MODEL)
    for l in range(depth):
        w_l = jnp.pad(w_in[l][:, in_cols], ((0, 0), (0, C_IN_PAD - C_IN))).astype(BF16)
        b_l = jnp.pad(b_in[l][in_cols], (0, C_IN_PAD - C_IN)).reshape(1, C_IN_PAD)
        zero = jnp.zeros((RW_LORA_W, RW_DIM), F32)
        wcomb = jnp.concatenate([jnp.concatenate([rw_w_up[l], zero], axis=1),
                                 jnp.concatenate([zero, rw_a_up[l]], axis=1)], axis=0).astype(BF16)
        w_o = w_out[l]
        w_o_rw = w_o[0:RW_DIM].astype(BF16)
        w_o_sw = w_o[RW_DIM + sw_perm].astype(BF16)
        w_o_ml = w_o[RW_DIM + SW_DIM:].astype(BF16)
        rw_pad = jnp.pad(router_w[l], ((0, 0), (0, LANES - N_EXPERTS)))
        rw_hi = rw_pad.astype(BF16)
        rw_r1 = rw_pad - rw_hi.astype(F32)
        rw_mid = rw_r1.astype(BF16)
        rw_lo = (rw_r1 - rw_mid.astype(F32)).astype(BF16)
        rb_pad = jnp.pad(router_b[l], (0, LANES - N_EXPERTS), constant_values=NEG_BIG).reshape(1, LANES)

        p_rw, p_sw, p_ml = _in_proj(h, w_l, b_l)
        r, lw, k, v, a, b, g = _rw_prep(p_rw, seq, row2(rw_shift_mu[l]), wcomb, rw_g_up[l].astype(BF16),
                                        row2(rw_w0[l]), row2(rw_a0[l]), row2(rw_k_k[l]), row2(rw_k_a[l]), ones_rw)
        m_mat, yh, g_mat, y0 = _rw_chunks(r, lw, k, v, a, b)
        y = _rw_scan(batch, seq, m_mat, yh, g_mat, y0).reshape(m, RW_DIM)
        rw_out = _rw_post(y, r, k, v, g, row2(rw_r_k[l]), row2(rw_ln_w[l]), row2(rw_ln_b[l]), ones_rw)
        sw_out = _swa(p_sw.reshape(batch, seq, SW_COLS), sw_sinks[l], cos_t, sin_t).reshape(m, SW_DIM)
        ml_out = _mlstm(p_ml.reshape(batch, seq, ML_COLS_PAD), ml_conv_w[l], row2(ml_conv_b[l]),
                        row2(ml_norm_w[l]), expand, ones_ml).reshape(m, ML_DIM)
        h1, h1_tiles, idx_pad, gates = _out_proj(rw_out, sw_out, ml_out, h, w_o_rw, w_o_sw, w_o_ml, row2(ln1_w[l]),
                                                 row2(ln1_b[l]), (rw_hi, rw_mid, rw_lo), rb_pad)

        tok, slot_ext, block_e, z_rows = _routing_tables(idx_pad[:, 0:TOP_K])
        z = _experts(l, block_e, tok, slot_ext, h1_tiles, exp_w_up, b_glu, b_lin, exp_w_down, b_dn, perm, z_rows)
        h = _combine(z, gates, h1, row2(ln2_w[l]), row2(ln2_b[l]))
    return h.reshape(batch, seq, D_MODEL)
```

```python
import functools

import numpy as np
import jax
import jax.numpy as jnp
from jax import lax
from jax.experimental import pallas as pl
from jax.experimental.pallas import tpu as pltpu

F32 = jnp.float32
BF16 = jnp.bfloat16

D_MODEL = 1024
HEAD_DIM = 64
RW_HEADS = 6
SW_HEADS = 6
SW_KV_HEADS = 2
ML_HEADS = 4
RW_DIM = RW_HEADS * HEAD_DIM
SW_DIM = SW_HEADS * HEAD_DIM
SW_KV_DIM = SW_KV_HEADS * HEAD_DIM
ML_DIM = ML_HEADS * HEAD_DIM
RW_LORA_W = 64
RW_LORA_A = 64
RW_LORA_G = 128
RW_COLS = 3 * RW_DIM + RW_LORA_W + RW_LORA_A + RW_LORA_G
SW_COLS = SW_DIM + 2 * SW_KV_DIM
ML_COLS = 4 * ML_DIM + 2 * ML_HEADS
C_IN = RW_COLS + SW_COLS + ML_COLS
WINDOW = 128
ROPE_THETA = 10000.0
CONV_WIDTH = 4
N_EXPERTS = 32
TOP_K = 4
D_EXPERT = 1024
SWIGLU_ALPHA = 1.702
SWIGLU_LIMIT = 7.0
LN_EPS = 1e-5
RW_GN_EPS = 64e-5
ML_GN_EPS = 1e-6
DEPTH = 4
DN_ALPHA = (2 * DEPTH) ** 0.25

LANES = 128
SUBLANES = 8
ML_COLS_PAD = 1152
C_IN_PAD = RW_COLS + SW_COLS + ML_COLS_PAD
ROW_TILE = 512
RW_CHUNK = 64
RW_PAIR = 2 * HEAD_DIM
RW_NPAIR = RW_HEADS // 2
ML_CHUNK = 256
MOE_BLOCK = 512
NEG_BIG = -1e30
VMEM_LIMIT = 56 * 1024 * 1024


def _cparams(*sem):
    return pltpu.CompilerParams(dimension_semantics=sem, vmem_limit_bytes=VMEM_LIMIT)


def _dot(a, b):
    return jnp.dot(a.astype(BF16), b.astype(BF16), preferred_element_type=F32)


def _dot_nt(a, b):
    return lax.dot_general(a.astype(BF16), b.astype(BF16), (((1,), (1,)), ((), ())),
                           preferred_element_type=F32)


def _split3(x):
    hi = x.astype(BF16)
    r1 = x - hi.astype(F32)
    mid = r1.astype(BF16)
    lo = (r1 - mid.astype(F32)).astype(BF16)
    return hi, mid, lo


def _dot_xl(x, w):
    hi, mid, lo = _split3(x)
    return (jnp.dot(hi, w, preferred_element_type=F32) + jnp.dot(mid, w, preferred_element_type=F32)
            + jnp.dot(lo, w, preferred_element_type=F32))


def _dot_lx(w, x):
    hi, mid, lo = _split3(x)
    return (jnp.dot(w, hi, preferred_element_type=F32) + jnp.dot(w, mid, preferred_element_type=F32)
            + jnp.dot(w, lo, preferred_element_type=F32))


def _softplus(z):
    return jnp.maximum(z, 0.0) + jnp.log(1.0 + jnp.exp(-jnp.abs(z)))


def _sigmoid(z):
    return 1.0 / (1.0 + jnp.exp(-z))


def _shift_rows(x, prev8, j):
    cat = jnp.concatenate([prev8, x], axis=0)
    return pltpu.roll(cat, j, 0)[SUBLANES:, :]


def _layer_norm(x, w, b):
    mu = jnp.mean(x, axis=-1, keepdims=True)
    d = x - mu
    var = jnp.mean(d * d, axis=-1, keepdims=True)
    return d * lax.rsqrt(var + LN_EPS) * w + b


def _to_row_tiles(ref, x):
    for j in range(x.shape[1] // LANES):
        ref[:, j, :] = x[:, j * LANES:(j + 1) * LANES]


def _from_row_tiles(ref):
    return jnp.concatenate([ref[:, j, :] for j in range(ref.shape[1])], axis=1)


def _head_norm(y, ones_bd, eps):
    inv = 1.0 / HEAD_DIM
    mu = _dot_xl(y, ones_bd) * inv
    d = y - mu
    var = _dot_xl(d * d, ones_bd) * inv
    return d * lax.rsqrt(var + eps)


def _inproj_kernel(x_ref, w_ref, b_ref, rw_ref, sw_ref, ml_ref):
    x = x_ref[...].astype(BF16)
    c1, c2 = RW_COLS, RW_COLS + SW_COLS
    rw_ref[...] = jnp.dot(x, w_ref[:, 0:c1], preferred_element_type=F32) + b_ref[:, 0:c1]
    sw_ref[...] = jnp.dot(x, w_ref[:, c1:c2], preferred_element_type=F32) + b_ref[:, c1:c2]
    ml_ref[...] = jnp.dot(x, w_ref[:, c2:C_IN_PAD], preferred_element_type=F32) + b_ref[:, c2:C_IN_PAD]


def _in_proj(h2d, w, b):
    m = h2d.shape[0]
    tm = ROW_TILE
    row = lambda i: (i, 0)
    fixed = lambda i: (0, 0)
    return pl.pallas_call(
        _inproj_kernel,
        grid=(m // tm,),
        in_specs=[pl.BlockSpec((tm, D_MODEL), row), pl.BlockSpec((D_MODEL, C_IN_PAD), fixed),
                  pl.BlockSpec((1, C_IN_PAD), fixed)],
        out_specs=[pl.BlockSpec((tm, RW_COLS), row), pl.BlockSpec((tm, SW_COLS), row),
                   pl.BlockSpec((tm, ML_COLS_PAD), row)],
        out_shape=[jax.ShapeDtypeStruct((m, RW_COLS), F32), jax.ShapeDtypeStruct((m, SW_COLS), F32),
                   jax.ShapeDtypeStruct((m, ML_COLS_PAD), F32)],
        compiler_params=_cparams("parallel"),
        name="in_proj",
    )(h2d, w, b)


def _rw_prep_kernel(tiles_per_seq, p_ref, pprev_ref, mu_ref, wcomb_ref, gup_ref, w0_ref, a0_ref, kk_ref,
                    ka_ref, ones_ref, r_o, lw_o, k_o, v_o, a_o, b_o, g_o):
    i = pl.program_id(0)
    x = p_ref[...]
    first = (i % tiles_per_seq) == 0
    prev8 = jnp.where(first, 0.0, pprev_ref[...])
    xs = x + (_shift_rows(x, prev8, 1) - x) * mu_ref[...]
    d = RW_DIM
    r = xs[:, 0:d]
    k = xs[:, d:2 * d]
    v = xs[:, 2 * d:3 * d]
    slab = xs[:, 3 * d:3 * d + LANES]
    lane = lax.broadcasted_iota(jnp.int32, slab.shape, 1)
    slab = jnp.where(lane < RW_LORA_W, jnp.tanh(slab), slab)
    wa = _dot(slab, wcomb_ref[...])
    w_pre = w0_ref[...] + wa[:, 0:d]
    a_pre = a0_ref[...] + wa[:, d:2 * d]
    w_log = -_softplus(-w_pre) - 0.5
    log_decay = -jnp.exp(w_log)
    a_sig = _sigmoid(a_pre)
    g = _dot(_sigmoid(xs[:, 3 * d + LANES:3 * d + 2 * LANES]), gup_ref[...])
    kk = k * kk_ref[...]
    ss = _dot_xl(kk * kk, ones_ref[...])
    kk = kk / jnp.maximum(jnp.sqrt(ss), 1e-12)
    r_o[...] = r
    lw_o[...] = log_decay
    k_o[...] = k * (1.0 + (a_sig - 1.0) * ka_ref[...])
    v_o[...] = v
    a_o[...] = -kk
    b_o[...] = kk * a_sig
    g_o[...] = g


def _rw_prep(p_rw, seq, mu, wcomb, gup, w0, a0, k_k, k_a, ones_bd):
    m = p_rw.shape[0]
    tm = ROW_TILE
    row = lambda i: (i, 0)
    fixed = lambda i: (0, 0)
    prev = lambda i: (jnp.maximum(i * (tm // SUBLANES) - 1, 0), 0)
    vec = pl.BlockSpec((1, RW_DIM), fixed)
    out = jax.ShapeDtypeStruct((m, RW_DIM), F32)
    return pl.pallas_call(
        functools.partial(_rw_prep_kernel, seq // tm),
        grid=(m // tm,),
        in_specs=[pl.BlockSpec((tm, RW_COLS), row), pl.BlockSpec((SUBLANES, RW_COLS), prev),
                  pl.BlockSpec((1, RW_COLS), fixed), pl.BlockSpec((LANES, 2 * RW_DIM), fixed),
                  pl.BlockSpec((RW_LORA_G, RW_DIM), fixed), vec, vec, vec, vec,
                  pl.BlockSpec((RW_DIM, RW_DIM), fixed)],
        out_specs=[pl.BlockSpec((tm, RW_DIM), row)] * 7,
        out_shape=[out] * 7,
        compiler_params=_cparams("parallel"),
        name="rw_prep",
    )(p_rw, p_rw, mu, wcomb, gup, w0, a0, k_k, k_a, ones_bd)


def _rw_chunk_kernel(n_chunk, r_ref, lw_ref, k_ref, v_ref, a_ref, b_ref, m_o, yh_o, g_o, y0_o):
    L = RW_CHUNK
    ti = lax.broadcasted_iota(jnp.int32, (L, L), 0)
    si = lax.broadcasted_iota(jnp.int32, (L, L), 1)
    tri = jnp.where(ti >= si, 1.0, 0.0).astype(BF16)
    n2 = 2 * L
    ri = lax.broadcasted_iota(jnp.int32, (n2, n2), 0)
    ci = lax.broadcasted_iota(jnp.int32, (n2, n2), 1)
    rr = jnp.where(ri >= L, ri - L, ri)
    cc = jnp.where(ci >= L, ci - L, ci)
    strict = rr > cc
    incl = rr >= cc
    eye = ri == ci
    eye_f = jnp.where(eye, 1.0, 0.0)
    lane = lax.broadcasted_iota(jnp.int32, (L, RW_PAIR), 1)
    head0 = lane < HEAD_DIM

    def bmm(x, y):
        return jnp.einsum('bmk,bkn->bmn', x.astype(BF16), y.astype(BF16), preferred_element_type=F32)

    def bmm_nt(x, y):
        return jnp.einsum('bmk,bnk->bmn', x.astype(BF16), y.astype(BF16), preferred_element_type=F32)

    def bmm_tn(x, y):
        return bmm(jnp.swapaxes(x, 1, 2), y)

    def stack(x):
        return jnp.concatenate([jnp.where(head0, x, 0.0), jnp.where(head0, 0.0, x)], axis=1)

    def chunked(ref, cols):
        return ref[:, cols].reshape(n_chunk, L, RW_PAIR)

    tri_b = jnp.broadcast_to(tri, (n_chunk, L, L))
    for p in range(RW_NPAIR):
        cols = slice(p * RW_PAIR, (p + 1) * RW_PAIR)
        lw = chunked(lw_ref, cols)
        hi, mid, lo = _split3(lw)
        tsum = lambda part: jnp.einsum('bts,bsn->btn', tri_b, part, preferred_element_type=F32)
        lc = tsum(hi) + tsum(mid) + tsum(lo)
        lcl = lc[:, L - 1:L, :]
        gam = jnp.exp(lc)
        gam_prev = jnp.exp(lc - lw)
        inv = jnp.exp(-lc)
        to_end = jnp.exp(lcl - lc)
        r = chunked(r_ref, cols)
        k = chunked(k_ref, cols)
        v = chunked(v_ref, cols)
        a = chunked(a_ref, cols)
        b = chunked(b_ref, cols)
        x1 = stack(a * gam_prev)
        r1 = stack(r * gam)
        x2 = stack(b * inv)
        k2 = stack(k * inv)
        vs = stack(v)
        bs = stack(b * to_end)
        ks = stack(k * to_end)
        aa = bmm_nt(jnp.concatenate([x1, r1], axis=1), jnp.concatenate([x2, k2], axis=1))
        a_ab = jnp.where(strict, aa[:, 0:n2, 0:n2], 0.0)
        a_ak = jnp.where(strict, aa[:, 0:n2, n2:2 * n2], 0.0)
        a_rb = jnp.where(incl, aa[:, n2:2 * n2, 0:n2], 0.0)
        a_rk = jnp.where(incl, aa[:, n2:2 * n2, n2:2 * n2], 0.0)
        apow = a_ab
        t_inv = eye_f + a_ab
        for _ in range(5):
            apow = bmm(apow, apow)
            t_inv = t_inv + bmm(t_inv, apow)
        akv = bmm(a_ak, vs)
        pw = bmm(t_inv, jnp.concatenate([x1, akv], axis=2))
        pmat = pw[:, :, 0:RW_PAIR]
        wmat = pw[:, :, RW_PAIR:2 * RW_PAIR]
        gl_row = jnp.exp(lcl)
        m_mat = bmm_tn(bs, pmat) + jnp.where(eye, gl_row, 0.0)
        g_mat = bmm_tn(jnp.concatenate([bs, ks], axis=1), jnp.concatenate([wmat, vs], axis=1))
        yy = bmm(a_rb, pw)
        yh = r1 + yy[:, :, 0:RW_PAIR]
        y0 = yy[:, :, RW_PAIR:2 * RW_PAIR] + bmm(a_rk, vs)
        m_o[:, p] = m_mat.astype(BF16)
        yh_o[:, p] = yh.astype(BF16)
        g_o[:, p] = g_mat
        y0_o[:, p] = y0


def _rw_chunks(r, lw, k, v, a, b):
    m = r.shape[0]
    tm = ROW_TILE
    nc = tm // RW_CHUNK
    row = lambda i: (i, 0)
    blk = lambda i: (i, 0, 0, 0)
    n2 = 2 * RW_CHUNK
    n_tot = m // RW_CHUNK
    ospec = pl.BlockSpec((nc, RW_NPAIR, n2, RW_PAIR), blk)
    return pl.pallas_call(
        functools.partial(_rw_chunk_kernel, nc),
        grid=(m // tm,),
        in_specs=[pl.BlockSpec((tm, RW_DIM), row)] * 6,
        out_specs=[ospec] * 4,
        out_shape=[jax.ShapeDtypeStruct((n_tot, RW_NPAIR, n2, RW_PAIR), BF16),
                   jax.ShapeDtypeStruct((n_tot, RW_NPAIR, n2, RW_PAIR), BF16),
                   jax.ShapeDtypeStruct((n_tot, RW_NPAIR, n2, RW_PAIR), F32),
                   jax.ShapeDtypeStruct((n_tot, RW_NPAIR, n2, RW_PAIR), F32)],
        compiler_params=_cparams("parallel"),
        name="rw_chunks",
    )(r, lw, k, v, a, b)


def _rw_scan_kernel(batch, m_ref, yh_ref, g_ref, y0_ref, y_o, h_ref):
    c = pl.program_id(0)

    @pl.when(c == 0)
    def _():
        h_ref[...] = jnp.zeros_like(h_ref)

    L = RW_CHUNK
    for bi in range(batch):
        for p in range(RW_NPAIR):
            hb = h_ref[bi, p].astype(BF16)
            yrows = jnp.dot(yh_ref[bi, 0, p], hb, preferred_element_type=F32) + y0_ref[bi, 0, p]
            y_o[bi, :, p * RW_PAIR:(p + 1) * RW_PAIR] = yrows[0:L, :] + yrows[L:2 * L, :]
            h_ref[bi, p] = jnp.dot(m_ref[bi, 0, p], hb, preferred_element_type=F32) + g_ref[bi, 0, p]


def _rw_scan(batch, seq, m_mat, yh, g_mat, y0):
    nc = seq // RW_CHUNK
    n2 = 2 * RW_CHUNK
    shp = (batch, nc, RW_NPAIR, n2, RW_PAIR)
    args = [t.reshape(shp) for t in (m_mat, yh, g_mat, y0)]
    ispec = pl.BlockSpec((batch, 1, RW_NPAIR, n2, RW_PAIR), lambda c: (0, c, 0, 0, 0))
    return pl.pallas_call(
        functools.partial(_rw_scan_kernel, batch),
        grid=(nc,),
        in_specs=[ispec] * 4,
        out_specs=pl.BlockSpec((batch, RW_CHUNK, RW_DIM), lambda c: (0, c, 0)),
        out_shape=jax.ShapeDtypeStruct((batch, seq, RW_DIM), F32),
        scratch_shapes=[pltpu.VMEM((batch, RW_NPAIR, RW_PAIR, RW_PAIR), F32)],
        compiler_params=_cparams("arbitrary"),
        name="rw_scan",
    )(*args)


def _rw_post_kernel(y_ref, r_ref, k_ref, v_ref, g_ref, rk_ref, lnw_ref, lnb_ref, ones_ref, o_ref):
    ones_bd = ones_ref[...]
    y = _head_norm(y_ref[...], ones_bd, RW_GN_EPS) * lnw_ref[...] + lnb_ref[...]
    bonus = _dot_xl(r_ref[...] * k_ref[...] * rk_ref[...], ones_bd) * v_ref[...]
    o_ref[...] = (y + bonus) * g_ref[...]


def _rw_post(y, r, k, v, g, r_k, ln_w, ln_b, ones_bd):
    m = y.shape[0]
    tm = ROW_TILE
    row = lambda i: (i, 0)
    fixed = lambda i: (0, 0)
    vec = pl.BlockSpec((1, RW_DIM), fixed)
    return pl.pallas_call(
        _rw_post_kernel,
        grid=(m // tm,),
        in_specs=[pl.BlockSpec((tm, RW_DIM), row)] * 5 + [vec, vec, vec, pl.BlockSpec((RW_DIM, RW_DIM), fixed)],
        out_specs=pl.BlockSpec((tm, RW_DIM), row),
        out_shape=jax.ShapeDtypeStruct((m, RW_DIM), F32),
        compiler_params=_cparams("parallel"),
        name="rw_post",
    )(y, r, k, v, g, r_k, ln_w, ln_b, ones_bd)


def _rope(x, cos, sin_signed):
    lane = lax.broadcasted_iota(jnp.int32, x.shape, 1)
    half = HEAD_DIM // 2
    first_half = (lane % HEAD_DIM) < half
    rot = jnp.where(first_half, pltpu.roll(x, LANES - half, 1), pltpu.roll(x, half, 1))
    return x * cos + rot * sin_signed


def _swa_kernel(sink_ref, q_ref, kc_ref, vc_ref, kp_ref, vp_ref, cosc_ref, sinc_ref, cosp_ref, sinp_ref, o_ref):
    j = pl.program_id(1)
    W = WINDOW
    cos_c, sin_c = cosc_ref[...], sinc_ref[...]
    k_cat = jnp.concatenate([_rope(kp_ref[0], cosp_ref[...], sinp_ref[...]), _rope(kc_ref[0], cos_c, sin_c)], axis=0)
    v_cat = jnp.concatenate([vp_ref[0], vc_ref[0]], axis=0).astype(BF16)
    k_cat = k_cat.astype(BF16)
    qi = lax.broadcasted_iota(jnp.int32, (W, 2 * W), 0)
    kj = lax.broadcasted_iota(jnp.int32, (W, 2 * W), 1)
    lo = jnp.where(j > 0, qi, jnp.maximum(qi, W - 1))
    bias = jnp.where((kj > lo) & (kj <= qi + W), 0.0, NEG_BIG)
    lane = lax.broadcasted_iota(jnp.int32, (W, LANES), 1)
    kv0 = lane < HEAD_DIM
    scale = HEAD_DIM ** -0.5
    group = SW_HEADS // SW_KV_HEADS
    for g in range(group):
        q = _rope(q_ref[0, :, g * LANES:(g + 1) * LANES], cos_c, sin_c) * scale
        outs = []
        for kv in range(SW_KV_HEADS):
            qm = jnp.where(kv0, q, 0.0) if kv == 0 else jnp.where(kv0, 0.0, q)
            s = _dot_nt(qm, k_cat) + bias
            sink = sink_ref[kv * group + g]
            mx = jnp.maximum(jnp.max(s, axis=-1, keepdims=True), sink)
            pr = jnp.exp(s - mx)
            den = jnp.sum(pr, axis=-1, keepdims=True) + jnp.exp(sink - mx)
            outs.append(jnp.dot(pr.astype(BF16), v_cat, preferred_element_type=F32) / den)
        o_ref[0, :, g * LANES:(g + 1) * LANES] = jnp.where(kv0, outs[0], outs[1])


def _swa(p_sw3, sinks, cos_t, sin_t):
    batch, seq, _ = p_sw3.shape
    W = WINDOW
    nb = seq // W
    kcol = SW_DIM // LANES
    cur = lambda c: (lambda b, j: (b, j, c))
    prv = lambda c: (lambda b, j: (b, jnp.maximum(j - 1, 0), c))
    tab_c = pl.BlockSpec((W, LANES), lambda b, j: (j, 0))
    tab_p = pl.BlockSpec((W, LANES), lambda b, j: (jnp.maximum(j - 1, 0), 0))
    kv_blk = lambda f: pl.BlockSpec((1, W, LANES), f)
    return pl.pallas_call(
        _swa_kernel,
        grid=(batch, nb),
        in_specs=[pl.BlockSpec(memory_space=pltpu.SMEM),
                  pl.BlockSpec((1, W, SW_DIM), cur(0)),
                  kv_blk(cur(kcol)), kv_blk(cur(kcol + 1)), kv_blk(prv(kcol)), kv_blk(prv(kcol + 1)),
                  tab_c, tab_c, tab_p, tab_p],
        out_specs=pl.BlockSpec((1, W, SW_DIM), cur(0)),
        out_shape=jax.ShapeDtypeStruct((batch, seq, SW_DIM), F32),
        compiler_params=_cparams("parallel", "parallel"),
        name="swa",
    )(sinks, p_sw3, p_sw3, p_sw3, p_sw3, p_sw3, cos_t, sin_t, cos_t, sin_t)


def _mlstm_kernel(p_ref, pprev_ref, cw_ref, cb_ref, nw_ref, expand_ref, ones_ref, o_ref, c_ref, nv_ref, m_ref):
    j = pl.program_id(1)
    L = ML_CHUNK
    d = ML_DIM

    @pl.when(j == 0)
    def _():
        c_ref[...] = jnp.zeros_like(c_ref)
        nv_ref[...] = jnp.zeros_like(nv_ref)
        m_ref[...] = jnp.zeros_like(m_ref)

    x = p_ref[0]
    qk_pre = x[:, 0:2 * d]
    prev8 = jnp.where(j == 0, 0.0, pprev_ref[0][:, 0:2 * d])
    conv = cb_ref[...] + cw_ref[CONV_WIDTH - 1:CONV_WIDTH, :] * qk_pre
    for s in range(1, CONV_WIDTH):
        conv = conv + cw_ref[CONV_WIDTH - 1 - s:CONV_WIDTH - s, :] * _shift_rows(qk_pre, prev8, s)
    qk = conv * _sigmoid(conv)
    q = qk[:, 0:d]
    k = qk[:, d:2 * d] * (HEAD_DIM ** -0.5)
    v = x[:, 2 * d:3 * d]
    o_gate = x[:, 3 * d:4 * d]
    gates = _dot_xl(x[:, 4 * d:4 * d + LANES], expand_ref[...])
    i_full = gates[:, 0:d]
    f_full = gates[:, d:2 * d]
    lf_full = jnp.minimum(f_full, 0.0) - jnp.log(1.0 + jnp.exp(-jnp.abs(f_full)))
    ti = lax.broadcasted_iota(jnp.int32, (L, L), 0)
    si = lax.broadcasted_iota(jnp.int32, (L, L), 1)
    causal = ti >= si
    tri = jnp.where(causal, 1.0, 0.0).astype(BF16)
    bc_full = _dot_lx(tri, lf_full)
    u_t = (i_full - bc_full).T
    lane = lax.broadcasted_iota(jnp.int32, (L, d), 1)
    lane_row = lax.broadcasted_iota(jnp.int32, (1, d), 1)
    q_b = q.astype(BF16)
    k_b = k.astype(BF16)
    m_prev_row = m_ref[...]
    nv_row = nv_ref[...]
    qn = q * nv_row
    num = jnp.zeros((L, d), F32)
    inter_full = jnp.zeros((L, d), F32)
    den_full = jnp.zeros((L, d), F32)
    mnew_row = jnp.zeros((1, d), F32)
    for h in range(ML_HEADS):
        c0 = h * HEAD_DIM
        in_head = (lane >= c0) & (lane < c0 + HEAD_DIM)
        in_head_row = (lane_row >= c0) & (lane_row < c0 + HEAD_DIM)
        bc_col = bc_full[:, c0:c0 + 1]
        dmat = jnp.where(causal, bc_col + u_t[c0:c0 + 1, :], NEG_BIG)
        m_inter = bc_col + m_prev_row[:, c0:c0 + 1]
        m_t = jnp.maximum(m_inter, jnp.max(dmat, axis=-1, keepdims=True))
        inter = jnp.exp(m_inter - m_t)
        e = jnp.exp(dmat - m_t)
        sm = _dot_nt(jnp.where(in_head, q, 0.0), k_b) * e
        num = num + _dot(sm, jnp.where(in_head, v, 0.0))
        nq = jnp.sum(sm, axis=-1, keepdims=True) + inter * jnp.sum(jnp.where(in_head, qn, 0.0), axis=-1, keepdims=True)
        den = jnp.maximum(jnp.abs(nq), jnp.exp(-m_t))
        inter_full = jnp.where(in_head, inter, inter_full)
        den_full = jnp.where(in_head, den, den_full)
        mnew_row = jnp.where(in_head_row, m_t[L - 1:L, :], mnew_row)
    c_mat = c_ref[...]
    num = num + inter_full * jnp.dot(q_b, c_mat.astype(BF16), preferred_element_type=F32)
    hout = num / den_full
    hn = _head_norm(hout, ones_ref[...], ML_GN_EPS) * nw_ref[...]
    o_ref[0] = _sigmoid(o_gate) * hn
    bcl_row = bc_full[L - 1:L, :]
    wst = jnp.exp(bcl_row - bc_full + i_full - mnew_row)
    dec_row = jnp.exp(bcl_row + m_prev_row - mnew_row)
    kw = k * wst
    ri = lax.broadcasted_iota(jnp.int32, (d, d), 0) // HEAD_DIM
    ci = lax.broadcasted_iota(jnp.int32, (d, d), 1) // HEAD_DIM
    c_ref[...] = dec_row * c_mat + jnp.where(ri == ci, _dot(kw.T, v), 0.0)
    nv_ref[...] = dec_row * nv_row + jnp.sum(kw, axis=0, keepdims=True)
    m_ref[...] = mnew_row


def _mlstm(p_ml3, conv_w, conv_b, norm_w, expand, ones_bd):
    batch, seq, _ = p_ml3.shape
    L = ML_CHUNK
    fixed = lambda b, j: (0, 0)
    return pl.pallas_call(
        _mlstm_kernel,
        grid=(batch, seq // L),
        in_specs=[pl.BlockSpec((1, L, ML_COLS_PAD), lambda b, j: (b, j, 0)),
                  pl.BlockSpec((1, SUBLANES, ML_COLS_PAD), lambda b, j: (b, jnp.maximum(j * (L // SUBLANES) - 1, 0), 0)),
                  pl.BlockSpec((CONV_WIDTH, 2 * ML_DIM), fixed), pl.BlockSpec((1, 2 * ML_DIM), fixed),
                  pl.BlockSpec((1, ML_DIM), fixed), pl.BlockSpec((LANES, 2 * ML_DIM), fixed),
                  pl.BlockSpec((ML_DIM, ML_DIM), fixed)],
        out_specs=pl.BlockSpec((1, L, ML_DIM), lambda b, j: (b, j, 0)),
        out_shape=jax.ShapeDtypeStruct((batch, seq, ML_DIM), F32),
        scratch_shapes=[pltpu.VMEM((ML_DIM, ML_DIM), F32), pltpu.VMEM((1, ML_DIM), F32),
                        pltpu.VMEM((1, ML_DIM), F32)],
        compiler_params=_cparams("parallel", "arbitrary"),
        name="mlstm",
    )(p_ml3, p_ml3, conv_w, conv_b, norm_w, expand, ones_bd)


def _outproj_kernel(rw_ref, sw_ref, ml_ref, h_ref, wrw_ref, wsw_ref, wml_ref, lnw_ref, lnb_ref,
                    rwh_ref, rwm_ref, rwl_ref, rb_ref, h1_o, h1t_o, idx_o, gate_o):
    mix = (_dot(rw_ref[...], wrw_ref[...]) + _dot(sw_ref[...], wsw_ref[...]) + _dot(ml_ref[...], wml_ref[...]))
    h1 = _layer_norm(DN_ALPHA * h_ref[...] + mix, lnw_ref[...], lnb_ref[...])
    h1_o[...] = h1
    _to_row_tiles(h1t_o, h1)
    xh, xm, xl = _split3(h1)
    wh, wm, wl = rwh_ref[...], rwm_ref[...], rwl_ref[...]
    dd = lambda a, b: jnp.dot(a, b, preferred_element_type=F32)
    logits = (dd(xh, wh) + (dd(xh, wm) + dd(xm, wh)) + (dd(xh, wl) + dd(xm, wm) + dd(xl, wh))) + rb_ref[...]
    lane = lax.broadcasted_iota(jnp.int32, logits.shape, 1).astype(F32)
    vals, idxs = [], []
    cur = logits
    for _ in range(TOP_K):
        mx = jnp.max(cur, axis=-1, keepdims=True)
        ix = jnp.min(jnp.where(cur == mx, lane, float(LANES)), axis=-1, keepdims=True)
        vals.append(mx)
        idxs.append(ix)
        cur = jnp.where(lane == ix, NEG_BIG * 2.0, cur)
    es = [jnp.exp(vv - vals[0]) for vv in vals]
    den = es[0] + es[1] + es[2] + es[3]
    idx_out = jnp.zeros_like(logits)
    gate_out = jnp.zeros_like(logits)
    for kk in range(TOP_K):
        idx_out = jnp.where(lane == float(kk), idxs[kk], idx_out)
        gate_out = jnp.where(lane == float(kk), es[kk] / den, gate_out)
    idx_o[...] = idx_out.astype(jnp.int32)
    gate_o[...] = gate_out


def _out_proj(rw_out, sw_out, ml_out, h2d, w_rw, w_sw, w_ml, ln_w, ln_b, rw_parts, rb):
    m = h2d.shape[0]
    tm = ROW_TILE
    row = lambda i: (i, 0)
    fixed = lambda i: (0, 0)
    full = lambda a: pl.BlockSpec(a.shape, fixed)
    return pl.pallas_call(
        _outproj_kernel,
        grid=(m // tm,),
        in_specs=[pl.BlockSpec((tm, RW_DIM), row), pl.BlockSpec((tm, SW_DIM), row), pl.BlockSpec((tm, ML_DIM), row),
                  pl.BlockSpec((tm, D_MODEL), row), full(w_rw), full(w_sw), full(w_ml), full(ln_w), full(ln_b),
                  full(rw_parts[0]), full(rw_parts[1]), full(rw_parts[2]), full(rb)],
        out_specs=[pl.BlockSpec((tm, D_MODEL), row), pl.BlockSpec((tm, SUBLANES, LANES), lambda i: (i, 0, 0)),
                   pl.BlockSpec((tm, LANES), row), pl.BlockSpec((tm, LANES), row)],
        out_shape=[jax.ShapeDtypeStruct((m, D_MODEL), F32), jax.ShapeDtypeStruct((m, SUBLANES, LANES), F32),
                   jax.ShapeDtypeStruct((m, LANES), jnp.int32), jax.ShapeDtypeStruct((m, LANES), F32)],
        compiler_params=_cparams("parallel"),
        name="out_proj_router",
    )(rw_out, sw_out, ml_out, h2d, w_rw, w_sw, w_ml, ln_w, ln_b, *rw_parts, rb)


def _expert_kernel(n_blocks, layer, be_ref, tokc_ref, tokn_ref, slot_ref, h_hbm, wup_ref, bg_ref, bl_ref,
                   wdn_ref, bd_ref, perm_ref, z_hbm, xbuf, ybuf, wg_s, wl_s, wd_s, gsem, ssem):
    del layer
    i = pl.program_id(0)
    s = i % 2
    bm = MOE_BLOCK

    def gather_row(tok, buf, r):
        return pltpu.make_async_copy(h_hbm.at[tok], xbuf.at[buf, r], gsem.at[buf])

    def scatter_row(dst, buf, r):
        return pltpu.make_async_copy(ybuf.at[buf, r], z_hbm.at[dst], ssem.at[buf])

    def wait_gather(buf):
        pltpu.make_async_copy(h_hbm.at[pl.ds(0, bm)], xbuf.at[buf], gsem.at[buf]).wait()

    def wait_scatter(buf):
        pltpu.make_async_copy(ybuf.at[buf], z_hbm.at[pl.ds(0, bm)], ssem.at[buf]).wait()

    @pl.when(i == 0)
    def _():
        ybuf[1] = jnp.zeros(ybuf.shape[1:], F32)

        def prime(r, c):
            gather_row(tokc_ref[r], 0, r).start()
            return c

        lax.fori_loop(0, bm, prime, 0, unroll=8)

    wait_gather(s)

    @pl.when(i > 0)
    def _():
        wait_scatter(s)

    blk = jnp.minimum(i, n_blocks - 1)
    e_now = be_ref[blk]
    e_prev = be_ref[jnp.maximum(blk - 1, 0)]

    @pl.when((i == 0) | (e_now != e_prev))
    def _():
        perm = perm_ref[...]
        for c in range(D_EXPERT // LANES):
            t = jnp.dot(wup_ref[0, 0, :, c * 2 * LANES:(c + 1) * 2 * LANES].astype(BF16), perm,
                        preferred_element_type=F32)
            wg_s[:, c * LANES:(c + 1) * LANES] = t[:, 0:LANES].astype(BF16)
            wl_s[:, c * LANES:(c + 1) * LANES] = t[:, LANES:2 * LANES].astype(BF16)
        wd_s[...] = wdn_ref[0, 0].astype(BF16)

    for r in range(bm):
        gather_row(tokn_ref[r], 1 - s, r).start(priority=0)
        scatter_row(slot_ref[r], 1 - s, r).start(priority=1)
    x = _from_row_tiles(xbuf.at[s]).astype(BF16)
    hg = jnp.dot(x, wg_s[...], preferred_element_type=F32) + bg_ref[0, 0]
    hl = jnp.dot(x, wl_s[...], preferred_element_type=F32) + bl_ref[0, 0]
    hg = jnp.minimum(hg, SWIGLU_LIMIT)
    hl = jnp.clip(hl, -SWIGLU_LIMIT, SWIGLU_LIMIT)
    act = hg * _sigmoid(SWIGLU_ALPHA * hg) * (hl + 1.0)
    _to_row_tiles(ybuf.at[s], jnp.dot(act.astype(BF16), wd_s[...], preferred_element_type=F32) + bd_ref[0, 0])

    @pl.when(i == n_blocks)
    def _():
        wait_gather(1 - s)
        wait_scatter(1 - s)


def _experts(layer, block_e, tok, slot_ext, h1, w_up, b_glu, b_lin, w_down, b_down, perm, z_rows):
    bm = MOE_BLOCK
    n_blocks = tok.shape[0] // bm
    last = n_blocks - 1
    by_e = lambda i, be: (layer, be[jnp.minimum(i, last)], 0, 0)
    smem = lambda f: pl.BlockSpec((bm,), f, memory_space=pltpu.SMEM)
    grid_spec = pltpu.PrefetchScalarGridSpec(
        num_scalar_prefetch=1,
        grid=(n_blocks + 1,),
        in_specs=[smem(lambda i, be: (jnp.minimum(i, last),)), smem(lambda i, be: (jnp.minimum(i + 1, last),)),
                  smem(lambda i, be: (i,)),
                  pl.BlockSpec(memory_space=pl.ANY),
                  pl.BlockSpec((1, 1, D_MODEL, 2 * D_EXPERT), by_e),
                  pl.BlockSpec((1, 1, 1, D_EXPERT), by_e), pl.BlockSpec((1, 1, 1, D_EXPERT), by_e),
                  pl.BlockSpec((1, 1, D_EXPERT, D_MODEL), by_e), pl.BlockSpec((1, 1, 1, D_MODEL), by_e),
                  pl.BlockSpec((2 * LANES, 2 * LANES), lambda i, be: (0, 0))],
        out_specs=pl.BlockSpec(memory_space=pl.ANY),
        scratch_shapes=[pltpu.VMEM((2, bm, SUBLANES, LANES), F32), pltpu.VMEM((2, bm, SUBLANES, LANES), F32),
                        pltpu.VMEM((D_MODEL, D_EXPERT), BF16), pltpu.VMEM((D_MODEL, D_EXPERT), BF16),
                        pltpu.VMEM((D_EXPERT, D_MODEL), BF16),
                        pltpu.SemaphoreType.DMA((2,)), pltpu.SemaphoreType.DMA((2,))],
    )
    return pl.pallas_call(
        functools.partial(_expert_kernel, n_blocks, layer),
        grid_spec=grid_spec,
        out_shape=jax.ShapeDtypeStruct((z_rows, SUBLANES, LANES), F32),
        compiler_params=pltpu.CompilerParams(dimension_semantics=("arbitrary",), vmem_limit_bytes=VMEM_LIMIT,
                                             disable_bounds_checks=True),
        name="experts",
    )(block_e, tok, tok, slot_ext, h1, w_up, b_glu, b_lin, w_down, b_down, perm)


def _combine_kernel(z0_ref, z1_ref, z2_ref, z3_ref, gate_ref, h_ref, lnw_ref, lnb_ref, o_ref):
    gates = gate_ref[...]
    y = gates[:, 0:1] * _from_row_tiles(z0_ref)
    for kk, z_ref in enumerate((z1_ref, z2_ref, z3_ref), start=1):
        y = y + gates[:, kk:kk + 1] * _from_row_tiles(z_ref)
    o_ref[...] = _layer_norm(DN_ALPHA * h_ref[...] + y, lnw_ref[...], lnb_ref[...])


def _combine(z, gates, h1, ln_w, ln_b):
    m = h1.shape[0]
    tm = ROW_TILE
    n_tiles = m // tm
    row = lambda i: (i, 0)
    fixed = lambda i: (0, 0)
    choice = lambda kk: pl.BlockSpec((tm, SUBLANES, LANES), lambda i: (kk * n_tiles + i, 0, 0))
    return pl.pallas_call(
        _combine_kernel,
        grid=(n_tiles,),
        in_specs=[choice(0), choice(1), choice(2), choice(3), pl.BlockSpec((tm, LANES), row),
                  pl.BlockSpec((tm, D_MODEL), row), pl.BlockSpec((1, D_MODEL), fixed), pl.BlockSpec((1, D_MODEL), fixed)],
        out_specs=pl.BlockSpec((tm, D_MODEL), row),
        out_shape=jax.ShapeDtypeStruct((m, D_MODEL), F32),
        compiler_params=_cparams("parallel"),
        name="combine_ln",
    )(z, z, z, z, gates, h1, ln_w, ln_b)


def _routing_tables(top_idx):
    m = top_idx.shape[0]
    n_assign = m * TOP_K
    bm = MOE_BLOCK
    n_blocks = -(-(n_assign + N_EXPERTS * (bm - 1)) // bm)
    p_rows = n_blocks * bm
    e_flat = top_idx.reshape(n_assign)
    order = jnp.argsort(e_flat, stable=True).astype(jnp.int32)
    experts = jnp.arange(N_EXPERTS, dtype=jnp.int32)
    counts = jnp.sum((e_flat[:, None] == experts[None, :]).astype(jnp.int32), axis=0)
    padded = (counts + bm - 1) // bm * bm
    start = jnp.cumsum(counts) - counts
    pend = jnp.cumsum(padded)
    pstart = pend - padded
    block_start = jnp.arange(n_blocks, dtype=jnp.int32) * bm
    block_e = jnp.minimum(jnp.sum((pend[None, :] <= block_start[:, None]).astype(jnp.int32), axis=1), N_EXPERTS - 1)
    per_row = lambda per_expert: jnp.repeat(per_expert[block_e], bm)
    row_pstart, row_start, row_count = per_row(pstart), per_row(start), per_row(counts)
    pos = jnp.arange(p_rows, dtype=jnp.int32) - row_pstart
    is_pad = pos >= row_count
    src_assign = order[jnp.clip(row_start + pos, 0, n_assign - 1)]
    tok = jnp.where(is_pad, 0, src_assign // TOP_K)
    pad_rank = (row_pstart - row_start) + (pos - row_count)
    slot = jnp.where(is_pad, n_assign + pad_rank, (src_assign % TOP_K) * m + src_assign // TOP_K)
    slot_ext = jnp.concatenate([p_rows + jnp.arange(bm, dtype=jnp.int32), slot])
    return tok, slot_ext, block_e, p_rows + bm


def _block_ones(n):
    g = np.arange(n) // HEAD_DIM
    return jnp.asarray(g[:, None] == g[None, :], BF16)


def _sw_head_order():
    group = SW_HEADS // SW_KV_HEADS
    heads = [kv * group + g for g in range(group) for kv in range(SW_KV_HEADS)]
    return np.concatenate([np.arange(h * HEAD_DIM, (h + 1) * HEAD_DIM) for h in heads])


def _rope_tables(seq):
    half = HEAD_DIM // 2
    inv = ROPE_THETA ** (-jnp.arange(half, dtype=F32) / half)
    ang = jnp.arange(seq, dtype=F32)[:, None] * inv[None, :]
    cos, sin = jnp.cos(ang), jnp.sin(ang)
    reps = LANES // HEAD_DIM
    cos_t = jnp.tile(jnp.concatenate([cos, cos], axis=-1), (1, reps))
    sin_t = jnp.tile(jnp.concatenate([-sin, sin], axis=-1), (1, reps))
    return cos_t, sin_t


def kernel(x, w_in, b_in, rw_shift_mu, rw_w_up, rw_w0, rw_a_up, rw_a0, rw_g_up, rw_k_k, rw_k_a, rw_r_k, rw_ln_w, rw_ln_b, sw_sinks, ml_conv_w, ml_conv_b, ml_norm_w, w_out, ln1_w, ln1_b, router_w, router_b, exp_w_up, exp_b_up, exp_w_down, exp_b_down, ln2_w, ln2_b):
    batch, seq, d_model = x.shape
    assert d_model == D_MODEL and seq % ROW_TILE == 0 and seq % ML_CHUNK == 0 and seq % WINDOW == 0
    m = batch * seq
    depth = w_in.shape[0]
    row2 = lambda t: t.reshape(1, -1)

    sw_perm = _sw_head_order()
    in_cols = np.concatenate([np.arange(RW_COLS), RW_COLS + sw_perm, np.arange(RW_COLS + SW_DIM, C_IN)])
    ones_rw = _block_ones(RW_DIM)
    ones_ml = _block_ones(ML_DIM)
    cos_t, sin_t = _rope_tables(seq)
    src = np.arange(LANES)[:, None]
    dst = np.arange(2 * ML_DIM)[None, :]
    expand = jnp.asarray(src == (dst // ML_DIM) * ML_HEADS + (dst % ML_DIM) // HEAD_DIM, BF16)
    pj = np.arange(2 * LANES)[:, None]
    pc = np.arange(2 * LANES)[None, :]
    perm = jnp.asarray(pj == 2 * (pc % LANES) + pc // LANES, BF16)
    n_layers = exp_b_up.shape[0]
    b_glu = exp_b_up[:, :, 0::2].reshape(n_layers, N_EXPERTS, 1, D_EXPERT)
    b_lin = exp_b_up[:, :, 1::2].reshape(n_layers, N_EXPERTS, 1, D_EXPERT)
    b_dn = exp_b_down.reshape(n_layers, N_EXPERTS, 1, D_MODEL)

    h = x.reshape(m, D_MODEL)
    for l in range(depth):
        w_l = jnp.pad(w_in[l][:, in_cols], ((0, 0), (0, C_IN_PAD - C_IN))).astype(BF16)
        b_l = jnp.pad(b_in[l][in_cols], (0, C_IN_PAD - C_IN)).reshape(1, C_IN_PAD)
        zero = jnp.zeros((RW_LORA_W, RW_DIM), F32)
        wcomb = jnp.concatenate([jnp.concatenate([rw_w_up[l], zero], axis=1),
                                 jnp.concatenate([zero, rw_a_up[l]], axis=1)], axis=0).astype(BF16)
        w_o = w_out[l]
        w_o_rw = w_o[0:RW_DIM].astype(BF16)
        w_o_sw = w_o[RW_DIM + sw_perm].astype(BF16)
        w_o_ml = w_o[RW_DIM + SW_DIM:].astype(BF16)
        rw_pad = jnp.pad(router_w[l], ((0, 0), (0, LANES - N_EXPERTS)))
        rw_hi = rw_pad.astype(BF16)
        rw_r1 = rw_pad - rw_hi.astype(F32)
        rw_mid = rw_r1.astype(BF16)
        rw_lo = (rw_r1 - rw_mid.astype(F32)).astype(BF16)
        rb_pad = jnp.pad(router_b[l], (0, LANES - N_EXPERTS), constant_values=NEG_BIG).reshape(1, LANES)

        p_rw, p_sw, p_ml = _in_proj(h, w_l, b_l)
        r, lw, k, v, a, b, g = _rw_prep(p_rw, seq, row2(rw_shift_mu[l]), wcomb, rw_g_up[l].astype(BF16),
                                        row2(rw_w0[l]), row2(rw_a0[l]), row2(rw_k_k[l]), row2(rw_k_a[l]), ones_rw)
        m_mat, yh, g_mat, y0 = _rw_chunks(r, lw, k, v, a, b)
        y = _rw_scan(batch, seq, m_mat, yh, g_mat, y0).reshape(m, RW_DIM)
        rw_out = _rw_post(y, r, k, v, g, row2(rw_r_k[l]), row2(rw_ln_w[l]), row2(rw_ln_b[l]), ones_rw)
        sw_out = _swa(p_sw.reshape(batch, seq, SW_COLS), sw_sinks[l], cos_t, sin_t).reshape(m, SW_DIM)
        ml_out = _mlstm(p_ml.reshape(batch, seq, ML_COLS_PAD), ml_conv_w[l], row2(ml_conv_b[l]),
                        row2(ml_norm_w[l]), expand, ones_ml).reshape(m, ML_DIM)
        h1, h1_tiles, idx_pad, gates = _out_proj(rw_out, sw_out, ml_out, h, w_o_rw, w_o_sw, w_o_ml, row2(ln1_w[l]),
                                                 row2(ln1_b[l]), (rw_hi, rw_mid, rw_lo), rb_pad)

        tok, slot_ext, block_e, z_rows = _routing_tables(idx_pad[:, 0:TOP_K])
        z = _experts(l, block_e, tok, slot_ext, h1_tiles, exp_w_up, b_glu, b_lin, exp_w_down, b_dn, perm, z_rows)
        h = _combine(z, gates, h1, row2(ln2_w[l]), row2(ln2_b[l]))
    return h.reshape(batch, seq, D_MODEL)
```

```python
import functools

import numpy as np
import jax
import jax.numpy as jnp
from jax import lax
from jax.experimental import pallas as pl
from jax.experimental.pallas import tpu as pltpu

F32 = jnp.float32
BF16 = jnp.bfloat16

D_MODEL = 1024
HEAD_DIM = 64
RW_HEADS = 6
SW_HEADS = 6
SW_KV_HEADS = 2
ML_HEADS = 4
RW_DIM = RW_HEADS * HEAD_DIM
SW_DIM = SW_HEADS * HEAD_DIM
SW_KV_DIM = SW_KV_HEADS * HEAD_DIM
ML_DIM = ML_HEADS * HEAD_DIM
RW_LORA_W = 64
RW_LORA_A = 64
RW_LORA_G = 128
RW_COLS = 3 * RW_DIM + RW_LORA_W + RW_LORA_A + RW_LORA_G
SW_COLS = SW_DIM + 2 * SW_KV_DIM
ML_COLS = 4 * ML_DIM + 2 * ML_HEADS
C_IN = RW_COLS + SW_COLS + ML_COLS
WINDOW = 128
ROPE_THETA = 10000.0
CONV_WIDTH = 4
N_EXPERTS = 32
TOP_K = 4
D_EXPERT = 1024
SWIGLU_ALPHA = 1.702
SWIGLU_LIMIT = 7.0
LN_EPS = 1e-5
RW_GN_EPS = 64e-5
ML_GN_EPS = 1e-6
DEPTH = 4
DN_ALPHA = (2 * DEPTH) ** 0.25

LANES = 128
SUBLANES = 8
ML_COLS_PAD = 1152
C_IN_PAD = RW_COLS + SW_COLS + ML_COLS_PAD
ROW_TILE = 512
RW_CHUNK = 64
RW_PAIR = 2 * HEAD_DIM
RW_NPAIR = RW_HEADS // 2
ML_CHUNK = 256
MOE_BLOCK = 512
NEG_BIG = -1e30
VMEM_LIMIT = 56 * 1024 * 1024


def _cparams(*sem):
    return pltpu.CompilerParams(dimension_semantics=sem, vmem_limit_bytes=VMEM_LIMIT)


def _dot(a, b):
    return jnp.dot(a.astype(BF16), b.astype(BF16), preferred_element_type=F32)


def _dot_nt(a, b):
    return lax.dot_general(a.astype(BF16), b.astype(BF16), (((1,), (1,)), ((), ())),
                           preferred_element_type=F32)


def _split3(x):
    hi = x.astype(BF16)
    r1 = x - hi.astype(F32)
    mid = r1.astype(BF16)
    lo = (r1 - mid.astype(F32)).astype(BF16)
    return hi, mid, lo


def _dot_xl(x, w):
    hi, mid, lo = _split3(x)
    return (jnp.dot(hi, w, preferred_element_type=F32) + jnp.dot(mid, w, preferred_element_type=F32)
            + jnp.dot(lo, w, preferred_element_type=F32))


def _dot_lx(w, x):
    hi, mid, lo = _split3(x)
    return (jnp.dot(w, hi, preferred_element_type=F32) + jnp.dot(w, mid, preferred_element_type=F32)
            + jnp.dot(w, lo, preferred_element_type=F32))


def _softplus(z):
    return jnp.maximum(z, 0.0) + jnp.log(1.0 + jnp.exp(-jnp.abs(z)))


def _sigmoid(z):
    return 1.0 / (1.0 + jnp.exp(-z))


def _shift_rows(x, prev8, j):
    cat = jnp.concatenate([prev8, x], axis=0)
    return pltpu.roll(cat, j, 0)[SUBLANES:, :]


def _layer_norm(x, w, b):
    mu = jnp.mean(x, axis=-1, keepdims=True)
    d = x - mu
    var = jnp.mean(d * d, axis=-1, keepdims=True)
    return d * lax.rsqrt(var + LN_EPS) * w + b


def _pack_bf16_pairs(x):
    c = x.shape[1] // 2
    bits = pltpu.bitcast(x.astype(BF16).astype(F32), jnp.uint32)
    return lax.shift_right_logical(bits[:, 0:c], jnp.uint32(16)) | (bits[:, c:2 * c] & jnp.uint32(0xFFFF0000))


def _unpack_bf16_pairs(w):
    lo = pltpu.bitcast(lax.shift_left(w, jnp.uint32(16)), F32)
    hi = pltpu.bitcast(w & jnp.uint32(0xFFFF0000), F32)
    return jnp.concatenate([lo, hi], axis=1).astype(BF16)


def _head_norm(y, ones_bd, eps):
    inv = 1.0 / HEAD_DIM
    mu = _dot_xl(y, ones_bd) * inv
    d = y - mu
    var = _dot_xl(d * d, ones_bd) * inv
    return d * lax.rsqrt(var + eps)


def _inproj_kernel(x_ref, w_ref, b_ref, rw_ref, sw_ref, ml_ref):
    x = x_ref[...].astype(BF16)
    c1, c2 = RW_COLS, RW_COLS + SW_COLS
    rw_ref[...] = jnp.dot(x, w_ref[:, 0:c1], preferred_element_type=F32) + b_ref[:, 0:c1]
    sw_ref[...] = jnp.dot(x, w_ref[:, c1:c2], preferred_element_type=F32) + b_ref[:, c1:c2]
    ml_ref[...] = jnp.dot(x, w_ref[:, c2:C_IN_PAD], preferred_element_type=F32) + b_ref[:, c2:C_IN_PAD]


def _in_proj(h2d, w, b):
    m = h2d.shape[0]
    tm = ROW_TILE
    row = lambda i: (i, 0)
    fixed = lambda i: (0, 0)
    return pl.pallas_call(
        _inproj_kernel,
        grid=(m // tm,),
        in_specs=[pl.BlockSpec((tm, D_MODEL), row), pl.BlockSpec((D_MODEL, C_IN_PAD), fixed),
                  pl.BlockSpec((1, C_IN_PAD), fixed)],
        out_specs=[pl.BlockSpec((tm, RW_COLS), row), pl.BlockSpec((tm, SW_COLS), row),
                   pl.BlockSpec((tm, ML_COLS_PAD), row)],
        out_shape=[jax.ShapeDtypeStruct((m, RW_COLS), F32), jax.ShapeDtypeStruct((m, SW_COLS), F32),
                   jax.ShapeDtypeStruct((m, ML_COLS_PAD), F32)],
        compiler_params=_cparams("parallel"),
        name="in_proj",
    )(h2d, w, b)


def _rw_prep_kernel(tiles_per_seq, p_ref, pprev_ref, mu_ref, wcomb_ref, gup_ref, w0_ref, a0_ref, kk_ref,
                    ka_ref, ones_ref, r_o, lw_o, k_o, v_o, a_o, b_o, g_o):
    i = pl.program_id(0)
    x = p_ref[...]
    first = (i % tiles_per_seq) == 0
    prev8 = jnp.where(first, 0.0, pprev_ref[...])
    xs = x + (_shift_rows(x, prev8, 1) - x) * mu_ref[...]
    d = RW_DIM
    r = xs[:, 0:d]
    k = xs[:, d:2 * d]
    v = xs[:, 2 * d:3 * d]
    slab = xs[:, 3 * d:3 * d + LANES]
    lane = lax.broadcasted_iota(jnp.int32, slab.shape, 1)
    slab = jnp.where(lane < RW_LORA_W, jnp.tanh(slab), slab)
    wa = _dot(slab, wcomb_ref[...])
    w_pre = w0_ref[...] + wa[:, 0:d]
    a_pre = a0_ref[...] + wa[:, d:2 * d]
    w_log = -_softplus(-w_pre) - 0.5
    log_decay = -jnp.exp(w_log)
    a_sig = _sigmoid(a_pre)
    g = _dot(_sigmoid(xs[:, 3 * d + LANES:3 * d + 2 * LANES]), gup_ref[...])
    kk = k * kk_ref[...]
    ss = _dot_xl(kk * kk, ones_ref[...])
    kk = kk / jnp.maximum(jnp.sqrt(ss), 1e-12)
    r_o[...] = r
    lw_o[...] = log_decay
    k_o[...] = k * (1.0 + (a_sig - 1.0) * ka_ref[...])
    v_o[...] = v
    a_o[...] = -kk
    b_o[...] = kk * a_sig
    g_o[...] = g


def _rw_prep(p_rw, seq, mu, wcomb, gup, w0, a0, k_k, k_a, ones_bd):
    m = p_rw.shape[0]
    tm = ROW_TILE
    row = lambda i: (i, 0)
    fixed = lambda i: (0, 0)
    prev = lambda i: (jnp.maximum(i * (tm // SUBLANES) - 1, 0), 0)
    vec = pl.BlockSpec((1, RW_DIM), fixed)
    out = jax.ShapeDtypeStruct((m, RW_DIM), F32)
    return pl.pallas_call(
        functools.partial(_rw_prep_kernel, seq // tm),
        grid=(m // tm,),
        in_specs=[pl.BlockSpec((tm, RW_COLS), row), pl.BlockSpec((SUBLANES, RW_COLS), prev),
                  pl.BlockSpec((1, RW_COLS), fixed), pl.BlockSpec((LANES, 2 * RW_DIM), fixed),
                  pl.BlockSpec((RW_LORA_G, RW_DIM), fixed), vec, vec, vec, vec,
                  pl.BlockSpec((RW_DIM, RW_DIM), fixed)],
        out_specs=[pl.BlockSpec((tm, RW_DIM), row)] * 7,
        out_shape=[out] * 7,
        compiler_params=_cparams("parallel"),
        name="rw_prep",
    )(p_rw, p_rw, mu, wcomb, gup, w0, a0, k_k, k_a, ones_bd)


def _rw_chunk_kernel(n_chunk, r_ref, lw_ref, k_ref, v_ref, a_ref, b_ref, m_o, yh_o, g_o, y0_o):
    L = RW_CHUNK
    ti = lax.broadcasted_iota(jnp.int32, (L, L), 0)
    si = lax.broadcasted_iota(jnp.int32, (L, L), 1)
    tri = jnp.where(ti >= si, 1.0, 0.0).astype(BF16)
    n2 = 2 * L
    ri = lax.broadcasted_iota(jnp.int32, (n2, n2), 0)
    ci = lax.broadcasted_iota(jnp.int32, (n2, n2), 1)
    rr = jnp.where(ri >= L, ri - L, ri)
    cc = jnp.where(ci >= L, ci - L, ci)
    strict = rr > cc
    incl = rr >= cc
    eye = ri == ci
    eye_f = jnp.where(eye, 1.0, 0.0)
    lane = lax.broadcasted_iota(jnp.int32, (L, RW_PAIR), 1)
    head0 = lane < HEAD_DIM

    def bmm(x, y):
        return jnp.einsum('bmk,bkn->bmn', x.astype(BF16), y.astype(BF16), preferred_element_type=F32)

    def bmm_nt(x, y):
        return jnp.einsum('bmk,bnk->bmn', x.astype(BF16), y.astype(BF16), preferred_element_type=F32)

    def bmm_tn(x, y):
        return bmm(jnp.swapaxes(x, 1, 2), y)

    def stack(x):
        return jnp.concatenate([jnp.where(head0, x, 0.0), jnp.where(head0, 0.0, x)], axis=1)

    def chunked(ref, cols):
        return ref[:, cols].reshape(n_chunk, L, RW_PAIR)

    tri_b = jnp.broadcast_to(tri, (n_chunk, L, L))
    for p in range(RW_NPAIR):
        cols = slice(p * RW_PAIR, (p + 1) * RW_PAIR)
        lw = chunked(lw_ref, cols)
        hi, mid, lo = _split3(lw)
        tsum = lambda part: jnp.einsum('bts,bsn->btn', tri_b, part, preferred_element_type=F32)
        lc = tsum(hi) + tsum(mid) + tsum(lo)
        lcl = lc[:, L - 1:L, :]
        gam = jnp.exp(lc)
        gam_prev = jnp.exp(lc - lw)
        inv = jnp.exp(-lc)
        to_end = jnp.exp(lcl - lc)
        r = chunked(r_ref, cols)
        k = chunked(k_ref, cols)
        v = chunked(v_ref, cols)
        a = chunked(a_ref, cols)
        b = chunked(b_ref, cols)
        x1 = stack(a * gam_prev)
        r1 = stack(r * gam)
        x2 = stack(b * inv)
        k2 = stack(k * inv)
        vs = stack(v)
        bs = stack(b * to_end)
        ks = stack(k * to_end)
        aa = bmm_nt(jnp.concatenate([x1, r1], axis=1), jnp.concatenate([x2, k2], axis=1))
        a_ab = jnp.where(strict, aa[:, 0:n2, 0:n2], 0.0)
        a_ak = jnp.where(strict, aa[:, 0:n2, n2:2 * n2], 0.0)
        a_rb = jnp.where(incl, aa[:, n2:2 * n2, 0:n2], 0.0)
        a_rk = jnp.where(incl, aa[:, n2:2 * n2, n2:2 * n2], 0.0)
        apow = a_ab
        t_inv = eye_f + a_ab
        for _ in range(5):
            apow = bmm(apow, apow)
            t_inv = t_inv + bmm(t_inv, apow)
        akv = bmm(a_ak, vs)
        pw = bmm(t_inv, jnp.concatenate([x1, akv], axis=2))
        pmat = pw[:, :, 0:RW_PAIR]
        wmat = pw[:, :, RW_PAIR:2 * RW_PAIR]
        gl_row = jnp.exp(lcl)
        m_mat = bmm_tn(bs, pmat) + jnp.where(eye, gl_row, 0.0)
        g_mat = bmm_tn(jnp.concatenate([bs, ks], axis=1), jnp.concatenate([wmat, vs], axis=1))
        yy = bmm(a_rb, pw)
        yh = r1 + yy[:, :, 0:RW_PAIR]
        y0 = yy[:, :, RW_PAIR:2 * RW_PAIR] + bmm(a_rk, vs)
        m_o[:, p] = m_mat.astype(BF16)
        yh_o[:, p] = yh.astype(BF16)
        g_o[:, p] = g_mat
        y0_o[:, p] = y0


def _rw_chunks(r, lw, k, v, a, b):
    m = r.shape[0]
    tm = ROW_TILE
    nc = tm // RW_CHUNK
    row = lambda i: (i, 0)
    blk = lambda i: (i, 0, 0, 0)
    n2 = 2 * RW_CHUNK
    n_tot = m // RW_CHUNK
    ospec = pl.BlockSpec((nc, RW_NPAIR, n2, RW_PAIR), blk)
    return pl.pallas_call(
        functools.partial(_rw_chunk_kernel, nc),
        grid=(m // tm,),
        in_specs=[pl.BlockSpec((tm, RW_DIM), row)] * 6,
        out_specs=[ospec] * 4,
        out_shape=[jax.ShapeDtypeStruct((n_tot, RW_NPAIR, n2, RW_PAIR), BF16),
                   jax.ShapeDtypeStruct((n_tot, RW_NPAIR, n2, RW_PAIR), BF16),
                   jax.ShapeDtypeStruct((n_tot, RW_NPAIR, n2, RW_PAIR), F32),
                   jax.ShapeDtypeStruct((n_tot, RW_NPAIR, n2, RW_PAIR), F32)],
        compiler_params=_cparams("parallel"),
        name="rw_chunks",
    )(r, lw, k, v, a, b)


def _rw_scan_kernel(batch, m_ref, yh_ref, g_ref, y0_ref, y_o, h_ref):
    c = pl.program_id(0)

    @pl.when(c == 0)
    def _():
        h_ref[...] = jnp.zeros_like(h_ref)

    L = RW_CHUNK
    for bi in range(batch):
        for p in range(RW_NPAIR):
            hb = h_ref[bi, p].astype(BF16)
            yrows = jnp.dot(yh_ref[bi, 0, p], hb, preferred_element_type=F32) + y0_ref[bi, 0, p]
            y_o[bi, :, p * RW_PAIR:(p + 1) * RW_PAIR] = yrows[0:L, :] + yrows[L:2 * L, :]
            h_ref[bi, p] = jnp.dot(m_ref[bi, 0, p], hb, preferred_element_type=F32) + g_ref[bi, 0, p]


def _rw_scan(batch, seq, m_mat, yh, g_mat, y0):
    nc = seq // RW_CHUNK
    n2 = 2 * RW_CHUNK
    shp = (batch, nc, RW_NPAIR, n2, RW_PAIR)
    args = [t.reshape(shp) for t in (m_mat, yh, g_mat, y0)]
    ispec = pl.BlockSpec((batch, 1, RW_NPAIR, n2, RW_PAIR), lambda c: (0, c, 0, 0, 0))
    return pl.pallas_call(
        functools.partial(_rw_scan_kernel, batch),
        grid=(nc,),
        in_specs=[ispec] * 4,
        out_specs=pl.BlockSpec((batch, RW_CHUNK, RW_DIM), lambda c: (0, c, 0)),
        out_shape=jax.ShapeDtypeStruct((batch, seq, RW_DIM), F32),
        scratch_shapes=[pltpu.VMEM((batch, RW_NPAIR, RW_PAIR, RW_PAIR), F32)],
        compiler_params=_cparams("arbitrary"),
        name="rw_scan",
    )(*args)


def _rw_post_kernel(y_ref, r_ref, k_ref, v_ref, g_ref, rk_ref, lnw_ref, lnb_ref, ones_ref, o_ref):
    ones_bd = ones_ref[...]
    y = _head_norm(y_ref[...], ones_bd, RW_GN_EPS) * lnw_ref[...] + lnb_ref[...]
    bonus = _dot_xl(r_ref[...] * k_ref[...] * rk_ref[...], ones_bd) * v_ref[...]
    o_ref[...] = (y + bonus) * g_ref[...]


def _rw_post(y, r, k, v, g, r_k, ln_w, ln_b, ones_bd):
    m = y.shape[0]
    tm = ROW_TILE
    row = lambda i: (i, 0)
    fixed = lambda i: (0, 0)
    vec = pl.BlockSpec((1, RW_DIM), fixed)
    return pl.pallas_call(
        _rw_post_kernel,
        grid=(m // tm,),
        in_specs=[pl.BlockSpec((tm, RW_DIM), row)] * 5 + [vec, vec, vec, pl.BlockSpec((RW_DIM, RW_DIM), fixed)],
        out_specs=pl.BlockSpec((tm, RW_DIM), row),
        out_shape=jax.ShapeDtypeStruct((m, RW_DIM), F32),
        compiler_params=_cparams("parallel"),
        name="rw_post",
    )(y, r, k, v, g, r_k, ln_w, ln_b, ones_bd)


def _rope(x, cos, sin_signed):
    lane = lax.broadcasted_iota(jnp.int32, x.shape, 1)
    half = HEAD_DIM // 2
    first_half = (lane % HEAD_DIM) < half
    rot = jnp.where(first_half, pltpu.roll(x, LANES - half, 1), pltpu.roll(x, half, 1))
    return x * cos + rot * sin_signed


def _swa_kernel(sink_ref, q_ref, kc_ref, vc_ref, kp_ref, vp_ref, cosc_ref, sinc_ref, cosp_ref, sinp_ref, o_ref):
    j = pl.program_id(1)
    W = WINDOW
    cos_c, sin_c = cosc_ref[...], sinc_ref[...]
    k_cat = jnp.concatenate([_rope(kp_ref[0], cosp_ref[...], sinp_ref[...]), _rope(kc_ref[0], cos_c, sin_c)], axis=0)
    v_cat = jnp.concatenate([vp_ref[0], vc_ref[0]], axis=0).astype(BF16)
    k_cat = k_cat.astype(BF16)
    qi = lax.broadcasted_iota(jnp.int32, (W, 2 * W), 0)
    kj = lax.broadcasted_iota(jnp.int32, (W, 2 * W), 1)
    lo = jnp.where(j > 0, qi, jnp.maximum(qi, W - 1))
    bias = jnp.where((kj > lo) & (kj <= qi + W), 0.0, NEG_BIG)
    lane = lax.broadcasted_iota(jnp.int32, (W, LANES), 1)
    kv0 = lane < HEAD_DIM
    scale = HEAD_DIM ** -0.5
    group = SW_HEADS // SW_KV_HEADS
    for g in range(group):
        q = _rope(q_ref[0, :, g * LANES:(g + 1) * LANES], cos_c, sin_c) * scale
        outs = []
        for kv in range(SW_KV_HEADS):
            qm = jnp.where(kv0, q, 0.0) if kv == 0 else jnp.where(kv0, 0.0, q)
            s = _dot_nt(qm, k_cat) + bias
            sink = sink_ref[kv * group + g]
            mx = jnp.maximum(jnp.max(s, axis=-1, keepdims=True), sink)
            pr = jnp.exp(s - mx)
            den = jnp.sum(pr, axis=-1, keepdims=True) + jnp.exp(sink - mx)
            outs.append(jnp.dot(pr.astype(BF16), v_cat, preferred_element_type=F32) / den)
        o_ref[0, :, g * LANES:(g + 1) * LANES] = jnp.where(kv0, outs[0], outs[1])


def _swa(p_sw3, sinks, cos_t, sin_t):
    batch, seq, _ = p_sw3.shape
    W = WINDOW
    nb = seq // W
    kcol = SW_DIM // LANES
    cur = lambda c: (lambda b, j: (b, j, c))
    prv = lambda c: (lambda b, j: (b, jnp.maximum(j - 1, 0), c))
    tab_c = pl.BlockSpec((W, LANES), lambda b, j: (j, 0))
    tab_p = pl.BlockSpec((W, LANES), lambda b, j: (jnp.maximum(j - 1, 0), 0))
    kv_blk = lambda f: pl.BlockSpec((1, W, LANES), f)
    return pl.pallas_call(
        _swa_kernel,
        grid=(batch, nb),
        in_specs=[pl.BlockSpec(memory_space=pltpu.SMEM),
                  pl.BlockSpec((1, W, SW_DIM), cur(0)),
                  kv_blk(cur(kcol)), kv_blk(cur(kcol + 1)), kv_blk(prv(kcol)), kv_blk(prv(kcol + 1)),
                  tab_c, tab_c, tab_p, tab_p],
        out_specs=pl.BlockSpec((1, W, SW_DIM), cur(0)),
        out_shape=jax.ShapeDtypeStruct((batch, seq, SW_DIM), F32),
        compiler_params=_cparams("parallel", "parallel"),
        name="swa",
    )(sinks, p_sw3, p_sw3, p_sw3, p_sw3, p_sw3, cos_t, sin_t, cos_t, sin_t)


def _mlstm_kernel(p_ref, pprev_ref, cw_ref, cb_ref, nw_ref, expand_ref, ones_ref, o_ref, c_ref, nv_ref, m_ref):
    j = pl.program_id(1)
    L = ML_CHUNK
    d = ML_DIM

    @pl.when(j == 0)
    def _():
        c_ref[...] = jnp.zeros_like(c_ref)
        nv_ref[...] = jnp.zeros_like(nv_ref)
        m_ref[...] = jnp.zeros_like(m_ref)

    x = p_ref[0]
    qk_pre = x[:, 0:2 * d]
    prev8 = jnp.where(j == 0, 0.0, pprev_ref[0][:, 0:2 * d])
    conv = cb_ref[...] + cw_ref[CONV_WIDTH - 1:CONV_WIDTH, :] * qk_pre
    for s in range(1, CONV_WIDTH):
        conv = conv + cw_ref[CONV_WIDTH - 1 - s:CONV_WIDTH - s, :] * _shift_rows(qk_pre, prev8, s)
    qk = conv * _sigmoid(conv)
    q = qk[:, 0:d]
    k = qk[:, d:2 * d] * (HEAD_DIM ** -0.5)
    v = x[:, 2 * d:3 * d]
    o_gate = x[:, 3 * d:4 * d]
    gates = _dot_xl(x[:, 4 * d:4 * d + LANES], expand_ref[...])
    i_full = gates[:, 0:d]
    f_full = gates[:, d:2 * d]
    lf_full = jnp.minimum(f_full, 0.0) - jnp.log(1.0 + jnp.exp(-jnp.abs(f_full)))
    ti = lax.broadcasted_iota(jnp.int32, (L, L), 0)
    si = lax.broadcasted_iota(jnp.int32, (L, L), 1)
    causal = ti >= si
    tri = jnp.where(causal, 1.0, 0.0).astype(BF16)
    bc_full = _dot_lx(tri, lf_full)
    u_t = (i_full - bc_full).T
    lane = lax.broadcasted_iota(jnp.int32, (L, d), 1)
    lane_row = lax.broadcasted_iota(jnp.int32, (1, d), 1)
    q_b = q.astype(BF16)
    k_b = k.astype(BF16)
    m_prev_row = m_ref[...]
    nv_row = nv_ref[...]
    qn = q * nv_row
    num = jnp.zeros((L, d), F32)
    inter_full = jnp.zeros((L, d), F32)
    den_full = jnp.zeros((L, d), F32)
    mnew_row = jnp.zeros((1, d), F32)
    for h in range(ML_HEADS):
        c0 = h * HEAD_DIM
        in_head = (lane >= c0) & (lane < c0 + HEAD_DIM)
        in_head_row = (lane_row >= c0) & (lane_row < c0 + HEAD_DIM)
        bc_col = bc_full[:, c0:c0 + 1]
        dmat = jnp.where(causal, bc_col + u_t[c0:c0 + 1, :], NEG_BIG)
        m_inter = bc_col + m_prev_row[:, c0:c0 + 1]
        m_t = jnp.maximum(m_inter, jnp.max(dmat, axis=-1, keepdims=True))
        inter = jnp.exp(m_inter - m_t)
        e = jnp.exp(dmat - m_t)
        sm = _dot_nt(jnp.where(in_head, q, 0.0), k_b) * e
        num = num + _dot(sm, jnp.where(in_head, v, 0.0))
        nq = jnp.sum(sm, axis=-1, keepdims=True) + inter * jnp.sum(jnp.where(in_head, qn, 0.0), axis=-1, keepdims=True)
        den = jnp.maximum(jnp.abs(nq), jnp.exp(-m_t))
        inter_full = jnp.where(in_head, inter, inter_full)
        den_full = jnp.where(in_head, den, den_full)
        mnew_row = jnp.where(in_head_row, m_t[L - 1:L, :], mnew_row)
    c_mat = c_ref[...]
    num = num + inter_full * jnp.dot(q_b, c_mat.astype(BF16), preferred_element_type=F32)
    hout = num / den_full
    hn = _head_norm(hout, ones_ref[...], ML_GN_EPS) * nw_ref[...]
    o_ref[0] = _sigmoid(o_gate) * hn
    bcl_row = bc_full[L - 1:L, :]
    wst = jnp.exp(bcl_row - bc_full + i_full - mnew_row)
    dec_row = jnp.exp(bcl_row + m_prev_row - mnew_row)
    kw = k * wst
    ri = lax.broadcasted_iota(jnp.int32, (d, d), 0) // HEAD_DIM
    ci = lax.broadcasted_iota(jnp.int32, (d, d), 1) // HEAD_DIM
    c_ref[...] = dec_row * c_mat + jnp.where(ri == ci, _dot(kw.T, v), 0.0)
    nv_ref[...] = dec_row * nv_row + jnp.sum(kw, axis=0, keepdims=True)
    m_ref[...] = mnew_row


def _mlstm(p_ml3, conv_w, conv_b, norm_w, expand, ones_bd):
    batch, seq, _ = p_ml3.shape
    L = ML_CHUNK
    fixed = lambda b, j: (0, 0)
    return pl.pallas_call(
        _mlstm_kernel,
        grid=(batch, seq // L),
        in_specs=[pl.BlockSpec((1, L, ML_COLS_PAD), lambda b, j: (b, j, 0)),
                  pl.BlockSpec((1, SUBLANES, ML_COLS_PAD), lambda b, j: (b, jnp.maximum(j * (L // SUBLANES) - 1, 0), 0)),
                  pl.BlockSpec((CONV_WIDTH, 2 * ML_DIM), fixed), pl.BlockSpec((1, 2 * ML_DIM), fixed),
                  pl.BlockSpec((1, ML_DIM), fixed), pl.BlockSpec((LANES, 2 * ML_DIM), fixed),
                  pl.BlockSpec((ML_DIM, ML_DIM), fixed)],
        out_specs=pl.BlockSpec((1, L, ML_DIM), lambda b, j: (b, j, 0)),
        out_shape=jax.ShapeDtypeStruct((batch, seq, ML_DIM), F32),
        scratch_shapes=[pltpu.VMEM((ML_DIM, ML_DIM), F32), pltpu.VMEM((1, ML_DIM), F32),
                        pltpu.VMEM((1, ML_DIM), F32)],
        compiler_params=_cparams("parallel", "arbitrary"),
        name="mlstm",
    )(p_ml3, p_ml3, conv_w, conv_b, norm_w, expand, ones_bd)


def _outproj_kernel(rw_ref, sw_ref, ml_ref, h_ref, wrw_ref, wsw_ref, wml_ref, lnw_ref, lnb_ref,
                    rwh_ref, rwm_ref, rwl_ref, rb_ref, h1_o, h1p_o, idx_o, gate_o):
    mix = (_dot(rw_ref[...], wrw_ref[...]) + _dot(sw_ref[...], wsw_ref[...]) + _dot(ml_ref[...], wml_ref[...]))
    h1 = _layer_norm(DN_ALPHA * h_ref[...] + mix, lnw_ref[...], lnb_ref[...])
    h1_o[...] = h1
    h1p_o[...] = _pack_bf16_pairs(h1)
    xh, xm, xl = _split3(h1)
    wh, wm, wl = rwh_ref[...], rwm_ref[...], rwl_ref[...]
    dd = lambda a, b: jnp.dot(a, b, preferred_element_type=F32)
    logits = (dd(xh, wh) + (dd(xh, wm) + dd(xm, wh)) + (dd(xh, wl) + dd(xm, wm) + dd(xl, wh))) + rb_ref[...]
    lane = lax.broadcasted_iota(jnp.int32, logits.shape, 1).astype(F32)
    vals, idxs = [], []
    cur = logits
    for _ in range(TOP_K):
        mx = jnp.max(cur, axis=-1, keepdims=True)
        ix = jnp.min(jnp.where(cur == mx, lane, float(LANES)), axis=-1, keepdims=True)
        vals.append(mx)
        idxs.append(ix)
        cur = jnp.where(lane == ix, NEG_BIG * 2.0, cur)
    es = [jnp.exp(vv - vals[0]) for vv in vals]
    den = es[0] + es[1] + es[2] + es[3]
    idx_out = jnp.zeros_like(logits)
    gate_out = jnp.zeros_like(logits)
    for kk in range(TOP_K):
        idx_out = jnp.where(lane == float(kk), idxs[kk], idx_out)
        gate_out = jnp.where(lane == float(kk), es[kk] / den, gate_out)
    idx_o[...] = idx_out.astype(jnp.int32)
    gate_o[...] = gate_out


def _out_proj(rw_out, sw_out, ml_out, h2d, w_rw, w_sw, w_ml, ln_w, ln_b, rw_parts, rb):
    m = h2d.shape[0]
    tm = ROW_TILE
    row = lambda i: (i, 0)
    fixed = lambda i: (0, 0)
    full = lambda a: pl.BlockSpec(a.shape, fixed)
    return pl.pallas_call(
        _outproj_kernel,
        grid=(m // tm,),
        in_specs=[pl.BlockSpec((tm, RW_DIM), row), pl.BlockSpec((tm, SW_DIM), row), pl.BlockSpec((tm, ML_DIM), row),
                  pl.BlockSpec((tm, D_MODEL), row), full(w_rw), full(w_sw), full(w_ml), full(ln_w), full(ln_b),
                  full(rw_parts[0]), full(rw_parts[1]), full(rw_parts[2]), full(rb)],
        out_specs=[pl.BlockSpec((tm, D_MODEL), row), pl.BlockSpec((tm, D_MODEL // 2), row),
                   pl.BlockSpec((tm, LANES), row), pl.BlockSpec((tm, LANES), row)],
        out_shape=[jax.ShapeDtypeStruct((m, D_MODEL), F32), jax.ShapeDtypeStruct((m, D_MODEL // 2), jnp.uint32),
                   jax.ShapeDtypeStruct((m, LANES), jnp.int32), jax.ShapeDtypeStruct((m, LANES), F32)],
        compiler_params=_cparams("parallel"),
        name="out_proj_router",
    )(rw_out, sw_out, ml_out, h2d, w_rw, w_sw, w_ml, ln_w, ln_b, *rw_parts, rb)


def _expert_kernel(n_blocks, layer, be_ref, tokc_ref, tokn_ref, slot_ref, h_hbm, wup_ref, bg_ref, bl_ref,
                   wdn_ref, bd_ref, perm_ref, z_hbm, xbuf, ybuf, wg_s, wl_s, wd_s, gsem, ssem):
    del layer
    i = pl.program_id(0)
    s = i % 2
    bm = MOE_BLOCK

    def gather_row(tok, buf, r):
        return pltpu.make_async_copy(h_hbm.at[pl.ds(tok, 1)], xbuf.at[buf, pl.ds(r, 1)], gsem.at[buf])

    def scatter_row(dst, buf, r):
        return pltpu.make_async_copy(ybuf.at[buf, pl.ds(r, 1)], z_hbm.at[pl.ds(dst, 1)], ssem.at[buf])

    def wait_gather(buf):
        pltpu.make_async_copy(h_hbm.at[pl.ds(0, bm)], xbuf.at[buf], gsem.at[buf]).wait()

    def wait_scatter(buf):
        pltpu.make_async_copy(ybuf.at[buf], z_hbm.at[pl.ds(0, bm)], ssem.at[buf]).wait()

    @pl.when(i == 0)
    def _():
        ybuf[1] = jnp.zeros(ybuf.shape[1:], F32)

        def prime(r, c):
            gather_row(tokc_ref[r], 0, r).start()
            return c

        lax.fori_loop(0, bm, prime, 0, unroll=8)

    wait_gather(s)

    @pl.when(i > 0)
    def _():
        wait_scatter(s)

    blk = jnp.minimum(i, n_blocks - 1)
    e_now = be_ref[blk]
    e_prev = be_ref[jnp.maximum(blk - 1, 0)]

    @pl.when((i == 0) | (e_now != e_prev))
    def _():
        perm = perm_ref[...]
        for c in range(D_EXPERT // LANES):
            t = jnp.dot(wup_ref[0, 0, :, c * 2 * LANES:(c + 1) * 2 * LANES].astype(BF16), perm,
                        preferred_element_type=F32)
            wg_s[:, c * LANES:(c + 1) * LANES] = t[:, 0:LANES].astype(BF16)
            wl_s[:, c * LANES:(c + 1) * LANES] = t[:, LANES:2 * LANES].astype(BF16)
        wd_s[...] = wdn_ref[0, 0].astype(BF16)

    for r in range(bm):
        gather_row(tokn_ref[r], 1 - s, r).start(priority=r % 2)
        scatter_row(slot_ref[r], 1 - s, r).start(priority=(r + 1) % 2)
    x = _unpack_bf16_pairs(xbuf[s])
    hg = jnp.dot(x, wg_s[...], preferred_element_type=F32) + bg_ref[0, 0]
    hl = jnp.dot(x, wl_s[...], preferred_element_type=F32) + bl_ref[0, 0]
    hg = jnp.minimum(hg, SWIGLU_LIMIT)
    hl = jnp.clip(hl, -SWIGLU_LIMIT, SWIGLU_LIMIT)
    act = hg * _sigmoid(SWIGLU_ALPHA * hg) * (hl + 1.0)
    ybuf[s] = jnp.dot(act.astype(BF16), wd_s[...], preferred_element_type=F32) + bd_ref[0, 0]

    @pl.when(i == n_blocks)
    def _():
        wait_gather(1 - s)
        wait_scatter(1 - s)


def _experts(layer, block_e, tok, slot_ext, h1, w_up, b_glu, b_lin, w_down, b_down, perm, z_rows):
    bm = MOE_BLOCK
    n_blocks = tok.shape[0] // bm
    last = n_blocks - 1
    by_e = lambda i, be: (layer, be[jnp.minimum(i, last)], 0, 0)
    smem = lambda f: pl.BlockSpec((bm,), f, memory_space=pltpu.SMEM)
    grid_spec = pltpu.PrefetchScalarGridSpec(
        num_scalar_prefetch=1,
        grid=(n_blocks + 1,),
        in_specs=[smem(lambda i, be: (jnp.minimum(i, last),)), smem(lambda i, be: (jnp.minimum(i + 1, last),)),
                  smem(lambda i, be: (i,)),
                  pl.BlockSpec(memory_space=pl.ANY),
                  pl.BlockSpec((1, 1, D_MODEL, 2 * D_EXPERT), by_e),
                  pl.BlockSpec((1, 1, 1, D_EXPERT), by_e), pl.BlockSpec((1, 1, 1, D_EXPERT), by_e),
                  pl.BlockSpec((1, 1, D_EXPERT, D_MODEL), by_e), pl.BlockSpec((1, 1, 1, D_MODEL), by_e),
                  pl.BlockSpec((2 * LANES, 2 * LANES), lambda i, be: (0, 0))],
        out_specs=pl.BlockSpec(memory_space=pl.ANY),
        scratch_shapes=[pltpu.VMEM((2, bm, D_MODEL // 2), jnp.uint32), pltpu.VMEM((2, bm, D_MODEL), F32),
                        pltpu.VMEM((D_MODEL, D_EXPERT), BF16), pltpu.VMEM((D_MODEL, D_EXPERT), BF16),
                        pltpu.VMEM((D_EXPERT, D_MODEL), BF16),
                        pltpu.SemaphoreType.DMA((2,)), pltpu.SemaphoreType.DMA((2,))],
    )
    return pl.pallas_call(
        functools.partial(_expert_kernel, n_blocks, layer),
        grid_spec=grid_spec,
        out_shape=jax.ShapeDtypeStruct((z_rows, D_MODEL), F32),
        compiler_params=pltpu.CompilerParams(dimension_semantics=("arbitrary",), vmem_limit_bytes=VMEM_LIMIT,
                                             disable_bounds_checks=True),
        name="experts",
    )(block_e, tok, tok, slot_ext, h1, w_up, b_glu, b_lin, w_down, b_down, perm)


def _combine_kernel(z0_ref, z1_ref, z2_ref, z3_ref, gate_ref, h_ref, lnw_ref, lnb_ref, o_ref):
    gates = gate_ref[...]
    y = gates[:, 0:1] * z0_ref[...]
    for kk, z_ref in enumerate((z1_ref, z2_ref, z3_ref), start=1):
        y = y + gates[:, kk:kk + 1] * z_ref[...]
    o_ref[...] = _layer_norm(DN_ALPHA * h_ref[...] + y, lnw_ref[...], lnb_ref[...])


def _combine(z, gates, h1, ln_w, ln_b):
    m = h1.shape[0]
    tm = ROW_TILE
    n_tiles = m // tm
    row = lambda i: (i, 0)
    fixed = lambda i: (0, 0)
    choice = lambda kk: pl.BlockSpec((tm, D_MODEL), lambda i: (kk * n_tiles + i, 0))
    return pl.pallas_call(
        _combine_kernel,
        grid=(n_tiles,),
        in_specs=[choice(0), choice(1), choice(2), choice(3), pl.BlockSpec((tm, LANES), row),
                  pl.BlockSpec((tm, D_MODEL), row), pl.BlockSpec((1, D_MODEL), fixed), pl.BlockSpec((1, D_MODEL), fixed)],
        out_specs=pl.BlockSpec((tm, D_MODEL), row),
        out_shape=jax.ShapeDtypeStruct((m, D_MODEL), F32),
        compiler_params=_cparams("parallel"),
        name="combine_ln",
    )(z, z, z, z, gates, h1, ln_w, ln_b)


def _routing_tables(top_idx):
    m = top_idx.shape[0]
    n_assign = m * TOP_K
    bm = MOE_BLOCK
    n_blocks = -(-(n_assign + N_EXPERTS * (bm - 1)) // bm)
    p_rows = n_blocks * bm
    e_flat = top_idx.reshape(n_assign)
    order = jnp.argsort(e_flat, stable=True).astype(jnp.int32)
    experts = jnp.arange(N_EXPERTS, dtype=jnp.int32)
    counts = jnp.sum((e_flat[:, None] == experts[None, :]).astype(jnp.int32), axis=0)
    padded = (counts + bm - 1) // bm * bm
    start = jnp.cumsum(counts) - counts
    pend = jnp.cumsum(padded)
    pstart = pend - padded
    block_start = jnp.arange(n_blocks, dtype=jnp.int32) * bm
    block_e = jnp.minimum(jnp.sum((pend[None, :] <= block_start[:, None]).astype(jnp.int32), axis=1), N_EXPERTS - 1)
    per_row = lambda per_expert: jnp.repeat(per_expert[block_e], bm)
    row_pstart, row_start, row_count = per_row(pstart), per_row(start), per_row(counts)
    pos = jnp.arange(p_rows, dtype=jnp.int32) - row_pstart
    is_pad = pos >= row_count
    src_assign = order[jnp.clip(row_start + pos, 0, n_assign - 1)]
    tok = jnp.where(is_pad, 0, src_assign // TOP_K)
    pad_rank = (row_pstart - row_start) + (pos - row_count)
    slot = jnp.where(is_pad, n_assign + pad_rank, (src_assign % TOP_K) * m + src_assign // TOP_K)
    slot_ext = jnp.concatenate([p_rows + jnp.arange(bm, dtype=jnp.int32), slot])
    return tok, slot_ext, block_e, p_rows + bm


def _block_ones(n):
    g = np.arange(n) // HEAD_DIM
    return jnp.asarray(g[:, None] == g[None, :], BF16)


def _sw_head_order():
    group = SW_HEADS // SW_KV_HEADS
    heads = [kv * group + g for g in range(group) for kv in range(SW_KV_HEADS)]
    return np.concatenate([np.arange(h * HEAD_DIM, (h + 1) * HEAD_DIM) for h in heads])


def _rope_tables(seq):
    half = HEAD_DIM // 2
    inv = ROPE_THETA ** (-jnp.arange(half, dtype=F32) / half)
    ang = jnp.arange(seq, dtype=F32)[:, None] * inv[None, :]
    cos, sin = jnp.cos(ang), jnp.sin(ang)
    reps = LANES // HEAD_DIM
    cos_t = jnp.tile(jnp.concatenate([cos, cos], axis=-1), (1, reps))
    sin_t = jnp.tile(jnp.concatenate([-sin, sin], axis=-1), (1, reps))
    return cos_t, sin_t


def kernel(x, w_in, b_in, rw_shift_mu, rw_w_up, rw_w0, rw_a_up, rw_a0, rw_g_up, rw_k_k, rw_k_a, rw_r_k, rw_ln_w, rw_ln_b, sw_sinks, ml_conv_w, ml_conv_b, ml_norm_w, w_out, ln1_w, ln1_b, router_w, router_b, exp_w_up, exp_b_up, exp_w_down, exp_b_down, ln2_w, ln2_b):
    batch, seq, d_model = x.shape
    assert d_model == D_MODEL and seq % ROW_TILE == 0 and seq % ML_CHUNK == 0 and seq % WINDOW == 0
    m = batch * seq
    depth = w_in.shape[0]
    row2 = lambda t: t.reshape(1, -1)

    sw_perm = _sw_head_order()
    in_cols = np.concatenate([np.arange(RW_COLS), RW_COLS + sw_perm, np.arange(RW_COLS + SW_DIM, C_IN)])
    ones_rw = _block_ones(RW_DIM)
    ones_ml = _block_ones(ML_DIM)
    cos_t, sin_t = _rope_tables(seq)
    src = np.arange(LANES)[:, None]
    dst = np.arange(2 * ML_DIM)[None, :]
    expand = jnp.asarray(src == (dst // ML_DIM) * ML_HEADS + (dst % ML_DIM) // HEAD_DIM, BF16)
    pj = np.arange(2 * LANES)[:, None]
    pc = np.arange(2 * LANES)[None, :]
    perm = jnp.asarray(pj == 2 * (pc % LANES) + pc // LANES, BF16)
    n_layers = exp_b_up.shape[0]
    b_glu = exp_b_up[:, :, 0::2].reshape(n_layers, N_EXPERTS, 1, D_EXPERT)
    b_lin = exp_b_up[:, :, 1::2].reshape(n_layers, N_EXPERTS, 1, D_EXPERT)
    b_dn = exp_b_down.reshape(n_layers, N_EXPERTS, 1, D_MODEL)

    h = x.reshape(m, D_MODEL)
    for l in range(depth):
        w_l = jnp.pad(w_in[l][:, in_cols], ((0, 0), (0, C_IN_PAD - C_IN))).astype(BF16)
        b_l = jnp.pad(b_in[l][in_cols], (0, C_IN_PAD - C_IN)).reshape(1, C_IN_PAD)
        zero = jnp.zeros((RW_LORA_W, RW_DIM), F32)
        wcomb = jnp.concatenate([jnp.concatenate([rw_w_up[l], zero], axis=1),
                                 jnp.concatenate([zero, rw_a_up[l]], axis=1)], axis=0).astype(BF16)
        w_o = w_out[l]
        w_o_rw = w_o[0:RW_DIM].astype(BF16)
        w_o_sw = w_o[RW_DIM + sw_perm].astype(BF16)
        w_o_ml = w_o[RW_DIM + SW_DIM:].astype(BF16)
        rw_pad = jnp.pad(router_w[l], ((0, 0), (0, LANES - N_EXPERTS)))
        rw_hi = rw_pad.astype(BF16)
        rw_r1 = rw_pad - rw_hi.astype(F32)
        rw_mid = rw_r1.astype(BF16)
        rw_lo = (rw_r1 - rw_mid.astype(F32)).astype(BF16)
        rb_pad = jnp.pad(router_b[l], (0, LANES - N_EXPERTS), constant_values=NEG_BIG).reshape(1, LANES)

        p_rw, p_sw, p_ml = _in_proj(h, w_l, b_l)
        r, lw, k, v, a, b, g = _rw_prep(p_rw, seq, row2(rw_shift_mu[l]), wcomb, rw_g_up[l].astype(BF16),
                                        row2(rw_w0[l]), row2(rw_a0[l]), row2(rw_k_k[l]), row2(rw_k_a[l]), ones_rw)
        m_mat, yh, g_mat, y0 = _rw_chunks(r, lw, k, v, a, b)
        y = _rw_scan(batch, seq, m_mat, yh, g_mat, y0).reshape(m, RW_DIM)
        rw_out = _rw_post(y, r, k, v, g, row2(rw_r_k[l]), row2(rw_ln_w[l]), row2(rw_ln_b[l]), ones_rw)
        sw_out = _swa(p_sw.reshape(batch, seq, SW_COLS), sw_sinks[l], cos_t, sin_t).reshape(m, SW_DIM)
        ml_out = _mlstm(p_ml.reshape(batch, seq, ML_COLS_PAD), ml_conv_w[l], row2(ml_conv_b[l]),
                        row2(ml_norm_w[l]), expand, ones_ml).reshape(m, ML_DIM)
        h1, h1_packed, idx_pad, gates = _out_proj(rw_out, sw_out, ml_out, h, w_o_rw, w_o_sw, w_o_ml, row2(ln1_w[l]),
                                                 row2(ln1_b[l]), (rw_hi, rw_mid, rw_lo), rb_pad)

        tok, slot_ext, block_e, z_rows = _routing_tables(idx_pad[:, 0:TOP_K])
        z = _experts(l, block_e, tok, slot_ext, h1_packed, exp_w_up, b_glu, b_lin, exp_w_down, b_dn, perm, z_rows)
        h = _combine(z, gates, h1, row2(ln2_w[l]), row2(ln2_b[l]))
    return h.reshape(batch, seq, D_MODEL)
```

```python
import functools

import numpy as np
import jax
import jax.numpy as jnp
from jax import lax
from jax.experimental import pallas as pl
from jax.experimental.pallas import tpu as pltpu

F32 = jnp.float32
BF16 = jnp.bfloat16

D_MODEL = 1024
HEAD_DIM = 64
RW_HEADS = 6
SW_HEADS = 6
SW_KV_HEADS = 2
ML_HEADS = 4
RW_DIM = RW_HEADS * HEAD_DIM
SW_DIM = SW_HEADS * HEAD_DIM
SW_KV_DIM = SW_KV_HEADS * HEAD_DIM
ML_DIM = ML_HEADS * HEAD_DIM
RW_LORA_W = 64
RW_LORA_A = 64
RW_LORA_G = 128
RW_COLS = 3 * RW_DIM + RW_LORA_W + RW_LORA_A + RW_LORA_G
SW_COLS = SW_DIM + 2 * SW_KV_DIM
ML_COLS = 4 * ML_DIM + 2 * ML_HEADS
C_IN = RW_COLS + SW_COLS + ML_COLS
WINDOW = 128
ROPE_THETA = 10000.0
CONV_WIDTH = 4
N_EXPERTS = 32
TOP_K = 4
D_EXPERT = 1024
SWIGLU_ALPHA = 1.702
SWIGLU_LIMIT = 7.0
LN_EPS = 1e-5
RW_GN_EPS = 64e-5
ML_GN_EPS = 1e-6
DEPTH = 4
DN_ALPHA = (2 * DEPTH) ** 0.25

LANES = 128
SUBLANES = 8
ML_COLS_PAD = 1152
C_IN_PAD = RW_COLS + SW_COLS + ML_COLS_PAD
ROW_TILE = 512
RW_CHUNK = 64
RW_PAIR = 2 * HEAD_DIM
RW_NPAIR = RW_HEADS // 2
ML_CHUNK = 256
MOE_BLOCK = 512
NEG_BIG = -1e30
VMEM_LIMIT = 56 * 1024 * 1024


def _cparams(*sem):
    return pltpu.CompilerParams(dimension_semantics=sem, vmem_limit_bytes=VMEM_LIMIT)


def _dot(a, b):
    return jnp.dot(a.astype(BF16), b.astype(BF16), preferred_element_type=F32)


def _dot_nt(a, b):
    return lax.dot_general(a.astype(BF16), b.astype(BF16), (((1,), (1,)), ((), ())),
                           preferred_element_type=F32)


def _split3(x):
    hi = x.astype(BF16)
    r1 = x - hi.astype(F32)
    mid = r1.astype(BF16)
    lo = (r1 - mid.astype(F32)).astype(BF16)
    return hi, mid, lo


def _dot_xl(x, w):
    hi, mid, lo = _split3(x)
    return (jnp.dot(hi, w, preferred_element_type=F32) + jnp.dot(mid, w, preferred_element_type=F32)
            + jnp.dot(lo, w, preferred_element_type=F32))


def _dot_lx(w, x):
    hi, mid, lo = _split3(x)
    return (jnp.dot(w, hi, preferred_element_type=F32) + jnp.dot(w, mid, preferred_element_type=F32)
            + jnp.dot(w, lo, preferred_element_type=F32))


def _softplus(z):
    return jnp.maximum(z, 0.0) + jnp.log(1.0 + jnp.exp(-jnp.abs(z)))


def _sigmoid(z):
    return 1.0 / (1.0 + jnp.exp(-z))


def _shift_rows(x, prev8, j):
    cat = jnp.concatenate([prev8, x], axis=0)
    return pltpu.roll(cat, j, 0)[SUBLANES:, :]


def _layer_norm(x, w, b):
    mu = jnp.mean(x, axis=-1, keepdims=True)
    d = x - mu
    var = jnp.mean(d * d, axis=-1, keepdims=True)
    return d * lax.rsqrt(var + LN_EPS) * w + b


def _pack_bf16_pairs(x):
    c = x.shape[1] // 2
    bits = pltpu.bitcast(x.astype(BF16).astype(F32), jnp.uint32)
    return lax.shift_right_logical(bits[:, 0:c], jnp.uint32(16)) | (bits[:, c:2 * c] & jnp.uint32(0xFFFF0000))


def _unpack_bf16_pairs(w):
    lo = pltpu.bitcast(lax.shift_left(w, jnp.uint32(16)), F32)
    hi = pltpu.bitcast(w & jnp.uint32(0xFFFF0000), F32)
    return jnp.concatenate([lo, hi], axis=1).astype(BF16)


def _head_norm(y, ones_bd, eps):
    inv = 1.0 / HEAD_DIM
    mu = _dot_xl(y, ones_bd) * inv
    d = y - mu
    var = _dot_xl(d * d, ones_bd) * inv
    return d * lax.rsqrt(var + eps)


def _inproj_kernel(x_ref, w_ref, b_ref, rw_ref, sw_ref, ml_ref):
    x = x_ref[...].astype(BF16)
    c1, c2 = RW_COLS, RW_COLS + SW_COLS
    rw_ref[...] = jnp.dot(x, w_ref[:, 0:c1], preferred_element_type=F32) + b_ref[:, 0:c1]
    sw_ref[...] = jnp.dot(x, w_ref[:, c1:c2], preferred_element_type=F32) + b_ref[:, c1:c2]
    ml_ref[...] = jnp.dot(x, w_ref[:, c2:C_IN_PAD], preferred_element_type=F32) + b_ref[:, c2:C_IN_PAD]


def _in_proj(h2d, w, b):
    m = h2d.shape[0]
    tm = ROW_TILE
    row = lambda i: (i, 0)
    fixed = lambda i: (0, 0)
    return pl.pallas_call(
        _inproj_kernel,
        grid=(m // tm,),
        in_specs=[pl.BlockSpec((tm, D_MODEL), row), pl.BlockSpec((D_MODEL, C_IN_PAD), fixed),
                  pl.BlockSpec((1, C_IN_PAD), fixed)],
        out_specs=[pl.BlockSpec((tm, RW_COLS), row), pl.BlockSpec((tm, SW_COLS), row),
                   pl.BlockSpec((tm, ML_COLS_PAD), row)],
        out_shape=[jax.ShapeDtypeStruct((m, RW_COLS), F32), jax.ShapeDtypeStruct((m, SW_COLS), F32),
                   jax.ShapeDtypeStruct((m, ML_COLS_PAD), F32)],
        compiler_params=_cparams("parallel"),
        name="in_proj",
    )(h2d, w, b)


def _rw_prep_kernel(tiles_per_seq, p_ref, pprev_ref, mu_ref, wcomb_ref, gup_ref, w0_ref, a0_ref, kk_ref,
                    ka_ref, ones_ref, r_o, lw_o, k_o, v_o, a_o, b_o, g_o):
    i = pl.program_id(0)
    x = p_ref[...]
    first = (i % tiles_per_seq) == 0
    prev8 = jnp.where(first, 0.0, pprev_ref[...])
    xs = x + (_shift_rows(x, prev8, 1) - x) * mu_ref[...]
    d = RW_DIM
    r = xs[:, 0:d]
    k = xs[:, d:2 * d]
    v = xs[:, 2 * d:3 * d]
    slab = xs[:, 3 * d:3 * d + LANES]
    lane = lax.broadcasted_iota(jnp.int32, slab.shape, 1)
    slab = jnp.where(lane < RW_LORA_W, jnp.tanh(slab), slab)
    wa = _dot(slab, wcomb_ref[...])
    w_pre = w0_ref[...] + wa[:, 0:d]
    a_pre = a0_ref[...] + wa[:, d:2 * d]
    w_log = -_softplus(-w_pre) - 0.5
    log_decay = -jnp.exp(w_log)
    a_sig = _sigmoid(a_pre)
    g = _dot(_sigmoid(xs[:, 3 * d + LANES:3 * d + 2 * LANES]), gup_ref[...])
    kk = k * kk_ref[...]
    ss = _dot_xl(kk * kk, ones_ref[...])
    kk = kk / jnp.maximum(jnp.sqrt(ss), 1e-12)
    r_o[...] = r
    lw_o[...] = log_decay
    k_o[...] = k * (1.0 + (a_sig - 1.0) * ka_ref[...])
    v_o[...] = v
    a_o[...] = -kk
    b_o[...] = kk * a_sig
    g_o[...] = g


def _rw_prep(p_rw, seq, mu, wcomb, gup, w0, a0, k_k, k_a, ones_bd):
    m = p_rw.shape[0]
    tm = ROW_TILE
    row = lambda i: (i, 0)
    fixed = lambda i: (0, 0)
    prev = lambda i: (jnp.maximum(i * (tm // SUBLANES) - 1, 0), 0)
    vec = pl.BlockSpec((1, RW_DIM), fixed)
    out = jax.ShapeDtypeStruct((m, RW_DIM), F32)
    return pl.pallas_call(
        functools.partial(_rw_prep_kernel, seq // tm),
        grid=(m // tm,),
        in_specs=[pl.BlockSpec((tm, RW_COLS), row), pl.BlockSpec((SUBLANES, RW_COLS), prev),
                  pl.BlockSpec((1, RW_COLS), fixed), pl.BlockSpec((LANES, 2 * RW_DIM), fixed),
                  pl.BlockSpec((RW_LORA_G, RW_DIM), fixed), vec, vec, vec, vec,
                  pl.BlockSpec((RW_DIM, RW_DIM), fixed)],
        out_specs=[pl.BlockSpec((tm, RW_DIM), row)] * 7,
        out_shape=[out] * 7,
        compiler_params=_cparams("parallel"),
        name="rw_prep",
    )(p_rw, p_rw, mu, wcomb, gup, w0, a0, k_k, k_a, ones_bd)


def _rw_chunk_kernel(n_chunk, r_ref, lw_ref, k_ref, v_ref, a_ref, b_ref, m_o, yh_o, g_o, y0_o):
    L = RW_CHUNK
    ti = lax.broadcasted_iota(jnp.int32, (L, L), 0)
    si = lax.broadcasted_iota(jnp.int32, (L, L), 1)
    tri = jnp.where(ti >= si, 1.0, 0.0).astype(BF16)
    n2 = 2 * L
    ri = lax.broadcasted_iota(jnp.int32, (n2, n2), 0)
    ci = lax.broadcasted_iota(jnp.int32, (n2, n2), 1)
    rr = jnp.where(ri >= L, ri - L, ri)
    cc = jnp.where(ci >= L, ci - L, ci)
    strict = rr > cc
    incl = rr >= cc
    eye = ri == ci
    eye_f = jnp.where(eye, 1.0, 0.0)
    lane = lax.broadcasted_iota(jnp.int32, (L, RW_PAIR), 1)
    head0 = lane < HEAD_DIM

    def bmm(x, y):
        return jnp.einsum('bmk,bkn->bmn', x.astype(BF16), y.astype(BF16), preferred_element_type=F32)

    def bmm_nt(x, y):
        return jnp.einsum('bmk,bnk->bmn', x.astype(BF16), y.astype(BF16), preferred_element_type=F32)

    def bmm_tn(x, y):
        return bmm(jnp.swapaxes(x, 1, 2), y)

    def stack(x):
        return jnp.concatenate([jnp.where(head0, x, 0.0), jnp.where(head0, 0.0, x)], axis=1)

    def chunked(ref, cols):
        return ref[:, cols].reshape(n_chunk, L, RW_PAIR)

    tri_b = jnp.broadcast_to(tri, (n_chunk, L, L))
    for p in range(RW_NPAIR):
        cols = slice(p * RW_PAIR, (p + 1) * RW_PAIR)
        lw = chunked(lw_ref, cols)
        hi, mid, lo = _split3(lw)
        tsum = lambda part: jnp.einsum('bts,bsn->btn', tri_b, part, preferred_element_type=F32)
        lc = tsum(hi) + tsum(mid) + tsum(lo)
        lcl = lc[:, L - 1:L, :]
        gam = jnp.exp(lc)
        gam_prev = jnp.exp(lc - lw)
        inv = jnp.exp(-lc)
        to_end = jnp.exp(lcl - lc)
        r = chunked(r_ref, cols)
        k = chunked(k_ref, cols)
        v = chunked(v_ref, cols)
        a = chunked(a_ref, cols)
        b = chunked(b_ref, cols)
        x1 = stack(a * gam_prev)
        r1 = stack(r * gam)
        x2 = stack(b * inv)
        k2 = stack(k * inv)
        vs = stack(v)
        bs = stack(b * to_end)
        ks = stack(k * to_end)
        aa = bmm_nt(jnp.concatenate([x1, r1], axis=1), jnp.concatenate([x2, k2], axis=1))
        a_ab = jnp.where(strict, aa[:, 0:n2, 0:n2], 0.0)
        a_ak = jnp.where(strict, aa[:, 0:n2, n2:2 * n2], 0.0)
        a_rb = jnp.where(incl, aa[:, n2:2 * n2, 0:n2], 0.0)
        a_rk = jnp.where(incl, aa[:, n2:2 * n2, n2:2 * n2], 0.0)
        apow = a_ab
        t_inv = eye_f + a_ab
        for _ in range(5):
            apow = bmm(apow, apow)
            t_inv = t_inv + bmm(t_inv, apow)
        akv = bmm(a_ak, vs)
        pw = bmm(t_inv, jnp.concatenate([x1, akv], axis=2))
        pmat = pw[:, :, 0:RW_PAIR]
        wmat = pw[:, :, RW_PAIR:2 * RW_PAIR]
        gl_row = jnp.exp(lcl)
        m_mat = bmm_tn(bs, pmat) + jnp.where(eye, gl_row, 0.0)
        g_mat = bmm_tn(jnp.concatenate([bs, ks], axis=1), jnp.concatenate([wmat, vs], axis=1))
        yy = bmm(a_rb, pw)
        yh = r1 + yy[:, :, 0:RW_PAIR]
        y0 = yy[:, :, RW_PAIR:2 * RW_PAIR] + bmm(a_rk, vs)
        m_o[:, p] = m_mat.astype(BF16)
        yh_o[:, p] = yh.astype(BF16)
        g_o[:, p] = g_mat
        y0_o[:, p] = y0


def _rw_chunks(r, lw, k, v, a, b):
    m = r.shape[0]
    tm = ROW_TILE
    nc = tm // RW_CHUNK
    row = lambda i: (i, 0)
    blk = lambda i: (i, 0, 0, 0)
    n2 = 2 * RW_CHUNK
    n_tot = m // RW_CHUNK
    ospec = pl.BlockSpec((nc, RW_NPAIR, n2, RW_PAIR), blk)
    return pl.pallas_call(
        functools.partial(_rw_chunk_kernel, nc),
        grid=(m // tm,),
        in_specs=[pl.BlockSpec((tm, RW_DIM), row)] * 6,
        out_specs=[ospec] * 4,
        out_shape=[jax.ShapeDtypeStruct((n_tot, RW_NPAIR, n2, RW_PAIR), BF16),
                   jax.ShapeDtypeStruct((n_tot, RW_NPAIR, n2, RW_PAIR), BF16),
                   jax.ShapeDtypeStruct((n_tot, RW_NPAIR, n2, RW_PAIR), F32),
                   jax.ShapeDtypeStruct((n_tot, RW_NPAIR, n2, RW_PAIR), F32)],
        compiler_params=_cparams("parallel"),
        name="rw_chunks",
    )(r, lw, k, v, a, b)


def _rw_scan_kernel(batch, m_ref, yh_ref, g_ref, y0_ref, y_o, h_ref):
    c = pl.program_id(0)

    @pl.when(c == 0)
    def _():
        h_ref[...] = jnp.zeros_like(h_ref)

    L = RW_CHUNK
    for bi in range(batch):
        for p in range(RW_NPAIR):
            hb = h_ref[bi, p].astype(BF16)
            yrows = jnp.dot(yh_ref[bi, 0, p], hb, preferred_element_type=F32) + y0_ref[bi, 0, p]
            y_o[bi, :, p * RW_PAIR:(p + 1) * RW_PAIR] = yrows[0:L, :] + yrows[L:2 * L, :]
            h_ref[bi, p] = jnp.dot(m_ref[bi, 0, p], hb, preferred_element_type=F32) + g_ref[bi, 0, p]


def _rw_scan(batch, seq, m_mat, yh, g_mat, y0):
    nc = seq // RW_CHUNK
    n2 = 2 * RW_CHUNK
    shp = (batch, nc, RW_NPAIR, n2, RW_PAIR)
    args = [t.reshape(shp) for t in (m_mat, yh, g_mat, y0)]
    ispec = pl.BlockSpec((batch, 1, RW_NPAIR, n2, RW_PAIR), lambda c: (0, c, 0, 0, 0))
    return pl.pallas_call(
        functools.partial(_rw_scan_kernel, batch),
        grid=(nc,),
        in_specs=[ispec] * 4,
        out_specs=pl.BlockSpec((batch, RW_CHUNK, RW_DIM), lambda c: (0, c, 0)),
        out_shape=jax.ShapeDtypeStruct((batch, seq, RW_DIM), F32),
        scratch_shapes=[pltpu.VMEM((batch, RW_NPAIR, RW_PAIR, RW_PAIR), F32)],
        compiler_params=_cparams("arbitrary"),
        name="rw_scan",
    )(*args)


def _rw_post_kernel(y_ref, r_ref, k_ref, v_ref, g_ref, rk_ref, lnw_ref, lnb_ref, ones_ref, o_ref):
    ones_bd = ones_ref[...]
    y = _head_norm(y_ref[...], ones_bd, RW_GN_EPS) * lnw_ref[...] + lnb_ref[...]
    bonus = _dot_xl(r_ref[...] * k_ref[...] * rk_ref[...], ones_bd) * v_ref[...]
    o_ref[...] = (y + bonus) * g_ref[...]


def _rw_post(y, r, k, v, g, r_k, ln_w, ln_b, ones_bd):
    m = y.shape[0]
    tm = ROW_TILE
    row = lambda i: (i, 0)
    fixed = lambda i: (0, 0)
    vec = pl.BlockSpec((1, RW_DIM), fixed)
    return pl.pallas_call(
        _rw_post_kernel,
        grid=(m // tm,),
        in_specs=[pl.BlockSpec((tm, RW_DIM), row)] * 5 + [vec, vec, vec, pl.BlockSpec((RW_DIM, RW_DIM), fixed)],
        out_specs=pl.BlockSpec((tm, RW_DIM), row),
        out_shape=jax.ShapeDtypeStruct((m, RW_DIM), F32),
        compiler_params=_cparams("parallel"),
        name="rw_post",
    )(y, r, k, v, g, r_k, ln_w, ln_b, ones_bd)


def _rope(x, cos, sin_signed):
    lane = lax.broadcasted_iota(jnp.int32, x.shape, 1)
    half = HEAD_DIM // 2
    first_half = (lane % HEAD_DIM) < half
    rot = jnp.where(first_half, pltpu.roll(x, LANES - half, 1), pltpu.roll(x, half, 1))
    return x * cos + rot * sin_signed


def _swa_kernel(sink_ref, q_ref, kc_ref, vc_ref, kp_ref, vp_ref, cosc_ref, sinc_ref, cosp_ref, sinp_ref, o_ref):
    j = pl.program_id(1)
    W = WINDOW
    cos_c, sin_c = cosc_ref[...], sinc_ref[...]
    k_cat = jnp.concatenate([_rope(kp_ref[0], cosp_ref[...], sinp_ref[...]), _rope(kc_ref[0], cos_c, sin_c)], axis=0)
    v_cat = jnp.concatenate([vp_ref[0], vc_ref[0]], axis=0).astype(BF16)
    k_cat = k_cat.astype(BF16)
    qi = lax.broadcasted_iota(jnp.int32, (W, 2 * W), 0)
    kj = lax.broadcasted_iota(jnp.int32, (W, 2 * W), 1)
    lo = jnp.where(j > 0, qi, jnp.maximum(qi, W - 1))
    bias = jnp.where((kj > lo) & (kj <= qi + W), 0.0, NEG_BIG)
    lane = lax.broadcasted_iota(jnp.int32, (W, LANES), 1)
    kv0 = lane < HEAD_DIM
    scale = HEAD_DIM ** -0.5
    group = SW_HEADS // SW_KV_HEADS
    for g in range(group):
        q = _rope(q_ref[0, :, g * LANES:(g + 1) * LANES], cos_c, sin_c) * scale
        outs = []
        for kv in range(SW_KV_HEADS):
            qm = jnp.where(kv0, q, 0.0) if kv == 0 else jnp.where(kv0, 0.0, q)
            s = _dot_nt(qm, k_cat) + bias
            sink = sink_ref[kv * group + g]
            mx = jnp.maximum(jnp.max(s, axis=-1, keepdims=True), sink)
            pr = jnp.exp(s - mx)
            den = jnp.sum(pr, axis=-1, keepdims=True) + jnp.exp(sink - mx)
            outs.append(jnp.dot(pr.astype(BF16), v_cat, preferred_element_type=F32) / den)
        o_ref[0, :, g * LANES:(g + 1) * LANES] = jnp.where(kv0, outs[0], outs[1])


def _swa(p_sw3, sinks, cos_t, sin_t):
    batch, seq, _ = p_sw3.shape
    W = WINDOW
    nb = seq // W
    kcol = SW_DIM // LANES
    cur = lambda c: (lambda b, j: (b, j, c))
    prv = lambda c: (lambda b, j: (b, jnp.maximum(j - 1, 0), c))
    tab_c = pl.BlockSpec((W, LANES), lambda b, j: (j, 0))
    tab_p = pl.BlockSpec((W, LANES), lambda b, j: (jnp.maximum(j - 1, 0), 0))
    kv_blk = lambda f: pl.BlockSpec((1, W, LANES), f)
    return pl.pallas_call(
        _swa_kernel,
        grid=(batch, nb),
        in_specs=[pl.BlockSpec(memory_space=pltpu.SMEM),
                  pl.BlockSpec((1, W, SW_DIM), cur(0)),
                  kv_blk(cur(kcol)), kv_blk(cur(kcol + 1)), kv_blk(prv(kcol)), kv_blk(prv(kcol + 1)),
                  tab_c, tab_c, tab_p, tab_p],
        out_specs=pl.BlockSpec((1, W, SW_DIM), cur(0)),
        out_shape=jax.ShapeDtypeStruct((batch, seq, SW_DIM), F32),
        compiler_params=_cparams("parallel", "parallel"),
        name="swa",
    )(sinks, p_sw3, p_sw3, p_sw3, p_sw3, p_sw3, cos_t, sin_t, cos_t, sin_t)


def _mlstm_kernel(p_ref, pprev_ref, cw_ref, cb_ref, nw_ref, expand_ref, ones_ref, o_ref, c_ref, nv_ref, m_ref):
    j = pl.program_id(1)
    L = ML_CHUNK
    d = ML_DIM

    @pl.when(j == 0)
    def _():
        c_ref[...] = jnp.zeros_like(c_ref)
        nv_ref[...] = jnp.zeros_like(nv_ref)
        m_ref[...] = jnp.zeros_like(m_ref)

    x = p_ref[0]
    qk_pre = x[:, 0:2 * d]
    prev8 = jnp.where(j == 0, 0.0, pprev_ref[0][:, 0:2 * d])
    conv = cb_ref[...] + cw_ref[CONV_WIDTH - 1:CONV_WIDTH, :] * qk_pre
    for s in range(1, CONV_WIDTH):
        conv = conv + cw_ref[CONV_WIDTH - 1 - s:CONV_WIDTH - s, :] * _shift_rows(qk_pre, prev8, s)
    qk = conv * _sigmoid(conv)
    q = qk[:, 0:d]
    k = qk[:, d:2 * d] * (HEAD_DIM ** -0.5)
    v = x[:, 2 * d:3 * d]
    o_gate = x[:, 3 * d:4 * d]
    gates = _dot_xl(x[:, 4 * d:4 * d + LANES], expand_ref[...])
    i_full = gates[:, 0:d]
    f_full = gates[:, d:2 * d]
    lf_full = jnp.minimum(f_full, 0.0) - jnp.log(1.0 + jnp.exp(-jnp.abs(f_full)))
    ti = lax.broadcasted_iota(jnp.int32, (L, L), 0)
    si = lax.broadcasted_iota(jnp.int32, (L, L), 1)
    causal = ti >= si
    tri = jnp.where(causal, 1.0, 0.0).astype(BF16)
    bc_full = _dot_lx(tri, lf_full)
    u_t = (i_full - bc_full).T
    lane = lax.broadcasted_iota(jnp.int32, (L, d), 1)
    lane_row = lax.broadcasted_iota(jnp.int32, (1, d), 1)
    q_b = q.astype(BF16)
    k_b = k.astype(BF16)
    m_prev_row = m_ref[...]
    nv_row = nv_ref[...]
    qn = q * nv_row
    num = jnp.zeros((L, d), F32)
    inter_full = jnp.zeros((L, d), F32)
    den_full = jnp.zeros((L, d), F32)
    mnew_row = jnp.zeros((1, d), F32)
    for h in range(ML_HEADS):
        c0 = h * HEAD_DIM
        in_head = (lane >= c0) & (lane < c0 + HEAD_DIM)
        in_head_row = (lane_row >= c0) & (lane_row < c0 + HEAD_DIM)
        bc_col = bc_full[:, c0:c0 + 1]
        dmat = jnp.where(causal, bc_col + u_t[c0:c0 + 1, :], NEG_BIG)
        m_inter = bc_col + m_prev_row[:, c0:c0 + 1]
        m_t = jnp.maximum(m_inter, jnp.max(dmat, axis=-1, keepdims=True))
        inter = jnp.exp(m_inter - m_t)
        e = jnp.exp(dmat - m_t)
        sm = _dot_nt(jnp.where(in_head, q, 0.0), k_b) * e
        num = num + _dot(sm, jnp.where(in_head, v, 0.0))
        nq = jnp.sum(sm, axis=-1, keepdims=True) + inter * jnp.sum(jnp.where(in_head, qn, 0.0), axis=-1, keepdims=True)
        den = jnp.maximum(jnp.abs(nq), jnp.exp(-m_t))
        inter_full = jnp.where(in_head, inter, inter_full)
        den_full = jnp.where(in_head, den, den_full)
        mnew_row = jnp.where(in_head_row, m_t[L - 1:L, :], mnew_row)
    c_mat = c_ref[...]
    num = num + inter_full * jnp.dot(q_b, c_mat.astype(BF16), preferred_element_type=F32)
    hout = num / den_full
    hn = _head_norm(hout, ones_ref[...], ML_GN_EPS) * nw_ref[...]
    o_ref[0] = _sigmoid(o_gate) * hn
    bcl_row = bc_full[L - 1:L, :]
    wst = jnp.exp(bcl_row - bc_full + i_full - mnew_row)
    dec_row = jnp.exp(bcl_row + m_prev_row - mnew_row)
    kw = k * wst
    ri = lax.broadcasted_iota(jnp.int32, (d, d), 0) // HEAD_DIM
    ci = lax.broadcasted_iota(jnp.int32, (d, d), 1) // HEAD_DIM
    c_ref[...] = dec_row * c_mat + jnp.where(ri == ci, _dot(kw.T, v), 0.0)
    nv_ref[...] = dec_row * nv_row + jnp.sum(kw, axis=0, keepdims=True)
    m_ref[...] = mnew_row


def _mlstm(p_ml3, conv_w, conv_b, norm_w, expand, ones_bd):
    batch, seq, _ = p_ml3.shape
    L = ML_CHUNK
    fixed = lambda b, j: (0, 0)
    return pl.pallas_call(
        _mlstm_kernel,
        grid=(batch, seq // L),
        in_specs=[pl.BlockSpec((1, L, ML_COLS_PAD), lambda b, j: (b, j, 0)),
                  pl.BlockSpec((1, SUBLANES, ML_COLS_PAD), lambda b, j: (b, jnp.maximum(j * (L // SUBLANES) - 1, 0), 0)),
                  pl.BlockSpec((CONV_WIDTH, 2 * ML_DIM), fixed), pl.BlockSpec((1, 2 * ML_DIM), fixed),
                  pl.BlockSpec((1, ML_DIM), fixed), pl.BlockSpec((LANES, 2 * ML_DIM), fixed),
                  pl.BlockSpec((ML_DIM, ML_DIM), fixed)],
        out_specs=pl.BlockSpec((1, L, ML_DIM), lambda b, j: (b, j, 0)),
        out_shape=jax.ShapeDtypeStruct((batch, seq, ML_DIM), F32),
        scratch_shapes=[pltpu.VMEM((ML_DIM, ML_DIM), F32), pltpu.VMEM((1, ML_DIM), F32),
                        pltpu.VMEM((1, ML_DIM), F32)],
        compiler_params=_cparams("parallel", "arbitrary"),
        name="mlstm",
    )(p_ml3, p_ml3, conv_w, conv_b, norm_w, expand, ones_bd)


def _outproj_kernel(rw_ref, sw_ref, ml_ref, h_ref, wrw_ref, wsw_ref, wml_ref, lnw_ref, lnb_ref,
                    rwh_ref, rwm_ref, rwl_ref, rb_ref, h1_o, h1p_o, idx_o, gate_o):
    mix = (_dot(rw_ref[...], wrw_ref[...]) + _dot(sw_ref[...], wsw_ref[...]) + _dot(ml_ref[...], wml_ref[...]))
    h1 = _layer_norm(DN_ALPHA * h_ref[...] + mix, lnw_ref[...], lnb_ref[...])
    h1_o[...] = h1
    h1p_o[...] = _pack_bf16_pairs(h1)
    xh, xm, xl = _split3(h1)
    wh, wm, wl = rwh_ref[...], rwm_ref[...], rwl_ref[...]
    dd = lambda a, b: jnp.dot(a, b, preferred_element_type=F32)
    logits = (dd(xh, wh) + (dd(xh, wm) + dd(xm, wh)) + (dd(xh, wl) + dd(xm, wm) + dd(xl, wh))) + rb_ref[...]
    lane = lax.broadcasted_iota(jnp.int32, logits.shape, 1).astype(F32)
    vals, idxs = [], []
    cur = logits
    for _ in range(TOP_K):
        mx = jnp.max(cur, axis=-1, keepdims=True)
        ix = jnp.min(jnp.where(cur == mx, lane, float(LANES)), axis=-1, keepdims=True)
        vals.append(mx)
        idxs.append(ix)
        cur = jnp.where(lane == ix, NEG_BIG * 2.0, cur)
    es = [jnp.exp(vv - vals[0]) for vv in vals]
    den = es[0] + es[1] + es[2] + es[3]
    idx_out = jnp.zeros_like(logits)
    gate_out = jnp.zeros_like(logits)
    for kk in range(TOP_K):
        idx_out = jnp.where(lane == float(kk), idxs[kk], idx_out)
        gate_out = jnp.where(lane == float(kk), es[kk] / den, gate_out)
    idx_o[...] = idx_out.astype(jnp.int32)
    gate_o[...] = gate_out


def _out_proj(rw_out, sw_out, ml_out, h2d, w_rw, w_sw, w_ml, ln_w, ln_b, rw_parts, rb):
    m = h2d.shape[0]
    tm = ROW_TILE
    row = lambda i: (i, 0)
    fixed = lambda i: (0, 0)
    full = lambda a: pl.BlockSpec(a.shape, fixed)
    return pl.pallas_call(
        _outproj_kernel,
        grid=(m // tm,),
        in_specs=[pl.BlockSpec((tm, RW_DIM), row), pl.BlockSpec((tm, SW_DIM), row), pl.BlockSpec((tm, ML_DIM), row),
                  pl.BlockSpec((tm, D_MODEL), row), full(w_rw), full(w_sw), full(w_ml), full(ln_w), full(ln_b),
                  full(rw_parts[0]), full(rw_parts[1]), full(rw_parts[2]), full(rb)],
        out_specs=[pl.BlockSpec((tm, D_MODEL), row), pl.BlockSpec((tm, D_MODEL // 2), row),
                   pl.BlockSpec((tm, LANES), row), pl.BlockSpec((tm, LANES), row)],
        out_shape=[jax.ShapeDtypeStruct((m, D_MODEL), F32), jax.ShapeDtypeStruct((m, D_MODEL // 2), jnp.uint32),
                   jax.ShapeDtypeStruct((m, LANES), jnp.int32), jax.ShapeDtypeStruct((m, LANES), F32)],
        compiler_params=_cparams("parallel"),
        name="out_proj_router",
    )(rw_out, sw_out, ml_out, h2d, w_rw, w_sw, w_ml, ln_w, ln_b, *rw_parts, rb)


def _expert_kernel(n_blocks, layer, be_ref, tokc_ref, tokn_ref, slot_ref, h_hbm, wup_ref, bg_ref, bl_ref,
                   wdn_ref, bd_ref, perm_ref, z_hbm, x0, x1, y0, y1, wg_s, wl_s, wd_s, gsem, ssem):
    del layer
    i = pl.program_id(0)
    bm = MOE_BLOCK

    def gather_row(tok, xdst, sem, r):
        return pltpu.make_async_copy(h_hbm.at[pl.ds(tok, 1)], xdst.at[pl.ds(r, 1)], sem)

    def scatter_row(dst, ysrc, sem, r):
        return pltpu.make_async_copy(ysrc.at[pl.ds(r, 1)], z_hbm.at[pl.ds(dst, 1)], sem)

    def wait_gather(xdst, sem):
        pltpu.make_async_copy(h_hbm.at[pl.ds(0, bm)], xdst, sem).wait()

    def wait_scatter(ysrc, sem):
        pltpu.make_async_copy(ysrc, z_hbm.at[pl.ds(0, bm)], sem).wait()

    @pl.when(i == 0)
    def _():
        y1[...] = jnp.zeros(y1.shape, F32)

        def prime(r, c):
            gather_row(tokc_ref[r], x0, gsem.at[0], r).start()
            return c

        lax.fori_loop(0, bm, prime, 0, unroll=8)

    blk = jnp.minimum(i, n_blocks - 1)
    e_now = be_ref[blk]
    e_prev = be_ref[jnp.maximum(blk - 1, 0)]

    @pl.when((i == 0) | (e_now != e_prev))
    def _():
        perm = perm_ref[...]
        for c in range(D_EXPERT // LANES):
            t = jnp.dot(wup_ref[0, 0, :, c * 2 * LANES:(c + 1) * 2 * LANES].astype(BF16), perm,
                        preferred_element_type=F32)
            wg_s[:, c * LANES:(c + 1) * LANES] = t[:, 0:LANES].astype(BF16)
            wl_s[:, c * LANES:(c + 1) * LANES] = t[:, LANES:2 * LANES].astype(BF16)
        wd_s[...] = wdn_ref[0, 0].astype(BF16)

    def step(x_cur, x_nxt, y_cur, y_nxt, g_cur, g_nxt, s_cur, s_nxt):
        wait_gather(x_cur, g_cur)

        @pl.when(i > 0)
        def _():
            wait_scatter(y_cur, s_cur)

        for r in range(bm):
            gather_row(tokn_ref[r], x_nxt, g_nxt, r).start(priority=r % 2)
            scatter_row(slot_ref[r], y_nxt, s_nxt, r).start(priority=(r + 1) % 2)
        x = _unpack_bf16_pairs(x_cur[...])
        hg = jnp.dot(x, wg_s[...], preferred_element_type=F32) + bg_ref[0, 0]
        hl = jnp.dot(x, wl_s[...], preferred_element_type=F32) + bl_ref[0, 0]
        hg = jnp.minimum(hg, SWIGLU_LIMIT)
        hl = jnp.clip(hl, -SWIGLU_LIMIT, SWIGLU_LIMIT)
        act = hg * _sigmoid(SWIGLU_ALPHA * hg) * (hl + 1.0)
        y_cur[...] = jnp.dot(act.astype(BF16), wd_s[...], preferred_element_type=F32) + bd_ref[0, 0]

        @pl.when(i == n_blocks)
        def _():
            wait_gather(x_nxt, g_nxt)
            wait_scatter(y_nxt, s_nxt)

    @pl.when(i % 2 == 0)
    def _():
        step(x0, x1, y0, y1, gsem.at[0], gsem.at[1], ssem.at[0], ssem.at[1])

    @pl.when(i % 2 == 1)
    def _():
        step(x1, x0, y1, y0, gsem.at[1], gsem.at[0], ssem.at[1], ssem.at[0])


def _experts(layer, block_e, tok, slot_ext, h1, w_up, b_glu, b_lin, w_down, b_down, perm, z_rows):
    bm = MOE_BLOCK
    n_blocks = tok.shape[0] // bm
    last = n_blocks - 1
    by_e = lambda i, be: (layer, be[jnp.minimum(i, last)], 0, 0)
    smem = lambda f: pl.BlockSpec((bm,), f, memory_space=pltpu.SMEM)
    grid_spec = pltpu.PrefetchScalarGridSpec(
        num_scalar_prefetch=1,
        grid=(n_blocks + 1,),
        in_specs=[smem(lambda i, be: (jnp.minimum(i, last),)), smem(lambda i, be: (jnp.minimum(i + 1, last),)),
                  smem(lambda i, be: (i,)),
                  pl.BlockSpec(memory_space=pl.ANY),
                  pl.BlockSpec((1, 1, D_MODEL, 2 * D_EXPERT), by_e),
                  pl.BlockSpec((1, 1, 1, D_EXPERT), by_e), pl.BlockSpec((1, 1, 1, D_EXPERT), by_e),
                  pl.BlockSpec((1, 1, D_EXPERT, D_MODEL), by_e), pl.BlockSpec((1, 1, 1, D_MODEL), by_e),
                  pl.BlockSpec((2 * LANES, 2 * LANES), lambda i, be: (0, 0))],
        out_specs=pl.BlockSpec(memory_space=pl.ANY),
        scratch_shapes=[pltpu.VMEM((bm, D_MODEL // 2), jnp.uint32), pltpu.VMEM((bm, D_MODEL // 2), jnp.uint32),
                        pltpu.VMEM((bm, D_MODEL), F32), pltpu.VMEM((bm, D_MODEL), F32),
                        pltpu.VMEM((D_MODEL, D_EXPERT), BF16), pltpu.VMEM((D_MODEL, D_EXPERT), BF16),
                        pltpu.VMEM((D_EXPERT, D_MODEL), BF16),
                        pltpu.SemaphoreType.DMA((2,)), pltpu.SemaphoreType.DMA((2,))],
    )
    return pl.pallas_call(
        functools.partial(_expert_kernel, n_blocks, layer),
        grid_spec=grid_spec,
        out_shape=jax.ShapeDtypeStruct((z_rows, D_MODEL), F32),
        compiler_params=pltpu.CompilerParams(dimension_semantics=("arbitrary",), vmem_limit_bytes=VMEM_LIMIT,
                                             disable_bounds_checks=True),
        name="experts",
    )(block_e, tok, tok, slot_ext, h1, w_up, b_glu, b_lin, w_down, b_down, perm)


def _combine_kernel(z0_ref, z1_ref, z2_ref, z3_ref, gate_ref, h_ref, lnw_ref, lnb_ref, o_ref):
    gates = gate_ref[...]
    y = gates[:, 0:1] * z0_ref[...]
    for kk, z_ref in enumerate((z1_ref, z2_ref, z3_ref), start=1):
        y = y + gates[:, kk:kk + 1] * z_ref[...]
    o_ref[...] = _layer_norm(DN_ALPHA * h_ref[...] + y, lnw_ref[...], lnb_ref[...])


def _combine(z, gates, h1, ln_w, ln_b):
    m = h1.shape[0]
    tm = ROW_TILE
    n_tiles = m // tm
    row = lambda i: (i, 0)
    fixed = lambda i: (0, 0)
    choice = lambda kk: pl.BlockSpec((tm, D_MODEL), lambda i: (kk * n_tiles + i, 0))
    return pl.pallas_call(
        _combine_kernel,
        grid=(n_tiles,),
        in_specs=[choice(0), choice(1), choice(2), choice(3), pl.BlockSpec((tm, LANES), row),
                  pl.BlockSpec((tm, D_MODEL), row), pl.BlockSpec((1, D_MODEL), fixed), pl.BlockSpec((1, D_MODEL), fixed)],
        out_specs=pl.BlockSpec((tm, D_MODEL), row),
        out_shape=jax.ShapeDtypeStruct((m, D_MODEL), F32),
        compiler_params=_cparams("parallel"),
        name="combine_ln",
    )(z, z, z, z, gates, h1, ln_w, ln_b)


def _routing_tables(top_idx):
    m = top_idx.shape[0]
    n_assign = m * TOP_K
    bm = MOE_BLOCK
    n_blocks = -(-(n_assign + N_EXPERTS * (bm - 1)) // bm)
    p_rows = n_blocks * bm
    e_flat = top_idx.reshape(n_assign)
    order = jnp.argsort(e_flat, stable=True).astype(jnp.int32)
    experts = jnp.arange(N_EXPERTS, dtype=jnp.int32)
    counts = jnp.sum((e_flat[:, None] == experts[None, :]).astype(jnp.int32), axis=0)
    padded = (counts + bm - 1) // bm * bm
    start = jnp.cumsum(counts) - counts
    pend = jnp.cumsum(padded)
    pstart = pend - padded
    block_start = jnp.arange(n_blocks, dtype=jnp.int32) * bm
    block_e = jnp.minimum(jnp.sum((pend[None, :] <= block_start[:, None]).astype(jnp.int32), axis=1), N_EXPERTS - 1)
    per_row = lambda per_expert: jnp.repeat(per_expert[block_e], bm)
    row_pstart, row_start, row_count = per_row(pstart), per_row(start), per_row(counts)
    pos = jnp.arange(p_rows, dtype=jnp.int32) - row_pstart
    is_pad = pos >= row_count
    src_assign = order[jnp.clip(row_start + pos, 0, n_assign - 1)]
    tok = jnp.where(is_pad, 0, src_assign // TOP_K)
    pad_rank = (row_pstart - row_start) + (pos - row_count)
    slot = jnp.where(is_pad, n_assign + pad_rank, (src_assign % TOP_K) * m + src_assign // TOP_K)
    slot_ext = jnp.concatenate([p_rows + jnp.arange(bm, dtype=jnp.int32), slot])
    return tok, slot_ext, block_e, p_rows + bm


def _block_ones(n):
    g = np.arange(n) // HEAD_DIM
    return jnp.asarray(g[:, None] == g[None, :], BF16)


def _sw_head_order():
    group = SW_HEADS // SW_KV_HEADS
    heads = [kv * group + g for g in range(group) for kv in range(SW_KV_HEADS)]
    return np.concatenate([np.arange(h * HEAD_DIM, (h + 1) * HEAD_DIM) for h in heads])


def _rope_tables(seq):
    half = HEAD_DIM // 2
    inv = ROPE_THETA ** (-jnp.arange(half, dtype=F32) / half)
    ang = jnp.arange(seq, dtype=F32)[:, None] * inv[None, :]
    cos, sin = jnp.cos(ang), jnp.sin(ang)
    reps = LANES // HEAD_DIM
    cos_t = jnp.tile(jnp.concatenate([cos, cos], axis=-1), (1, reps))
    sin_t = jnp.tile(jnp.concatenate([-sin, sin], axis=-1), (1, reps))
    return cos_t, sin_t


def kernel(x, w_in, b_in, rw_shift_mu, rw_w_up, rw_w0, rw_a_up, rw_a0, rw_g_up, rw_k_k, rw_k_a, rw_r_k, rw_ln_w, rw_ln_b, sw_sinks, ml_conv_w, ml_conv_b, ml_norm_w, w_out, ln1_w, ln1_b, router_w, router_b, exp_w_up, exp_b_up, exp_w_down, exp_b_down, ln2_w, ln2_b):
    batch, seq, d_model = x.shape
    assert d_model == D_MODEL and seq % ROW_TILE == 0 and seq % ML_CHUNK == 0 and seq % WINDOW == 0
    m = batch * seq
    depth = w_in.shape[0]
    row2 = lambda t: t.reshape(1, -1)

    sw_perm = _sw_head_order()
    in_cols = np.concatenate([np.arange(RW_COLS), RW_COLS + sw_perm, np.arange(RW_COLS + SW_DIM, C_IN)])
    ones_rw = _block_ones(RW_DIM)
    ones_ml = _block_ones(ML_DIM)
    cos_t, sin_t = _rope_tables(seq)
    src = np.arange(LANES)[:, None]
    dst = np.arange(2 * ML_DIM)[None, :]
    expand = jnp.asarray(src == (dst // ML_DIM) * ML_HEADS + (dst % ML_DIM) // HEAD_DIM, BF16)
    pj = np.arange(2 * LANES)[:, None]
    pc = np.arange(2 * LANES)[None, :]
    perm = jnp.asarray(pj == 2 * (pc % LANES) + pc // LANES, BF16)
    n_layers = exp_b_up.shape[0]
    b_glu = exp_b_up[:, :, 0::2].reshape(n_layers, N_EXPERTS, 1, D_EXPERT)
    b_lin = exp_b_up[:, :, 1::2].reshape(n_layers, N_EXPERTS, 1, D_EXPERT)
    b_dn = exp_b_down.reshape(n_layers, N_EXPERTS, 1, D_MODEL)

    h = x.reshape(m, D_MODEL)
    for l in range(depth):
        w_l = jnp.pad(w_in[l][:, in_cols], ((0, 0), (0, C_IN_PAD - C_IN))).astype(BF16)
        b_l = jnp.pad(b_in[l][in_cols], (0, C_IN_PAD - C_IN)).reshape(1, C_IN_PAD)
        zero = jnp.zeros((RW_LORA_W, RW_DIM), F32)
        wcomb = jnp.concatenate([jnp.concatenate([rw_w_up[l], zero], axis=1),
                                 jnp.concatenate([zero, rw_a_up[l]], axis=1)], axis=0).astype(BF16)
        w_o = w_out[l]
        w_o_rw = w_o[0:RW_DIM].astype(BF16)
        w_o_sw = w_o[RW_DIM + sw_perm].astype(BF16)
        w_o_ml = w_o[RW_DIM + SW_DIM:].astype(BF16)
        rw_pad = jnp.pad(router_w[l], ((0, 0), (0, LANES - N_EXPERTS)))
        rw_hi = rw_pad.astype(BF16)
        rw_r1 = rw_pad - rw_hi.astype(F32)
        rw_mid = rw_r1.astype(BF16)
        rw_lo = (rw_r1 - rw_mid.astype(F32)).astype(BF16)
        rb_pad = jnp.pad(router_b[l], (0, LANES - N_EXPERTS), constant_values=NEG_BIG).reshape(1, LANES)

        p_rw, p_sw, p_ml = _in_proj(h, w_l, b_l)
        r, lw, k, v, a, b, g = _rw_prep(p_rw, seq, row2(rw_shift_mu[l]), wcomb, rw_g_up[l].astype(BF16),
                                        row2(rw_w0[l]), row2(rw_a0[l]), row2(rw_k_k[l]), row2(rw_k_a[l]), ones_rw)
        m_mat, yh, g_mat, y0 = _rw_chunks(r, lw, k, v, a, b)
        y = _rw_scan(batch, seq, m_mat, yh, g_mat, y0).reshape(m, RW_DIM)
        rw_out = _rw_post(y, r, k, v, g, row2(rw_r_k[l]), row2(rw_ln_w[l]), row2(rw_ln_b[l]), ones_rw)
        sw_out = _swa(p_sw.reshape(batch, seq, SW_COLS), sw_sinks[l], cos_t, sin_t).reshape(m, SW_DIM)
        ml_out = _mlstm(p_ml.reshape(batch, seq, ML_COLS_PAD), ml_conv_w[l], row2(ml_conv_b[l]),
                        row2(ml_norm_w[l]), expand, ones_ml).reshape(m, ML_DIM)
        h1, h1_packed, idx_pad, gates = _out_proj(rw_out, sw_out, ml_out, h, w_o_rw, w_o_sw, w_o_ml, row2(ln1_w[l]),
                                                 row2(ln1_b[l]), (rw_hi, rw_mid, rw_lo), rb_pad)

        tok, slot_ext, block_e, z_rows = _routing_tables(idx_pad[:, 0:TOP_K])
        z = _experts(l, block_e, tok, slot_ext, h1_packed, exp_w_up, b_glu, b_lin, exp_w_down, b_dn, perm, z_rows)
        h = _combine(z, gates, h1, row2(ln2_w[l]), row2(ln2_b[l]))
    return h.reshape(batch, seq, D_MODEL)
```

```python
import functools

import numpy as np
import jax
import jax.numpy as jnp
from jax import lax
from jax.experimental import pallas as pl
from jax.experimental.pallas import tpu as pltpu

F32 = jnp.float32
BF16 = jnp.bfloat16

D_MODEL = 1024
HEAD_DIM = 64
RW_HEADS = 6
SW_HEADS = 6
SW_KV_HEADS = 2
ML_HEADS = 4
RW_DIM = RW_HEADS * HEAD_DIM
SW_DIM = SW_HEADS * HEAD_DIM
SW_KV_DIM = SW_KV_HEADS * HEAD_DIM
ML_DIM = ML_HEADS * HEAD_DIM
RW_LORA_W = 64
RW_LORA_A = 64
RW_LORA_G = 128
RW_COLS = 3 * RW_DIM + RW_LORA_W + RW_LORA_A + RW_LORA_G
SW_COLS = SW_DIM + 2 * SW_KV_DIM
ML_COLS = 4 * ML_DIM + 2 * ML_HEADS
C_IN = RW_COLS + SW_COLS + ML_COLS
WINDOW = 128
ROPE_THETA = 10000.0
CONV_WIDTH = 4
N_EXPERTS = 32
TOP_K = 4
D_EXPERT = 1024
SWIGLU_ALPHA = 1.702
SWIGLU_LIMIT = 7.0
LN_EPS = 1e-5
RW_GN_EPS = 64e-5
ML_GN_EPS = 1e-6
DEPTH = 4
DN_ALPHA = (2 * DEPTH) ** 0.25

LANES = 128
SUBLANES = 8
ML_COLS_PAD = 1152
C_IN_PAD = RW_COLS + SW_COLS + ML_COLS_PAD
ROW_TILE = 512
RW_CHUNK = 64
RW_PAIR = 2 * HEAD_DIM
RW_NPAIR = RW_HEADS // 2
ML_CHUNK = 256
MOE_BLOCK = 512
NEG_BIG = -1e30
VMEM_LIMIT = 56 * 1024 * 1024


def _cparams(*sem):
    return pltpu.CompilerParams(dimension_semantics=sem, vmem_limit_bytes=VMEM_LIMIT)


def _dot(a, b):
    return jnp.dot(a.astype(BF16), b.astype(BF16), preferred_element_type=F32)


def _dot_nt(a, b):
    return lax.dot_general(a.astype(BF16), b.astype(BF16), (((1,), (1,)), ((), ())),
                           preferred_element_type=F32)


def _split3(x):
    hi = x.astype(BF16)
    r1 = x - hi.astype(F32)
    mid = r1.astype(BF16)
    lo = (r1 - mid.astype(F32)).astype(BF16)
    return hi, mid, lo


def _dot_xl(x, w):
    hi, mid, lo = _split3(x)
    return (jnp.dot(hi, w, preferred_element_type=F32) + jnp.dot(mid, w, preferred_element_type=F32)
            + jnp.dot(lo, w, preferred_element_type=F32))


def _dot_lx(w, x):
    hi, mid, lo = _split3(x)
    return (jnp.dot(w, hi, preferred_element_type=F32) + jnp.dot(w, mid, preferred_element_type=F32)
            + jnp.dot(w, lo, preferred_element_type=F32))


def _softplus(z):
    return jnp.maximum(z, 0.0) + jnp.log(1.0 + jnp.exp(-jnp.abs(z)))


def _sigmoid(z):
    return 1.0 / (1.0 + jnp.exp(-z))


def _shift_rows(x, prev8, j):
    cat = jnp.concatenate([prev8, x], axis=0)
    return pltpu.roll(cat, j, 0)[SUBLANES:, :]


def _layer_norm(x, w, b):
    mu = jnp.mean(x, axis=-1, keepdims=True)
    d = x - mu
    var = jnp.mean(d * d, axis=-1, keepdims=True)
    return d * lax.rsqrt(var + LN_EPS) * w + b


def _pack_bf16_pairs(x):
    c = x.shape[1] // 2
    bits = pltpu.bitcast(x.astype(BF16).astype(F32), jnp.uint32)
    return lax.shift_right_logical(bits[:, 0:c], jnp.uint32(16)) | (bits[:, c:2 * c] & jnp.uint32(0xFFFF0000))


def _unpack_bf16_pairs(w):
    lo = pltpu.bitcast(lax.shift_left(w, jnp.uint32(16)), F32)
    hi = pltpu.bitcast(w & jnp.uint32(0xFFFF0000), F32)
    return jnp.concatenate([lo, hi], axis=1).astype(BF16)


def _head_norm(y, ones_bd, eps):
    inv = 1.0 / HEAD_DIM
    mu = _dot_xl(y, ones_bd) * inv
    d = y - mu
    var = _dot_xl(d * d, ones_bd) * inv
    return d * lax.rsqrt(var + eps)


def _inproj_kernel(x_ref, w_ref, b_ref, rw_ref, sw_ref, ml_ref):
    x = x_ref[...].astype(BF16)
    c1, c2 = RW_COLS, RW_COLS + SW_COLS
    rw_ref[...] = jnp.dot(x, w_ref[:, 0:c1], preferred_element_type=F32) + b_ref[:, 0:c1]
    sw_ref[...] = jnp.dot(x, w_ref[:, c1:c2], preferred_element_type=F32) + b_ref[:, c1:c2]
    ml_ref[...] = jnp.dot(x, w_ref[:, c2:C_IN_PAD], preferred_element_type=F32) + b_ref[:, c2:C_IN_PAD]


def _in_proj(h2d, w, b):
    m = h2d.shape[0]
    tm = ROW_TILE
    row = lambda i: (i, 0)
    fixed = lambda i: (0, 0)
    return pl.pallas_call(
        _inproj_kernel,
        grid=(m // tm,),
        in_specs=[pl.BlockSpec((tm, D_MODEL), row), pl.BlockSpec((D_MODEL, C_IN_PAD), fixed),
                  pl.BlockSpec((1, C_IN_PAD), fixed)],
        out_specs=[pl.BlockSpec((tm, RW_COLS), row), pl.BlockSpec((tm, SW_COLS), row),
                   pl.BlockSpec((tm, ML_COLS_PAD), row)],
        out_shape=[jax.ShapeDtypeStruct((m, RW_COLS), F32), jax.ShapeDtypeStruct((m, SW_COLS), F32),
                   jax.ShapeDtypeStruct((m, ML_COLS_PAD), F32)],
        compiler_params=_cparams("parallel"),
        name="in_proj",
    )(h2d, w, b)


def _rw_prep_kernel(tiles_per_seq, p_ref, pprev_ref, mu_ref, wcomb_ref, gup_ref, w0_ref, a0_ref, kk_ref,
                    ka_ref, ones_ref, r_o, lw_o, k_o, v_o, a_o, b_o, g_o):
    i = pl.program_id(0)
    x = p_ref[...]
    first = (i % tiles_per_seq) == 0
    prev8 = jnp.where(first, 0.0, pprev_ref[...])
    xs = x + (_shift_rows(x, prev8, 1) - x) * mu_ref[...]
    d = RW_DIM
    r = xs[:, 0:d]
    k = xs[:, d:2 * d]
    v = xs[:, 2 * d:3 * d]
    slab = xs[:, 3 * d:3 * d + LANES]
    lane = lax.broadcasted_iota(jnp.int32, slab.shape, 1)
    slab = jnp.where(lane < RW_LORA_W, jnp.tanh(slab), slab)
    wa = _dot(slab, wcomb_ref[...])
    w_pre = w0_ref[...] + wa[:, 0:d]
    a_pre = a0_ref[...] + wa[:, d:2 * d]
    w_log = -_softplus(-w_pre) - 0.5
    log_decay = -jnp.exp(w_log)
    a_sig = _sigmoid(a_pre)
    g = _dot(_sigmoid(xs[:, 3 * d + LANES:3 * d + 2 * LANES]), gup_ref[...])
    kk = k * kk_ref[...]
    ss = _dot_xl(kk * kk, ones_ref[...])
    kk = kk / jnp.maximum(jnp.sqrt(ss), 1e-12)
    r_o[...] = r
    lw_o[...] = log_decay
    k_o[...] = k * (1.0 + (a_sig - 1.0) * ka_ref[...])
    v_o[...] = v
    a_o[...] = -kk
    b_o[...] = kk * a_sig
    g_o[...] = g


def _rw_prep(p_rw, seq, mu, wcomb, gup, w0, a0, k_k, k_a, ones_bd):
    m = p_rw.shape[0]
    tm = ROW_TILE
    row = lambda i: (i, 0)
    fixed = lambda i: (0, 0)
    prev = lambda i: (jnp.maximum(i * (tm // SUBLANES) - 1, 0), 0)
    vec = pl.BlockSpec((1, RW_DIM), fixed)
    out = jax.ShapeDtypeStruct((m, RW_DIM), F32)
    return pl.pallas_call(
        functools.partial(_rw_prep_kernel, seq // tm),
        grid=(m // tm,),
        in_specs=[pl.BlockSpec((tm, RW_COLS), row), pl.BlockSpec((SUBLANES, RW_COLS), prev),
                  pl.BlockSpec((1, RW_COLS), fixed), pl.BlockSpec((LANES, 2 * RW_DIM), fixed),
                  pl.BlockSpec((RW_LORA_G, RW_DIM), fixed), vec, vec, vec, vec,
                  pl.BlockSpec((RW_DIM, RW_DIM), fixed)],
        out_specs=[pl.BlockSpec((tm, RW_DIM), row)] * 7,
        out_shape=[out] * 7,
        compiler_params=_cparams("parallel"),
        name="rw_prep",
    )(p_rw, p_rw, mu, wcomb, gup, w0, a0, k_k, k_a, ones_bd)


def _rw_chunk_kernel(n_chunk, r_ref, lw_ref, k_ref, v_ref, a_ref, b_ref, m_o, yh_o, g_o, y0_o):
    L = RW_CHUNK
    ti = lax.broadcasted_iota(jnp.int32, (L, L), 0)
    si = lax.broadcasted_iota(jnp.int32, (L, L), 1)
    tri = jnp.where(ti >= si, 1.0, 0.0).astype(BF16)
    n2 = 2 * L
    ri = lax.broadcasted_iota(jnp.int32, (n2, n2), 0)
    ci = lax.broadcasted_iota(jnp.int32, (n2, n2), 1)
    rr = jnp.where(ri >= L, ri - L, ri)
    cc = jnp.where(ci >= L, ci - L, ci)
    strict = rr > cc
    incl = rr >= cc
    eye = ri == ci
    eye_f = jnp.where(eye, 1.0, 0.0)
    lane = lax.broadcasted_iota(jnp.int32, (L, RW_PAIR), 1)
    head0 = lane < HEAD_DIM

    def bmm(x, y):
        return jnp.einsum('bmk,bkn->bmn', x.astype(BF16), y.astype(BF16), preferred_element_type=F32)

    def bmm_nt(x, y):
        return jnp.einsum('bmk,bnk->bmn', x.astype(BF16), y.astype(BF16), preferred_element_type=F32)

    def bmm_tn(x, y):
        return bmm(jnp.swapaxes(x, 1, 2), y)

    def stack(x):
        return jnp.concatenate([jnp.where(head0, x, 0.0), jnp.where(head0, 0.0, x)], axis=1)

    def chunked(ref, cols):
        return ref[:, cols].reshape(n_chunk, L, RW_PAIR)

    tri_b = jnp.broadcast_to(tri, (n_chunk, L, L))
    for p in range(RW_NPAIR):
        cols = slice(p * RW_PAIR, (p + 1) * RW_PAIR)
        lw = chunked(lw_ref, cols)
        hi, mid, lo = _split3(lw)
        tsum = lambda part: jnp.einsum('bts,bsn->btn', tri_b, part, preferred_element_type=F32)
        lc = tsum(hi) + tsum(mid) + tsum(lo)
        lcl = lc[:, L - 1:L, :]
        gam = jnp.exp(lc)
        gam_prev = jnp.exp(lc - lw)
        inv = jnp.exp(-lc)
        to_end = jnp.exp(lcl - lc)
        r = chunked(r_ref, cols)
        k = chunked(k_ref, cols)
        v = chunked(v_ref, cols)
        a = chunked(a_ref, cols)
        b = chunked(b_ref, cols)
        x1 = stack(a * gam_prev)
        r1 = stack(r * gam)
        x2 = stack(b * inv)
        k2 = stack(k * inv)
        vs = stack(v)
        bs = stack(b * to_end)
        ks = stack(k * to_end)
        aa = bmm_nt(jnp.concatenate([x1, r1], axis=1), jnp.concatenate([x2, k2], axis=1))
        a_ab = jnp.where(strict, aa[:, 0:n2, 0:n2], 0.0)
        a_ak = jnp.where(strict, aa[:, 0:n2, n2:2 * n2], 0.0)
        a_rb = jnp.where(incl, aa[:, n2:2 * n2, 0:n2], 0.0)
        a_rk = jnp.where(incl, aa[:, n2:2 * n2, n2:2 * n2], 0.0)
        apow = a_ab
        t_inv = eye_f + a_ab
        for _ in range(5):
            apow = bmm(apow, apow)
            t_inv = t_inv + bmm(t_inv, apow)
        akv = bmm(a_ak, vs)
        pw = bmm(t_inv, jnp.concatenate([x1, akv], axis=2))
        pmat = pw[:, :, 0:RW_PAIR]
        wmat = pw[:, :, RW_PAIR:2 * RW_PAIR]
        gl_row = jnp.exp(lcl)
        m_mat = bmm_tn(bs, pmat) + jnp.where(eye, gl_row, 0.0)
        g_mat = bmm_tn(jnp.concatenate([bs, ks], axis=1), jnp.concatenate([wmat, vs], axis=1))
        yy = bmm(a_rb, pw)
        yh = r1 + yy[:, :, 0:RW_PAIR]
        y0 = yy[:, :, RW_PAIR:2 * RW_PAIR] + bmm(a_rk, vs)
        m_o[:, p] = m_mat.astype(BF16)
        yh_o[:, p] = yh.astype(BF16)
        g_o[:, p] = g_mat
        y0_o[:, p] = y0


def _rw_chunks(r, lw, k, v, a, b):
    m = r.shape[0]
    tm = ROW_TILE
    nc = tm // RW_CHUNK
    row = lambda i: (i, 0)
    blk = lambda i: (i, 0, 0, 0)
    n2 = 2 * RW_CHUNK
    n_tot = m // RW_CHUNK
    ospec = pl.BlockSpec((nc, RW_NPAIR, n2, RW_PAIR), blk)
    return pl.pallas_call(
        functools.partial(_rw_chunk_kernel, nc),
        grid=(m // tm,),
        in_specs=[pl.BlockSpec((tm, RW_DIM), row)] * 6,
        out_specs=[ospec] * 4,
        out_shape=[jax.ShapeDtypeStruct((n_tot, RW_NPAIR, n2, RW_PAIR), BF16),
                   jax.ShapeDtypeStruct((n_tot, RW_NPAIR, n2, RW_PAIR), BF16),
                   jax.ShapeDtypeStruct((n_tot, RW_NPAIR, n2, RW_PAIR), F32),
                   jax.ShapeDtypeStruct((n_tot, RW_NPAIR, n2, RW_PAIR), F32)],
        compiler_params=_cparams("parallel"),
        name="rw_chunks",
    )(r, lw, k, v, a, b)


def _rw_scan_kernel(batch, m_ref, yh_ref, g_ref, y0_ref, y_o, h_ref):
    c = pl.program_id(0)

    @pl.when(c == 0)
    def _():
        h_ref[...] = jnp.zeros_like(h_ref)

    L = RW_CHUNK
    for bi in range(batch):
        for p in range(RW_NPAIR):
            hb = h_ref[bi, p].astype(BF16)
            yrows = jnp.dot(yh_ref[bi, 0, p], hb, preferred_element_type=F32) + y0_ref[bi, 0, p]
            y_o[bi, :, p * RW_PAIR:(p + 1) * RW_PAIR] = yrows[0:L, :] + yrows[L:2 * L, :]
            h_ref[bi, p] = jnp.dot(m_ref[bi, 0, p], hb, preferred_element_type=F32) + g_ref[bi, 0, p]


def _rw_scan(batch, seq, m_mat, yh, g_mat, y0):
    nc = seq // RW_CHUNK
    n2 = 2 * RW_CHUNK
    shp = (batch, nc, RW_NPAIR, n2, RW_PAIR)
    args = [t.reshape(shp) for t in (m_mat, yh, g_mat, y0)]
    ispec = pl.BlockSpec((batch, 1, RW_NPAIR, n2, RW_PAIR), lambda c: (0, c, 0, 0, 0))
    return pl.pallas_call(
        functools.partial(_rw_scan_kernel, batch),
        grid=(nc,),
        in_specs=[ispec] * 4,
        out_specs=pl.BlockSpec((batch, RW_CHUNK, RW_DIM), lambda c: (0, c, 0)),
        out_shape=jax.ShapeDtypeStruct((batch, seq, RW_DIM), F32),
        scratch_shapes=[pltpu.VMEM((batch, RW_NPAIR, RW_PAIR, RW_PAIR), F32)],
        compiler_params=_cparams("arbitrary"),
        name="rw_scan",
    )(*args)


def _rw_post_kernel(y_ref, r_ref, k_ref, v_ref, g_ref, rk_ref, lnw_ref, lnb_ref, ones_ref, o_ref):
    ones_bd = ones_ref[...]
    y = _head_norm(y_ref[...], ones_bd, RW_GN_EPS) * lnw_ref[...] + lnb_ref[...]
    bonus = _dot_xl(r_ref[...] * k_ref[...] * rk_ref[...], ones_bd) * v_ref[...]
    o_ref[...] = (y + bonus) * g_ref[...]


def _rw_post(y, r, k, v, g, r_k, ln_w, ln_b, ones_bd):
    m = y.shape[0]
    tm = ROW_TILE
    row = lambda i: (i, 0)
    fixed = lambda i: (0, 0)
    vec = pl.BlockSpec((1, RW_DIM), fixed)
    return pl.pallas_call(
        _rw_post_kernel,
        grid=(m // tm,),
        in_specs=[pl.BlockSpec((tm, RW_DIM), row)] * 5 + [vec, vec, vec, pl.BlockSpec((RW_DIM, RW_DIM), fixed)],
        out_specs=pl.BlockSpec((tm, RW_DIM), row),
        out_shape=jax.ShapeDtypeStruct((m, RW_DIM), F32),
        compiler_params=_cparams("parallel"),
        name="rw_post",
    )(y, r, k, v, g, r_k, ln_w, ln_b, ones_bd)


def _rope(x, cos, sin_signed):
    lane = lax.broadcasted_iota(jnp.int32, x.shape, 1)
    half = HEAD_DIM // 2
    first_half = (lane % HEAD_DIM) < half
    rot = jnp.where(first_half, pltpu.roll(x, LANES - half, 1), pltpu.roll(x, half, 1))
    return x * cos + rot * sin_signed


def _swa_kernel(sink_ref, q_ref, kc_ref, vc_ref, kp_ref, vp_ref, cosc_ref, sinc_ref, cosp_ref, sinp_ref, o_ref):
    j = pl.program_id(1)
    W = WINDOW
    cos_c, sin_c = cosc_ref[...], sinc_ref[...]
    k_cat = jnp.concatenate([_rope(kp_ref[0], cosp_ref[...], sinp_ref[...]), _rope(kc_ref[0], cos_c, sin_c)], axis=0)
    v_cat = jnp.concatenate([vp_ref[0], vc_ref[0]], axis=0).astype(BF16)
    k_cat = k_cat.astype(BF16)
    qi = lax.broadcasted_iota(jnp.int32, (W, 2 * W), 0)
    kj = lax.broadcasted_iota(jnp.int32, (W, 2 * W), 1)
    lo = jnp.where(j > 0, qi, jnp.maximum(qi, W - 1))
    bias = jnp.where((kj > lo) & (kj <= qi + W), 0.0, NEG_BIG)
    lane = lax.broadcasted_iota(jnp.int32, (W, LANES), 1)
    kv0 = lane < HEAD_DIM
    scale = HEAD_DIM ** -0.5
    group = SW_HEADS // SW_KV_HEADS
    for g in range(group):
        q = _rope(q_ref[0, :, g * LANES:(g + 1) * LANES], cos_c, sin_c) * scale
        outs = []
        for kv in range(SW_KV_HEADS):
            qm = jnp.where(kv0, q, 0.0) if kv == 0 else jnp.where(kv0, 0.0, q)
            s = _dot_nt(qm, k_cat) + bias
            sink = sink_ref[kv * group + g]
            mx = jnp.maximum(jnp.max(s, axis=-1, keepdims=True), sink)
            pr = jnp.exp(s - mx)
            den = jnp.sum(pr, axis=-1, keepdims=True) + jnp.exp(sink - mx)
            outs.append(jnp.dot(pr.astype(BF16), v_cat, preferred_element_type=F32) / den)
        o_ref[0, :, g * LANES:(g + 1) * LANES] = jnp.where(kv0, outs[0], outs[1])


def _swa(p_sw3, sinks, cos_t, sin_t):
    batch, seq, _ = p_sw3.shape
    W = WINDOW
    nb = seq // W
    kcol = SW_DIM // LANES
    cur = lambda c: (lambda b, j: (b, j, c))
    prv = lambda c: (lambda b, j: (b, jnp.maximum(j - 1, 0), c))
    tab_c = pl.BlockSpec((W, LANES), lambda b, j: (j, 0))
    tab_p = pl.BlockSpec((W, LANES), lambda b, j: (jnp.maximum(j - 1, 0), 0))
    kv_blk = lambda f: pl.BlockSpec((1, W, LANES), f)
    return pl.pallas_call(
        _swa_kernel,
        grid=(batch, nb),
        in_specs=[pl.BlockSpec(memory_space=pltpu.SMEM),
                  pl.BlockSpec((1, W, SW_DIM), cur(0)),
                  kv_blk(cur(kcol)), kv_blk(cur(kcol + 1)), kv_blk(prv(kcol)), kv_blk(prv(kcol + 1)),
                  tab_c, tab_c, tab_p, tab_p],
        out_specs=pl.BlockSpec((1, W, SW_DIM), cur(0)),
        out_shape=jax.ShapeDtypeStruct((batch, seq, SW_DIM), F32),
        compiler_params=_cparams("parallel", "parallel"),
        name="swa",
    )(sinks, p_sw3, p_sw3, p_sw3, p_sw3, p_sw3, cos_t, sin_t, cos_t, sin_t)


def _mlstm_kernel(p_ref, pprev_ref, cw_ref, cb_ref, nw_ref, expand_ref, ones_ref, o_ref, c_ref, nv_ref, m_ref):
    j = pl.program_id(1)
    L = ML_CHUNK
    d = ML_DIM

    @pl.when(j == 0)
    def _():
        c_ref[...] = jnp.zeros_like(c_ref)
        nv_ref[...] = jnp.zeros_like(nv_ref)
        m_ref[...] = jnp.zeros_like(m_ref)

    x = p_ref[0]
    qk_pre = x[:, 0:2 * d]
    prev8 = jnp.where(j == 0, 0.0, pprev_ref[0][:, 0:2 * d])
    conv = cb_ref[...] + cw_ref[CONV_WIDTH - 1:CONV_WIDTH, :] * qk_pre
    for s in range(1, CONV_WIDTH):
        conv = conv + cw_ref[CONV_WIDTH - 1 - s:CONV_WIDTH - s, :] * _shift_rows(qk_pre, prev8, s)
    qk = conv * _sigmoid(conv)
    q = qk[:, 0:d]
    k = qk[:, d:2 * d] * (HEAD_DIM ** -0.5)
    v = x[:, 2 * d:3 * d]
    o_gate = x[:, 3 * d:4 * d]
    gates = _dot_xl(x[:, 4 * d:4 * d + LANES], expand_ref[...])
    i_full = gates[:, 0:d]
    f_full = gates[:, d:2 * d]
    lf_full = jnp.minimum(f_full, 0.0) - jnp.log(1.0 + jnp.exp(-jnp.abs(f_full)))
    ti = lax.broadcasted_iota(jnp.int32, (L, L), 0)
    si = lax.broadcasted_iota(jnp.int32, (L, L), 1)
    causal = ti >= si
    tri = jnp.where(causal, 1.0, 0.0).astype(BF16)
    bc_full = _dot_lx(tri, lf_full)
    u_t = (i_full - bc_full).T
    lane = lax.broadcasted_iota(jnp.int32, (L, d), 1)
    lane_row = lax.broadcasted_iota(jnp.int32, (1, d), 1)
    q_b = q.astype(BF16)
    k_b = k.astype(BF16)
    m_prev_row = m_ref[...]
    nv_row = nv_ref[...]
    qn = q * nv_row
    num = jnp.zeros((L, d), F32)
    inter_full = jnp.zeros((L, d), F32)
    den_full = jnp.zeros((L, d), F32)
    mnew_row = jnp.zeros((1, d), F32)
    for h in range(ML_HEADS):
        c0 = h * HEAD_DIM
        in_head = (lane >= c0) & (lane < c0 + HEAD_DIM)
        in_head_row = (lane_row >= c0) & (lane_row < c0 + HEAD_DIM)
        bc_col = bc_full[:, c0:c0 + 1]
        dmat = jnp.where(causal, bc_col + u_t[c0:c0 + 1, :], NEG_BIG)
        m_inter = bc_col + m_prev_row[:, c0:c0 + 1]
        m_t = jnp.maximum(m_inter, jnp.max(dmat, axis=-1, keepdims=True))
        inter = jnp.exp(m_inter - m_t)
        e = jnp.exp(dmat - m_t)
        sm = _dot_nt(jnp.where(in_head, q, 0.0), k_b) * e
        num = num + _dot(sm, jnp.where(in_head, v, 0.0))
        nq = jnp.sum(sm, axis=-1, keepdims=True) + inter * jnp.sum(jnp.where(in_head, qn, 0.0), axis=-1, keepdims=True)
        den = jnp.maximum(jnp.abs(nq), jnp.exp(-m_t))
        inter_full = jnp.where(in_head, inter, inter_full)
        den_full = jnp.where(in_head, den, den_full)
        mnew_row = jnp.where(in_head_row, m_t[L - 1:L, :], mnew_row)
    c_mat = c_ref[...]
    num = num + inter_full * jnp.dot(q_b, c_mat.astype(BF16), preferred_element_type=F32)
    hout = num / den_full
    hn = _head_norm(hout, ones_ref[...], ML_GN_EPS) * nw_ref[...]
    o_ref[0] = _sigmoid(o_gate) * hn
    bcl_row = bc_full[L - 1:L, :]
    wst = jnp.exp(bcl_row - bc_full + i_full - mnew_row)
    dec_row = jnp.exp(bcl_row + m_prev_row - mnew_row)
    kw = k * wst
    ri = lax.broadcasted_iota(jnp.int32, (d, d), 0) // HEAD_DIM
    ci = lax.broadcasted_iota(jnp.int32, (d, d), 1) // HEAD_DIM
    c_ref[...] = dec_row * c_mat + jnp.where(ri == ci, _dot(kw.T, v), 0.0)
    nv_ref[...] = dec_row * nv_row + jnp.sum(kw, axis=0, keepdims=True)
    m_ref[...] = mnew_row


def _mlstm(p_ml3, conv_w, conv_b, norm_w, expand, ones_bd):
    batch, seq, _ = p_ml3.shape
    L = ML_CHUNK
    fixed = lambda b, j: (0, 0)
    return pl.pallas_call(
        _mlstm_kernel,
        grid=(batch, seq // L),
        in_specs=[pl.BlockSpec((1, L, ML_COLS_PAD), lambda b, j: (b, j, 0)),
                  pl.BlockSpec((1, SUBLANES, ML_COLS_PAD), lambda b, j: (b, jnp.maximum(j * (L // SUBLANES) - 1, 0), 0)),
                  pl.BlockSpec((CONV_WIDTH, 2 * ML_DIM), fixed), pl.BlockSpec((1, 2 * ML_DIM), fixed),
                  pl.BlockSpec((1, ML_DIM), fixed), pl.BlockSpec((LANES, 2 * ML_DIM), fixed),
                  pl.BlockSpec((ML_DIM, ML_DIM), fixed)],
        out_specs=pl.BlockSpec((1, L, ML_DIM), lambda b, j: (b, j, 0)),
        out_shape=jax.ShapeDtypeStruct((batch, seq, ML_DIM), F32),
        scratch_shapes=[pltpu.VMEM((ML_DIM, ML_DIM), F32), pltpu.VMEM((1, ML_DIM), F32),
                        pltpu.VMEM((1, ML_DIM), F32)],
        compiler_params=_cparams("parallel", "arbitrary"),
        name="mlstm",
    )(p_ml3, p_ml3, conv_w, conv_b, norm_w, expand, ones_bd)


def _outproj_kernel(rw_ref, sw_ref, ml_ref, h_ref, wrw_ref, wsw_ref, wml_ref, lnw_ref, lnb_ref,
                    rwh_ref, rwm_ref, rwl_ref, rb_ref, h1_o, h1p_o, idx_o, gate_o):
    mix = (_dot(rw_ref[...], wrw_ref[...]) + _dot(sw_ref[...], wsw_ref[...]) + _dot(ml_ref[...], wml_ref[...]))
    h1 = _layer_norm(DN_ALPHA * h_ref[...] + mix, lnw_ref[...], lnb_ref[...])
    h1_o[...] = h1
    h1p_o[...] = _pack_bf16_pairs(h1)
    xh, xm, xl = _split3(h1)
    wh, wm, wl = rwh_ref[...], rwm_ref[...], rwl_ref[...]
    dd = lambda a, b: jnp.dot(a, b, preferred_element_type=F32)
    logits = (dd(xh, wh) + (dd(xh, wm) + dd(xm, wh)) + (dd(xh, wl) + dd(xm, wm) + dd(xl, wh))) + rb_ref[...]
    lane = lax.broadcasted_iota(jnp.int32, logits.shape, 1).astype(F32)
    vals, idxs = [], []
    cur = logits
    for _ in range(TOP_K):
        mx = jnp.max(cur, axis=-1, keepdims=True)
        ix = jnp.min(jnp.where(cur == mx, lane, float(LANES)), axis=-1, keepdims=True)
        vals.append(mx)
        idxs.append(ix)
        cur = jnp.where(lane == ix, NEG_BIG * 2.0, cur)
    es = [jnp.exp(vv - vals[0]) for vv in vals]
    den = es[0] + es[1] + es[2] + es[3]
    idx_out = jnp.zeros_like(logits)
    gate_out = jnp.zeros_like(logits)
    for kk in range(TOP_K):
        idx_out = jnp.where(lane == float(kk), idxs[kk], idx_out)
        gate_out = jnp.where(lane == float(kk), es[kk] / den, gate_out)
    idx_o[...] = idx_out.astype(jnp.int32)
    gate_o[...] = gate_out


def _out_proj(rw_out, sw_out, ml_out, h2d, w_rw, w_sw, w_ml, ln_w, ln_b, rw_parts, rb):
    m = h2d.shape[0]
    tm = ROW_TILE
    row = lambda i: (i, 0)
    fixed = lambda i: (0, 0)
    full = lambda a: pl.BlockSpec(a.shape, fixed)
    return pl.pallas_call(
        _outproj_kernel,
        grid=(m // tm,),
        in_specs=[pl.BlockSpec((tm, RW_DIM), row), pl.BlockSpec((tm, SW_DIM), row), pl.BlockSpec((tm, ML_DIM), row),
                  pl.BlockSpec((tm, D_MODEL), row), full(w_rw), full(w_sw), full(w_ml), full(ln_w), full(ln_b),
                  full(rw_parts[0]), full(rw_parts[1]), full(rw_parts[2]), full(rb)],
        out_specs=[pl.BlockSpec((tm, D_MODEL), row), pl.BlockSpec((tm, D_MODEL // 2), row),
                   pl.BlockSpec((tm, LANES), row), pl.BlockSpec((tm, LANES), row)],
        out_shape=[jax.ShapeDtypeStruct((m, D_MODEL), F32), jax.ShapeDtypeStruct((m, D_MODEL // 2), jnp.uint32),
                   jax.ShapeDtypeStruct((m, LANES), jnp.int32), jax.ShapeDtypeStruct((m, LANES), F32)],
        compiler_params=_cparams("parallel"),
        name="out_proj_router",
    )(rw_out, sw_out, ml_out, h2d, w_rw, w_sw, w_ml, ln_w, ln_b, *rw_parts, rb)


def _expert_kernel(n_blocks, layer, be_ref, tokc_ref, tokn_ref, slot_ref, h_hbm, wup_ref, bg_ref, bl_ref,
                   wdn_ref, bd_ref, perm_ref, z_hbm, x0, x1, y0, y1, wg_s, wl_s, wd_s, gsem, ssem):
    del layer
    i = pl.program_id(0)
    bm = MOE_BLOCK

    def gather_row(tok, xdst, sem, r):
        return pltpu.make_async_copy(h_hbm.at[pl.ds(tok, 1)], xdst.at[pl.ds(r, 1)], sem)

    def scatter_row(dst, ysrc, sem, r):
        return pltpu.make_async_copy(ysrc.at[pl.ds(r, 1)], z_hbm.at[pl.ds(dst, 1)], sem)

    def wait_gather(xdst, sem):
        pltpu.make_async_copy(h_hbm.at[pl.ds(0, bm)], xdst, sem).wait()

    def wait_scatter(ysrc, sem):
        pltpu.make_async_copy(ysrc, z_hbm.at[pl.ds(0, bm)], sem).wait()

    @pl.when(i == 0)
    def _():
        y1[...] = jnp.zeros(y1.shape, F32)

        def prime(r, c):
            gather_row(tokc_ref[r], x0, gsem.at[0], r).start()
            return c

        lax.fori_loop(0, bm, prime, 0, unroll=8)

    blk = jnp.minimum(i, n_blocks - 1)
    e_now = be_ref[blk]
    e_prev = be_ref[jnp.maximum(blk - 1, 0)]

    @pl.when((i == 0) | (e_now != e_prev))
    def _():
        perm = perm_ref[...]
        for c in range(D_EXPERT // LANES):
            t = jnp.dot(wup_ref[0, 0, :, c * 2 * LANES:(c + 1) * 2 * LANES].astype(BF16), perm,
                        preferred_element_type=F32)
            wg_s[:, c * LANES:(c + 1) * LANES] = t[:, 0:LANES].astype(BF16)
            wl_s[:, c * LANES:(c + 1) * LANES] = t[:, LANES:2 * LANES].astype(BF16)
        wd_s[...] = wdn_ref[0, 0].astype(BF16)

    def step(x_cur, x_nxt, y_cur, y_nxt, g_cur, g_nxt, s_cur, s_nxt):
        wait_gather(x_cur, g_cur)
        for r in range(bm):
            gather_row(tokn_ref[r], x_nxt, g_nxt, r).start(priority=r % 2)
            scatter_row(slot_ref[r], y_nxt, s_nxt, r).start(priority=(r + 1) % 2)

        @pl.when(i > 0)
        def _():
            wait_scatter(y_cur, s_cur)

        x = _unpack_bf16_pairs(x_cur[...])
        hg = jnp.dot(x, wg_s[...], preferred_element_type=F32) + bg_ref[0, 0]
        hl = jnp.dot(x, wl_s[...], preferred_element_type=F32) + bl_ref[0, 0]
        hg = jnp.minimum(hg, SWIGLU_LIMIT)
        hl = jnp.clip(hl, -SWIGLU_LIMIT, SWIGLU_LIMIT)
        act = hg * _sigmoid(SWIGLU_ALPHA * hg) * (hl + 1.0)
        y_cur[...] = jnp.dot(act.astype(BF16), wd_s[...], preferred_element_type=F32) + bd_ref[0, 0]

        @pl.when(i == n_blocks)
        def _():
            wait_gather(x_nxt, g_nxt)
            wait_scatter(y_nxt, s_nxt)

    @pl.when(i % 2 == 0)
    def _():
        step(x0, x1, y0, y1, gsem.at[0], gsem.at[1], ssem.at[0], ssem.at[1])

    @pl.when(i % 2 == 1)
    def _():
        step(x1, x0, y1, y0, gsem.at[1], gsem.at[0], ssem.at[1], ssem.at[0])


def _experts(layer, block_e, tok, slot_ext, h1, w_up, b_glu, b_lin, w_down, b_down, perm, z_rows):
    bm = MOE_BLOCK
    n_blocks = tok.shape[0] // bm
    last = n_blocks - 1
    by_e = lambda i, be: (layer, be[jnp.minimum(i, last)], 0, 0)
    smem = lambda f: pl.BlockSpec((bm,), f, memory_space=pltpu.SMEM)
    grid_spec = pltpu.PrefetchScalarGridSpec(
        num_scalar_prefetch=1,
        grid=(n_blocks + 1,),
        in_specs=[smem(lambda i, be: (jnp.minimum(i, last),)), smem(lambda i, be: (jnp.minimum(i + 1, last),)),
                  smem(lambda i, be: (i,)),
                  pl.BlockSpec(memory_space=pl.ANY),
                  pl.BlockSpec((1, 1, D_MODEL, 2 * D_EXPERT), by_e),
                  pl.BlockSpec((1, 1, 1, D_EXPERT), by_e), pl.BlockSpec((1, 1, 1, D_EXPERT), by_e),
                  pl.BlockSpec((1, 1, D_EXPERT, D_MODEL), by_e), pl.BlockSpec((1, 1, 1, D_MODEL), by_e),
                  pl.BlockSpec((2 * LANES, 2 * LANES), lambda i, be: (0, 0))],
        out_specs=pl.BlockSpec(memory_space=pl.ANY),
        scratch_shapes=[pltpu.VMEM((bm, D_MODEL // 2), jnp.uint32), pltpu.VMEM((bm, D_MODEL // 2), jnp.uint32),
                        pltpu.VMEM((bm, D_MODEL), F32), pltpu.VMEM((bm, D_MODEL), F32),
                        pltpu.VMEM((D_MODEL, D_EXPERT), BF16), pltpu.VMEM((D_MODEL, D_EXPERT), BF16),
                        pltpu.VMEM((D_EXPERT, D_MODEL), BF16),
                        pltpu.SemaphoreType.DMA((2,)), pltpu.SemaphoreType.DMA((2,))],
    )
    return pl.pallas_call(
        functools.partial(_expert_kernel, n_blocks, layer),
        grid_spec=grid_spec,
        out_shape=jax.ShapeDtypeStruct((z_rows, D_MODEL), F32),
        compiler_params=pltpu.CompilerParams(dimension_semantics=("arbitrary",), vmem_limit_bytes=VMEM_LIMIT,
                                             disable_bounds_checks=True),
        name="experts",
    )(block_e, tok, tok, slot_ext, h1, w_up, b_glu, b_lin, w_down, b_down, perm)


def _combine_kernel(z0_ref, z1_ref, z2_ref, z3_ref, gate_ref, h_ref, lnw_ref, lnb_ref, o_ref):
    gates = gate_ref[...]
    y = gates[:, 0:1] * z0_ref[...]
    for kk, z_ref in enumerate((z1_ref, z2_ref, z3_ref), start=1):
        y = y + gates[:, kk:kk + 1] * z_ref[...]
    o_ref[...] = _layer_norm(DN_ALPHA * h_ref[...] + y, lnw_ref[...], lnb_ref[...])


def _combine(z, gates, h1, ln_w, ln_b):
    m = h1.shape[0]
    tm = ROW_TILE
    n_tiles = m // tm
    row = lambda i: (i, 0)
    fixed = lambda i: (0, 0)
    choice = lambda kk: pl.BlockSpec((tm, D_MODEL), lambda i: (kk * n_tiles + i, 0))
    return pl.pallas_call(
        _combine_kernel,
        grid=(n_tiles,),
        in_specs=[choice(0), choice(1), choice(2), choice(3), pl.BlockSpec((tm, LANES), row),
                  pl.BlockSpec((tm, D_MODEL), row), pl.BlockSpec((1, D_MODEL), fixed), pl.BlockSpec((1, D_MODEL), fixed)],
        out_specs=pl.BlockSpec((tm, D_MODEL), row),
        out_shape=jax.ShapeDtypeStruct((m, D_MODEL), F32),
        compiler_params=_cparams("parallel"),
        name="combine_ln",
    )(z, z, z, z, gates, h1, ln_w, ln_b)


def _routing_tables(top_idx):
    m = top_idx.shape[0]
    n_assign = m * TOP_K
    bm = MOE_BLOCK
    n_blocks = -(-(n_assign + N_EXPERTS * (bm - 1)) // bm)
    p_rows = n_blocks * bm
    e_flat = top_idx.reshape(n_assign)
    order = jnp.argsort(e_flat, stable=True).astype(jnp.int32)
    experts = jnp.arange(N_EXPERTS, dtype=jnp.int32)
    counts = jnp.sum((e_flat[:, None] == experts[None, :]).astype(jnp.int32), axis=0)
    padded = (counts + bm - 1) // bm * bm
    start = jnp.cumsum(counts) - counts
    pend = jnp.cumsum(padded)
    pstart = pend - padded
    block_start = jnp.arange(n_blocks, dtype=jnp.int32) * bm
    block_e = jnp.minimum(jnp.sum((pend[None, :] <= block_start[:, None]).astype(jnp.int32), axis=1), N_EXPERTS - 1)
    per_row = lambda per_expert: jnp.repeat(per_expert[block_e], bm)
    row_pstart, row_start, row_count = per_row(pstart), per_row(start), per_row(counts)
    pos = jnp.arange(p_rows, dtype=jnp.int32) - row_pstart
    is_pad = pos >= row_count
    src_assign = order[jnp.clip(row_start + pos, 0, n_assign - 1)]
    tok = jnp.where(is_pad, 0, src_assign // TOP_K)
    pad_rank = (row_pstart - row_start) + (pos - row_count)
    slot = jnp.where(is_pad, n_assign + pad_rank, (src_assign % TOP_K) * m + src_assign // TOP_K)
    slot_ext = jnp.concatenate([p_rows + jnp.arange(bm, dtype=jnp.int32), slot])
    return tok, slot_ext, block_e, p_rows + bm


def _block_ones(n):
    g = np.arange(n) // HEAD_DIM
    return jnp.asarray(g[:, None] == g[None, :], BF16)


def _sw_head_order():
    group = SW_HEADS // SW_KV_HEADS
    heads = [kv * group + g for g in range(group) for kv in range(SW_KV_HEADS)]
    return np.concatenate([np.arange(h * HEAD_DIM, (h + 1) * HEAD_DIM) for h in heads])


def _rope_tables(seq):
    half = HEAD_DIM // 2
    inv = ROPE_THETA ** (-jnp.arange(half, dtype=F32) / half)
    ang = jnp.arange(seq, dtype=F32)[:, None] * inv[None, :]
    cos, sin = jnp.cos(ang), jnp.sin(ang)
    reps = LANES // HEAD_DIM
    cos_t = jnp.tile(jnp.concatenate([cos, cos], axis=-1), (1, reps))
    sin_t = jnp.tile(jnp.concatenate([-sin, sin], axis=-1), (1, reps))
    return cos_t, sin_t


def kernel(x, w_in, b_in, rw_shift_mu, rw_w_up, rw_w0, rw_a_up, rw_a0, rw_g_up, rw_k_k, rw_k_a, rw_r_k, rw_ln_w, rw_ln_b, sw_sinks, ml_conv_w, ml_conv_b, ml_norm_w, w_out, ln1_w, ln1_b, router_w, router_b, exp_w_up, exp_b_up, exp_w_down, exp_b_down, ln2_w, ln2_b):
    batch, seq, d_model = x.shape
    assert d_model == D_MODEL and seq % ROW_TILE == 0 and seq % ML_CHUNK == 0 and seq % WINDOW == 0
    m = batch * seq
    depth = w_in.shape[0]
    row2 = lambda t: t.reshape(1, -1)

    sw_perm = _sw_head_order()
    in_cols = np.concatenate([np.arange(RW_COLS), RW_COLS + sw_perm, np.arange(RW_COLS + SW_DIM, C_IN)])
    ones_rw = _block_ones(RW_DIM)
    ones_ml = _block_ones(ML_DIM)
    cos_t, sin_t = _rope_tables(seq)
    src = np.arange(LANES)[:, None]
    dst = np.arange(2 * ML_DIM)[None, :]
    expand = jnp.asarray(src == (dst // ML_DIM) * ML_HEADS + (dst % ML_DIM) // HEAD_DIM, BF16)
    pj = np.arange(2 * LANES)[:, None]
    pc = np.arange(2 * LANES)[None, :]
    perm = jnp.asarray(pj == 2 * (pc % LANES) + pc // LANES, BF16)
    n_layers = exp_b_up.shape[0]
    b_glu = exp_b_up[:, :, 0::2].reshape(n_layers, N_EXPERTS, 1, D_EXPERT)
    b_lin = exp_b_up[:, :, 1::2].reshape(n_layers, N_EXPERTS, 1, D_EXPERT)
    b_dn = exp_b_down.reshape(n_layers, N_EXPERTS, 1, D_MODEL)

    h = x.reshape(m, D_MODEL)
    for l in range(depth):
        w_l = jnp.pad(w_in[l][:, in_cols], ((0, 0), (0, C_IN_PAD - C_IN))).astype(BF16)
        b_l = jnp.pad(b_in[l][in_cols], (0, C_IN_PAD - C_IN)).reshape(1, C_IN_PAD)
        zero = jnp.zeros((RW_LORA_W, RW_DIM), F32)
        wcomb = jnp.concatenate([jnp.concatenate([rw_w_up[l], zero], axis=1),
                                 jnp.concatenate([zero, rw_a_up[l]], axis=1)], axis=0).astype(BF16)
        w_o = w_out[l]
        w_o_rw = w_o[0:RW_DIM].astype(BF16)
        w_o_sw = w_o[RW_DIM + sw_perm].astype(BF16)
        w_o_ml = w_o[RW_DIM + SW_DIM:].astype(BF16)
        rw_pad = jnp.pad(router_w[l], ((0, 0), (0, LANES - N_EXPERTS)))
        rw_hi = rw_pad.astype(BF16)
        rw_r1 = rw_pad - rw_hi.astype(F32)
        rw_mid = rw_r1.astype(BF16)
        rw_lo = (rw_r1 - rw_mid.astype(F32)).astype(BF16)
        rb_pad = jnp.pad(router_b[l], (0, LANES - N_EXPERTS), constant_values=NEG_BIG).reshape(1, LANES)

        p_rw, p_sw, p_ml = _in_proj(h, w_l, b_l)
        r, lw, k, v, a, b, g = _rw_prep(p_rw, seq, row2(rw_shift_mu[l]), wcomb, rw_g_up[l].astype(BF16),
                                        row2(rw_w0[l]), row2(rw_a0[l]), row2(rw_k_k[l]), row2(rw_k_a[l]), ones_rw)
        m_mat, yh, g_mat, y0 = _rw_chunks(r, lw, k, v, a, b)
        y = _rw_scan(batch, seq, m_mat, yh, g_mat, y0).reshape(m, RW_DIM)
        rw_out = _rw_post(y, r, k, v, g, row2(rw_r_k[l]), row2(rw_ln_w[l]), row2(rw_ln_b[l]), ones_rw)
        sw_out = _swa(p_sw.reshape(batch, seq, SW_COLS), sw_sinks[l], cos_t, sin_t).reshape(m, SW_DIM)
        ml_out = _mlstm(p_ml.reshape(batch, seq, ML_COLS_PAD), ml_conv_w[l], row2(ml_conv_b[l]),
                        row2(ml_norm_w[l]), expand, ones_ml).reshape(m, ML_DIM)
        h1, h1_packed, idx_pad, gates = _out_proj(rw_out, sw_out, ml_out, h, w_o_rw, w_o_sw, w_o_ml, row2(ln1_w[l]),
                                                 row2(ln1_b[l]), (rw_hi, rw_mid, rw_lo), rb_pad)

        tok, slot_ext, block_e, z_rows = _routing_tables(idx_pad[:, 0:TOP_K])
        z = _experts(l, block_e, tok, slot_ext, h1_packed, exp_w_up, b_glu, b_lin, exp_w_down, b_dn, perm, z_rows)
        h = _combine(z, gates, h1, row2(ln2_w[l]), row2(ln2_b[l]))
    return h.reshape(batch, seq, D_MODEL)
```

```python
import functools

import numpy as np
import jax
import jax.numpy as jnp
from jax import lax
from jax.experimental import pallas as pl
from jax.experimental.pallas import tpu as pltpu

F32 = jnp.float32
BF16 = jnp.bfloat16

D_MODEL = 1024
HEAD_DIM = 64
RW_HEADS = 6
SW_HEADS = 6
SW_KV_HEADS = 2
ML_HEADS = 4
RW_DIM = RW_HEADS * HEAD_DIM
SW_DIM = SW_HEADS * HEAD_DIM
SW_KV_DIM = SW_KV_HEADS * HEAD_DIM
ML_DIM = ML_HEADS * HEAD_DIM
RW_LORA_W = 64
RW_LORA_A = 64
RW_LORA_G = 128
RW_COLS = 3 * RW_DIM + RW_LORA_W + RW_LORA_A + RW_LORA_G
SW_COLS = SW_DIM + 2 * SW_KV_DIM
ML_COLS = 4 * ML_DIM + 2 * ML_HEADS
C_IN = RW_COLS + SW_COLS + ML_COLS
WINDOW = 128
ROPE_THETA = 10000.0
CONV_WIDTH = 4
N_EXPERTS = 32
TOP_K = 4
D_EXPERT = 1024
SWIGLU_ALPHA = 1.702
SWIGLU_LIMIT = 7.0
LN_EPS = 1e-5
RW_GN_EPS = 64e-5
ML_GN_EPS = 1e-6
DEPTH = 4
DN_ALPHA = (2 * DEPTH) ** 0.25

LANES = 128
SUBLANES = 8
ML_COLS_PAD = 1152
C_IN_PAD = RW_COLS + SW_COLS + ML_COLS_PAD
ROW_TILE = 512
RW_CHUNK = 64
RW_PAIR = 2 * HEAD_DIM
RW_NPAIR = RW_HEADS // 2
ML_CHUNK = 256
MOE_BLOCK = 512
NEG_BIG = -1e30
VMEM_LIMIT = 56 * 1024 * 1024


def _cparams(*sem):
    return pltpu.CompilerParams(dimension_semantics=sem, vmem_limit_bytes=VMEM_LIMIT)


def _dot(a, b):
    return jnp.dot(a.astype(BF16), b.astype(BF16), preferred_element_type=F32)


def _dot_nt(a, b):
    return lax.dot_general(a.astype(BF16), b.astype(BF16), (((1,), (1,)), ((), ())),
                           preferred_element_type=F32)


def _split3(x):
    hi = x.astype(BF16)
    r1 = x - hi.astype(F32)
    mid = r1.astype(BF16)
    lo = (r1 - mid.astype(F32)).astype(BF16)
    return hi, mid, lo


def _dot_xl(x, w):
    hi = x.astype(BF16)
    mid = (x - hi.astype(F32)).astype(BF16)
    return jnp.dot(hi, w, preferred_element_type=F32) + jnp.dot(mid, w, preferred_element_type=F32)


def _dot_lx(w, x):
    hi, mid, lo = _split3(x)
    return (jnp.dot(w, hi, preferred_element_type=F32) + jnp.dot(w, mid, preferred_element_type=F32)
            + jnp.dot(w, lo, preferred_element_type=F32))


def _softplus(z):
    return jnp.maximum(z, 0.0) + jnp.log(1.0 + jnp.exp(-jnp.abs(z)))


def _sigmoid(z):
    return 1.0 / (1.0 + jnp.exp(-z))


def _shift_rows(x, prev8, j):
    cat = jnp.concatenate([prev8, x], axis=0)
    return pltpu.roll(cat, j, 0)[SUBLANES:, :]


def _layer_norm(x, w, b):
    mu = jnp.mean(x, axis=-1, keepdims=True)
    d = x - mu
    var = jnp.mean(d * d, axis=-1, keepdims=True)
    return d * lax.rsqrt(var + LN_EPS) * w + b


def _pack_bf16_pairs(x):
    c = x.shape[1] // 2
    bits = pltpu.bitcast(x.astype(BF16).astype(F32), jnp.uint32)
    return lax.shift_right_logical(bits[:, 0:c], jnp.uint32(16)) | (bits[:, c:2 * c] & jnp.uint32(0xFFFF0000))


def _unpack_bf16_pairs(w):
    lo = pltpu.bitcast(lax.shift_left(w, jnp.uint32(16)), F32)
    hi = pltpu.bitcast(w & jnp.uint32(0xFFFF0000), F32)
    return jnp.concatenate([lo, hi], axis=1).astype(BF16)


def _head_norm(y, ones_bd, eps):
    inv = 1.0 / HEAD_DIM
    mu = _dot_xl(y, ones_bd) * inv
    d = y - mu
    var = _dot_xl(d * d, ones_bd) * inv
    return d * lax.rsqrt(var + eps)


def _inproj_kernel(x_ref, w_ref, b_ref, rw_ref, sw_ref, ml_ref):
    x = x_ref[...].astype(BF16)
    c1, c2 = RW_COLS, RW_COLS + SW_COLS
    rw_ref[...] = jnp.dot(x, w_ref[:, 0:c1], preferred_element_type=F32) + b_ref[:, 0:c1]
    sw_ref[...] = jnp.dot(x, w_ref[:, c1:c2], preferred_element_type=F32) + b_ref[:, c1:c2]
    ml_ref[...] = jnp.dot(x, w_ref[:, c2:C_IN_PAD], preferred_element_type=F32) + b_ref[:, c2:C_IN_PAD]


def _in_proj(h2d, w, b):
    m = h2d.shape[0]
    tm = ROW_TILE
    row = lambda i: (i, 0)
    fixed = lambda i: (0, 0)
    return pl.pallas_call(
        _inproj_kernel,
        grid=(m // tm,),
        in_specs=[pl.BlockSpec((tm, D_MODEL), row), pl.BlockSpec((D_MODEL, C_IN_PAD), fixed),
                  pl.BlockSpec((1, C_IN_PAD), fixed)],
        out_specs=[pl.BlockSpec((tm, RW_COLS), row), pl.BlockSpec((tm, SW_COLS), row),
                   pl.BlockSpec((tm, ML_COLS_PAD), row)],
        out_shape=[jax.ShapeDtypeStruct((m, RW_COLS), F32), jax.ShapeDtypeStruct((m, SW_COLS), F32),
                   jax.ShapeDtypeStruct((m, ML_COLS_PAD), F32)],
        compiler_params=_cparams("parallel"),
        name="in_proj",
    )(h2d, w, b)


def _rw_prep_kernel(tiles_per_seq, p_ref, pprev_ref, mu_ref, wcomb_ref, gup_ref, w0_ref, a0_ref, kk_ref,
                    ka_ref, ones_ref, r_o, lw_o, k_o, v_o, a_o, b_o, g_o):
    i = pl.program_id(0)
    x = p_ref[...]
    first = (i % tiles_per_seq) == 0
    prev8 = jnp.where(first, 0.0, pprev_ref[...])
    xs = x + (_shift_rows(x, prev8, 1) - x) * mu_ref[...]
    d = RW_DIM
    r = xs[:, 0:d]
    k = xs[:, d:2 * d]
    v = xs[:, 2 * d:3 * d]
    slab = xs[:, 3 * d:3 * d + LANES]
    lane = lax.broadcasted_iota(jnp.int32, slab.shape, 1)
    slab = jnp.where(lane < RW_LORA_W, jnp.tanh(slab), slab)
    wa = _dot(slab, wcomb_ref[...])
    w_pre = w0_ref[...] + wa[:, 0:d]
    a_pre = a0_ref[...] + wa[:, d:2 * d]
    w_log = -_softplus(-w_pre) - 0.5
    log_decay = -jnp.exp(w_log)
    a_sig = _sigmoid(a_pre)
    g = _dot(_sigmoid(xs[:, 3 * d + LANES:3 * d + 2 * LANES]), gup_ref[...])
    kk = k * kk_ref[...]
    ss = _dot_xl(kk * kk, ones_ref[...])
    kk = kk / jnp.maximum(jnp.sqrt(ss), 1e-12)
    r_o[...] = r
    lw_o[...] = log_decay
    k_o[...] = k * (1.0 + (a_sig - 1.0) * ka_ref[...])
    v_o[...] = v
    a_o[...] = -kk
    b_o[...] = kk * a_sig
    g_o[...] = g


def _rw_prep(p_rw, seq, mu, wcomb, gup, w0, a0, k_k, k_a, ones_bd):
    m = p_rw.shape[0]
    tm = ROW_TILE
    row = lambda i: (i, 0)
    fixed = lambda i: (0, 0)
    prev = lambda i: (jnp.maximum(i * (tm // SUBLANES) - 1, 0), 0)
    vec = pl.BlockSpec((1, RW_DIM), fixed)
    out = jax.ShapeDtypeStruct((m, RW_DIM), F32)
    return pl.pallas_call(
        functools.partial(_rw_prep_kernel, seq // tm),
        grid=(m // tm,),
        in_specs=[pl.BlockSpec((tm, RW_COLS), row), pl.BlockSpec((SUBLANES, RW_COLS), prev),
                  pl.BlockSpec((1, RW_COLS), fixed), pl.BlockSpec((LANES, 2 * RW_DIM), fixed),
                  pl.BlockSpec((RW_LORA_G, RW_DIM), fixed), vec, vec, vec, vec,
                  pl.BlockSpec((RW_DIM, RW_DIM), fixed)],
        out_specs=[pl.BlockSpec((tm, RW_DIM), row)] * 7,
        out_shape=[out] * 7,
        compiler_params=_cparams("parallel"),
        name="rw_prep",
    )(p_rw, p_rw, mu, wcomb, gup, w0, a0, k_k, k_a, ones_bd)


def _rw_chunk_kernel(n_chunk, r_ref, lw_ref, k_ref, v_ref, a_ref, b_ref, m_o, yh_o, g_o, y0_o):
    L = RW_CHUNK
    ti = lax.broadcasted_iota(jnp.int32, (L, L), 0)
    si = lax.broadcasted_iota(jnp.int32, (L, L), 1)
    tri = jnp.where(ti >= si, 1.0, 0.0).astype(BF16)
    n2 = 2 * L
    ri = lax.broadcasted_iota(jnp.int32, (n2, n2), 0)
    ci = lax.broadcasted_iota(jnp.int32, (n2, n2), 1)
    rr = jnp.where(ri >= L, ri - L, ri)
    cc = jnp.where(ci >= L, ci - L, ci)
    strict = rr > cc
    incl = rr >= cc
    eye = ri == ci
    eye_f = jnp.where(eye, 1.0, 0.0)
    lane = lax.broadcasted_iota(jnp.int32, (L, RW_PAIR), 1)
    head0 = lane < HEAD_DIM

    def bmm(x, y):
        return jnp.einsum('bmk,bkn->bmn', x.astype(BF16), y.astype(BF16), preferred_element_type=F32)

    def bmm_nt(x, y):
        return jnp.einsum('bmk,bnk->bmn', x.astype(BF16), y.astype(BF16), preferred_element_type=F32)

    def bmm_tn(x, y):
        return bmm(jnp.swapaxes(x, 1, 2), y)

    def stack(x):
        return jnp.concatenate([jnp.where(head0, x, 0.0), jnp.where(head0, 0.0, x)], axis=1)

    def chunked(ref, cols):
        return ref[:, cols].reshape(n_chunk, L, RW_PAIR)

    tri_b = jnp.broadcast_to(tri, (n_chunk, L, L))
    for p in range(RW_NPAIR):
        cols = slice(p * RW_PAIR, (p + 1) * RW_PAIR)
        lw = chunked(lw_ref, cols)
        hi, mid, lo = _split3(lw)
        tsum = lambda part: jnp.einsum('bts,bsn->btn', tri_b, part, preferred_element_type=F32)
        lc = tsum(hi) + tsum(mid) + tsum(lo)
        lcl = lc[:, L - 1:L, :]
        gam = jnp.exp(lc)
        gam_prev = jnp.exp(lc - lw)
        inv = jnp.exp(-lc)
        to_end = jnp.exp(lcl - lc)
        r = chunked(r_ref, cols)
        k = chunked(k_ref, cols)
        v = chunked(v_ref, cols)
        a = chunked(a_ref, cols)
        b = chunked(b_ref, cols)
        x1 = stack(a * gam_prev)
        r1 = stack(r * gam)
        x2 = stack(b * inv)
        k2 = stack(k * inv)
        vs = stack(v)
        bs = stack(b * to_end)
        ks = stack(k * to_end)
        aa = bmm_nt(jnp.concatenate([x1, r1], axis=1), jnp.concatenate([x2, k2], axis=1))
        a_ab = jnp.where(strict, aa[:, 0:n2, 0:n2], 0.0)
        a_ak = jnp.where(strict, aa[:, 0:n2, n2:2 * n2], 0.0)
        a_rb = jnp.where(incl, aa[:, n2:2 * n2, 0:n2], 0.0)
        a_rk = jnp.where(incl, aa[:, n2:2 * n2, n2:2 * n2], 0.0)
        apow = a_ab
        t_inv = eye_f + a_ab
        for _ in range(5):
            apow = bmm(apow, apow)
            t_inv = t_inv + bmm(t_inv, apow)
        akv = bmm(a_ak, vs)
        pw = bmm(t_inv, jnp.concatenate([x1, akv], axis=2))
        pmat = pw[:, :, 0:RW_PAIR]
        wmat = pw[:, :, RW_PAIR:2 * RW_PAIR]
        gl_row = jnp.exp(lcl)
        m_mat = bmm_tn(bs, pmat) + jnp.where(eye, gl_row, 0.0)
        g_mat = bmm_tn(jnp.concatenate([bs, ks], axis=1), jnp.concatenate([wmat, vs], axis=1))
        yy = bmm(a_rb, pw)
        yh = r1 + yy[:, :, 0:RW_PAIR]
        y0 = yy[:, :, RW_PAIR:2 * RW_PAIR] + bmm(a_rk, vs)
        m_o[:, p] = m_mat.astype(BF16)
        yh_o[:, p] = yh.astype(BF16)
        g_o[:, p] = g_mat
        y0_o[:, p] = y0


def _rw_chunks(r, lw, k, v, a, b):
    m = r.shape[0]
    tm = ROW_TILE
    nc = tm // RW_CHUNK
    row = lambda i: (i, 0)
    blk = lambda i: (i, 0, 0, 0)
    n2 = 2 * RW_CHUNK
    n_tot = m // RW_CHUNK
    ospec = pl.BlockSpec((nc, RW_NPAIR, n2, RW_PAIR), blk)
    return pl.pallas_call(
        functools.partial(_rw_chunk_kernel, nc),
        grid=(m // tm,),
        in_specs=[pl.BlockSpec((tm, RW_DIM), row)] * 6,
        out_specs=[ospec] * 4,
        out_shape=[jax.ShapeDtypeStruct((n_tot, RW_NPAIR, n2, RW_PAIR), BF16),
                   jax.ShapeDtypeStruct((n_tot, RW_NPAIR, n2, RW_PAIR), BF16),
                   jax.ShapeDtypeStruct((n_tot, RW_NPAIR, n2, RW_PAIR), F32),
                   jax.ShapeDtypeStruct((n_tot, RW_NPAIR, n2, RW_PAIR), F32)],
        compiler_params=_cparams("parallel"),
        name="rw_chunks",
    )(r, lw, k, v, a, b)


def _rw_scan_kernel(batch, m_ref, yh_ref, g_ref, y0_ref, y_o, h_ref):
    c = pl.program_id(0)

    @pl.when(c == 0)
    def _():
        h_ref[...] = jnp.zeros_like(h_ref)

    L = RW_CHUNK
    for bi in range(batch):
        for p in range(RW_NPAIR):
            hb = h_ref[bi, p].astype(BF16)
            yrows = jnp.dot(yh_ref[bi, 0, p], hb, preferred_element_type=F32) + y0_ref[bi, 0, p]
            y_o[bi, :, p * RW_PAIR:(p + 1) * RW_PAIR] = yrows[0:L, :] + yrows[L:2 * L, :]
            h_ref[bi, p] = jnp.dot(m_ref[bi, 0, p], hb, preferred_element_type=F32) + g_ref[bi, 0, p]


def _rw_scan(batch, seq, m_mat, yh, g_mat, y0):
    nc = seq // RW_CHUNK
    n2 = 2 * RW_CHUNK
    shp = (batch, nc, RW_NPAIR, n2, RW_PAIR)
    args = [t.reshape(shp) for t in (m_mat, yh, g_mat, y0)]
    ispec = pl.BlockSpec((batch, 1, RW_NPAIR, n2, RW_PAIR), lambda c: (0, c, 0, 0, 0))
    return pl.pallas_call(
        functools.partial(_rw_scan_kernel, batch),
        grid=(nc,),
        in_specs=[ispec] * 4,
        out_specs=pl.BlockSpec((batch, RW_CHUNK, RW_DIM), lambda c: (0, c, 0)),
        out_shape=jax.ShapeDtypeStruct((batch, seq, RW_DIM), F32),
        scratch_shapes=[pltpu.VMEM((batch, RW_NPAIR, RW_PAIR, RW_PAIR), F32)],
        compiler_params=_cparams("arbitrary"),
        name="rw_scan",
    )(*args)


def _rw_post_kernel(y_ref, r_ref, k_ref, v_ref, g_ref, rk_ref, lnw_ref, lnb_ref, ones_ref, o_ref):
    ones_bd = ones_ref[...]
    y = _head_norm(y_ref[...], ones_bd, RW_GN_EPS) * lnw_ref[...] + lnb_ref[...]
    bonus = _dot_xl(r_ref[...] * k_ref[...] * rk_ref[...], ones_bd) * v_ref[...]
    o_ref[...] = (y + bonus) * g_ref[...]


def _rw_post(y, r, k, v, g, r_k, ln_w, ln_b, ones_bd):
    m = y.shape[0]
    tm = ROW_TILE
    row = lambda i: (i, 0)
    fixed = lambda i: (0, 0)
    vec = pl.BlockSpec((1, RW_DIM), fixed)
    return pl.pallas_call(
        _rw_post_kernel,
        grid=(m // tm,),
        in_specs=[pl.BlockSpec((tm, RW_DIM), row)] * 5 + [vec, vec, vec, pl.BlockSpec((RW_DIM, RW_DIM), fixed)],
        out_specs=pl.BlockSpec((tm, RW_DIM), row),
        out_shape=jax.ShapeDtypeStruct((m, RW_DIM), F32),
        compiler_params=_cparams("parallel"),
        name="rw_post",
    )(y, r, k, v, g, r_k, ln_w, ln_b, ones_bd)


def _rope(x, cos, sin_signed):
    lane = lax.broadcasted_iota(jnp.int32, x.shape, 1)
    half = HEAD_DIM // 2
    first_half = (lane % HEAD_DIM) < half
    rot = jnp.where(first_half, pltpu.roll(x, LANES - half, 1), pltpu.roll(x, half, 1))
    return x * cos + rot * sin_signed


def _swa_kernel(sink_ref, q_ref, kc_ref, vc_ref, kp_ref, vp_ref, cosc_ref, sinc_ref, cosp_ref, sinp_ref, o_ref):
    j = pl.program_id(1)
    W = WINDOW
    cos_c, sin_c = cosc_ref[...], sinc_ref[...]
    k_cat = jnp.concatenate([_rope(kp_ref[0], cosp_ref[...], sinp_ref[...]), _rope(kc_ref[0], cos_c, sin_c)], axis=0)
    v_cat = jnp.concatenate([vp_ref[0], vc_ref[0]], axis=0).astype(BF16)
    k_cat = k_cat.astype(BF16)
    qi = lax.broadcasted_iota(jnp.int32, (W, 2 * W), 0)
    kj = lax.broadcasted_iota(jnp.int32, (W, 2 * W), 1)
    lo = jnp.where(j > 0, qi, jnp.maximum(qi, W - 1))
    bias = jnp.where((kj > lo) & (kj <= qi + W), 0.0, NEG_BIG)
    lane = lax.broadcasted_iota(jnp.int32, (W, LANES), 1)
    kv0 = lane < HEAD_DIM
    scale = HEAD_DIM ** -0.5
    group = SW_HEADS // SW_KV_HEADS
    for g in range(group):
        q = _rope(q_ref[0, :, g * LANES:(g + 1) * LANES], cos_c, sin_c) * scale
        outs = []
        for kv in range(SW_KV_HEADS):
            qm = jnp.where(kv0, q, 0.0) if kv == 0 else jnp.where(kv0, 0.0, q)
            s = _dot_nt(qm, k_cat) + bias
            sink = sink_ref[kv * group + g]
            mx = jnp.maximum(jnp.max(s, axis=-1, keepdims=True), sink)
            pr = jnp.exp(s - mx)
            den = jnp.sum(pr, axis=-1, keepdims=True) + jnp.exp(sink - mx)
            outs.append(jnp.dot(pr.astype(BF16), v_cat, preferred_element_type=F32) / den)
        o_ref[0, :, g * LANES:(g + 1) * LANES] = jnp.where(kv0, outs[0], outs[1])


def _swa(p_sw3, sinks, cos_t, sin_t):
    batch, seq, _ = p_sw3.shape
    W = WINDOW
    nb = seq // W
    kcol = SW_DIM // LANES
    cur = lambda c: (lambda b, j: (b, j, c))
    prv = lambda c: (lambda b, j: (b, jnp.maximum(j - 1, 0), c))
    tab_c = pl.BlockSpec((W, LANES), lambda b, j: (j, 0))
    tab_p = pl.BlockSpec((W, LANES), lambda b, j: (jnp.maximum(j - 1, 0), 0))
    kv_blk = lambda f: pl.BlockSpec((1, W, LANES), f)
    return pl.pallas_call(
        _swa_kernel,
        grid=(batch, nb),
        in_specs=[pl.BlockSpec(memory_space=pltpu.SMEM),
                  pl.BlockSpec((1, W, SW_DIM), cur(0)),
                  kv_blk(cur(kcol)), kv_blk(cur(kcol + 1)), kv_blk(prv(kcol)), kv_blk(prv(kcol + 1)),
                  tab_c, tab_c, tab_p, tab_p],
        out_specs=pl.BlockSpec((1, W, SW_DIM), cur(0)),
        out_shape=jax.ShapeDtypeStruct((batch, seq, SW_DIM), F32),
        compiler_params=_cparams("parallel", "parallel"),
        name="swa",
    )(sinks, p_sw3, p_sw3, p_sw3, p_sw3, p_sw3, cos_t, sin_t, cos_t, sin_t)


def _mlstm_kernel(p_ref, pprev_ref, cw_ref, cb_ref, nw_ref, expand_ref, ones_ref, o_ref, c_ref, nv_ref, m_ref):
    j = pl.program_id(1)
    L = ML_CHUNK
    d = ML_DIM

    @pl.when(j == 0)
    def _():
        c_ref[...] = jnp.zeros_like(c_ref)
        nv_ref[...] = jnp.zeros_like(nv_ref)
        m_ref[...] = jnp.zeros_like(m_ref)

    x = p_ref[0]
    qk_pre = x[:, 0:2 * d]
    prev8 = jnp.where(j == 0, 0.0, pprev_ref[0][:, 0:2 * d])
    conv = cb_ref[...] + cw_ref[CONV_WIDTH - 1:CONV_WIDTH, :] * qk_pre
    for s in range(1, CONV_WIDTH):
        conv = conv + cw_ref[CONV_WIDTH - 1 - s:CONV_WIDTH - s, :] * _shift_rows(qk_pre, prev8, s)
    qk = conv * _sigmoid(conv)
    q = qk[:, 0:d]
    k = qk[:, d:2 * d] * (HEAD_DIM ** -0.5)
    v = x[:, 2 * d:3 * d]
    o_gate = x[:, 3 * d:4 * d]
    gates = _dot_xl(x[:, 4 * d:4 * d + LANES], expand_ref[...])
    i_full = gates[:, 0:d]
    f_full = gates[:, d:2 * d]
    lf_full = jnp.minimum(f_full, 0.0) - jnp.log(1.0 + jnp.exp(-jnp.abs(f_full)))
    ti = lax.broadcasted_iota(jnp.int32, (L, L), 0)
    si = lax.broadcasted_iota(jnp.int32, (L, L), 1)
    causal = ti >= si
    tri = jnp.where(causal, 1.0, 0.0).astype(BF16)
    bc_full = _dot_lx(tri, lf_full)
    u_t = (i_full - bc_full).T
    lane = lax.broadcasted_iota(jnp.int32, (L, d), 1)
    lane_row = lax.broadcasted_iota(jnp.int32, (1, d), 1)
    q_b = q.astype(BF16)
    k_b = k.astype(BF16)
    m_prev_row = m_ref[...]
    nv_row = nv_ref[...]
    qn = q * nv_row
    num = jnp.zeros((L, d), F32)
    inter_full = jnp.zeros((L, d), F32)
    den_full = jnp.zeros((L, d), F32)
    mnew_row = jnp.zeros((1, d), F32)
    for h in range(ML_HEADS):
        c0 = h * HEAD_DIM
        in_head = (lane >= c0) & (lane < c0 + HEAD_DIM)
        in_head_row = (lane_row >= c0) & (lane_row < c0 + HEAD_DIM)
        bc_col = bc_full[:, c0:c0 + 1]
        dmat = jnp.where(causal, bc_col + u_t[c0:c0 + 1, :], NEG_BIG)
        m_inter = bc_col + m_prev_row[:, c0:c0 + 1]
        m_t = jnp.maximum(m_inter, jnp.max(dmat, axis=-1, keepdims=True))
        inter = jnp.exp(m_inter - m_t)
        e = jnp.exp(dmat - m_t)
        sm = _dot_nt(jnp.where(in_head, q, 0.0), k_b) * e
        num = num + _dot(sm, jnp.where(in_head, v, 0.0))
        nq = jnp.sum(sm, axis=-1, keepdims=True) + inter * jnp.sum(jnp.where(in_head, qn, 0.0), axis=-1, keepdims=True)
        den = jnp.maximum(jnp.abs(nq), jnp.exp(-m_t))
        inter_full = jnp.where(in_head, inter, inter_full)
        den_full = jnp.where(in_head, den, den_full)
        mnew_row = jnp.where(in_head_row, m_t[L - 1:L, :], mnew_row)
    c_mat = c_ref[...]
    num = num + inter_full * jnp.dot(q_b, c_mat.astype(BF16), preferred_element_type=F32)
    hout = num / den_full
    hn = _head_norm(hout, ones_ref[...], ML_GN_EPS) * nw_ref[...]
    o_ref[0] = _sigmoid(o_gate) * hn
    bcl_row = bc_full[L - 1:L, :]
    wst = jnp.exp(bcl_row - bc_full + i_full - mnew_row)
    dec_row = jnp.exp(bcl_row + m_prev_row - mnew_row)
    kw = k * wst
    ri = lax.broadcasted_iota(jnp.int32, (d, d), 0) // HEAD_DIM
    ci = lax.broadcasted_iota(jnp.int32, (d, d), 1) // HEAD_DIM
    c_ref[...] = dec_row * c_mat + jnp.where(ri == ci, _dot(kw.T, v), 0.0)
    nv_ref[...] = dec_row * nv_row + jnp.sum(kw, axis=0, keepdims=True)
    m_ref[...] = mnew_row


def _mlstm(p_ml3, conv_w, conv_b, norm_w, expand, ones_bd):
    batch, seq, _ = p_ml3.shape
    L = ML_CHUNK
    fixed = lambda b, j: (0, 0)
    return pl.pallas_call(
        _mlstm_kernel,
        grid=(batch, seq // L),
        in_specs=[pl.BlockSpec((1, L, ML_COLS_PAD), lambda b, j: (b, j, 0)),
                  pl.BlockSpec((1, SUBLANES, ML_COLS_PAD), lambda b, j: (b, jnp.maximum(j * (L // SUBLANES) - 1, 0), 0)),
                  pl.BlockSpec((CONV_WIDTH, 2 * ML_DIM), fixed), pl.BlockSpec((1, 2 * ML_DIM), fixed),
                  pl.BlockSpec((1, ML_DIM), fixed), pl.BlockSpec((LANES, 2 * ML_DIM), fixed),
                  pl.BlockSpec((ML_DIM, ML_DIM), fixed)],
        out_specs=pl.BlockSpec((1, L, ML_DIM), lambda b, j: (b, j, 0)),
        out_shape=jax.ShapeDtypeStruct((batch, seq, ML_DIM), F32),
        scratch_shapes=[pltpu.VMEM((ML_DIM, ML_DIM), F32), pltpu.VMEM((1, ML_DIM), F32),
                        pltpu.VMEM((1, ML_DIM), F32)],
        compiler_params=_cparams("parallel", "arbitrary"),
        name="mlstm",
    )(p_ml3, p_ml3, conv_w, conv_b, norm_w, expand, ones_bd)


def _outproj_kernel(rw_ref, sw_ref, ml_ref, h_ref, wrw_ref, wsw_ref, wml_ref, lnw_ref, lnb_ref,
                    rwh_ref, rwm_ref, rwl_ref, rb_ref, h1_o, h1p_o, idx_o, gate_o):
    mix = (_dot(rw_ref[...], wrw_ref[...]) + _dot(sw_ref[...], wsw_ref[...]) + _dot(ml_ref[...], wml_ref[...]))
    h1 = _layer_norm(DN_ALPHA * h_ref[...] + mix, lnw_ref[...], lnb_ref[...])
    h1_o[...] = h1
    h1p_o[...] = _pack_bf16_pairs(h1)
    xh, xm, xl = _split3(h1)
    wh, wm, wl = rwh_ref[...], rwm_ref[...], rwl_ref[...]
    dd = lambda a, b: jnp.dot(a, b, preferred_element_type=F32)
    logits = (dd(xh, wh) + (dd(xh, wm) + dd(xm, wh)) + (dd(xh, wl) + dd(xm, wm) + dd(xl, wh))) + rb_ref[...]
    lane = lax.broadcasted_iota(jnp.int32, logits.shape, 1).astype(F32)
    vals, idxs = [], []
    cur = logits
    for _ in range(TOP_K):
        mx = jnp.max(cur, axis=-1, keepdims=True)
        ix = jnp.min(jnp.where(cur == mx, lane, float(LANES)), axis=-1, keepdims=True)
        vals.append(mx)
        idxs.append(ix)
        cur = jnp.where(lane == ix, NEG_BIG * 2.0, cur)
    es = [jnp.exp(vv - vals[0]) for vv in vals]
    den = es[0] + es[1] + es[2] + es[3]
    idx_out = jnp.zeros_like(logits)
    gate_out = jnp.zeros_like(logits)
    for kk in range(TOP_K):
        idx_out = jnp.where(lane == float(kk), idxs[kk], idx_out)
        gate_out = jnp.where(lane == float(kk), es[kk] / den, gate_out)
    idx_o[...] = idx_out.astype(jnp.int32)
    gate_o[...] = gate_out


def _out_proj(rw_out, sw_out, ml_out, h2d, w_rw, w_sw, w_ml, ln_w, ln_b, rw_parts, rb):
    m = h2d.shape[0]
    tm = ROW_TILE
    row = lambda i: (i, 0)
    fixed = lambda i: (0, 0)
    full = lambda a: pl.BlockSpec(a.shape, fixed)
    return pl.pallas_call(
        _outproj_kernel,
        grid=(m // tm,),
        in_specs=[pl.BlockSpec((tm, RW_DIM), row), pl.BlockSpec((tm, SW_DIM), row), pl.BlockSpec((tm, ML_DIM), row),
                  pl.BlockSpec((tm, D_MODEL), row), full(w_rw), full(w_sw), full(w_ml), full(ln_w), full(ln_b),
                  full(rw_parts[0]), full(rw_parts[1]), full(rw_parts[2]), full(rb)],
        out_specs=[pl.BlockSpec((tm, D_MODEL), row), pl.BlockSpec((tm, D_MODEL // 2), row),
                   pl.BlockSpec((tm, LANES), row), pl.BlockSpec((tm, LANES), row)],
        out_shape=[jax.ShapeDtypeStruct((m, D_MODEL), F32), jax.ShapeDtypeStruct((m, D_MODEL // 2), jnp.uint32),
                   jax.ShapeDtypeStruct((m, LANES), jnp.int32), jax.ShapeDtypeStruct((m, LANES), F32)],
        compiler_params=_cparams("parallel"),
        name="out_proj_router",
    )(rw_out, sw_out, ml_out, h2d, w_rw, w_sw, w_ml, ln_w, ln_b, *rw_parts, rb)


def _expert_kernel(n_blocks, layer, be_ref, tokc_ref, tokn_ref, slot_ref, h_hbm, wup_ref, bg_ref, bl_ref,
                   wdn_ref, bd_ref, perm_ref, z_hbm, x0, x1, y0, y1, wg_s, wl_s, wd_s, gsem, ssem):
    del layer
    i = pl.program_id(0)
    bm = MOE_BLOCK

    def gather_row(tok, xdst, sem, r):
        return pltpu.make_async_copy(h_hbm.at[pl.ds(tok, 1)], xdst.at[pl.ds(r, 1)], sem)

    def scatter_row(dst, ysrc, sem, r):
        return pltpu.make_async_copy(ysrc.at[pl.ds(r, 1)], z_hbm.at[pl.ds(dst, 1)], sem)

    def wait_gather(xdst, sem):
        pltpu.make_async_copy(h_hbm.at[pl.ds(0, bm)], xdst, sem).wait()

    def wait_scatter(ysrc, sem):
        pltpu.make_async_copy(ysrc, z_hbm.at[pl.ds(0, bm)], sem).wait()

    @pl.when(i == 0)
    def _():
        y1[...] = jnp.zeros(y1.shape, F32)

        def prime(r, c):
            gather_row(tokc_ref[r], x0, gsem.at[0], r).start()
            return c

        lax.fori_loop(0, bm, prime, 0, unroll=8)

    n_used = be_ref[n_blocks]
    live = i <= n_used
    blk = jnp.minimum(i, n_blocks - 1)
    e_now = be_ref[blk]
    e_prev = be_ref[jnp.maximum(blk - 1, 0)]

    @pl.when(((i == 0) | (e_now != e_prev)) & live)
    def _():
        perm = perm_ref[...]
        for c in range(D_EXPERT // LANES):
            t = jnp.dot(wup_ref[0, 0, :, c * 2 * LANES:(c + 1) * 2 * LANES].astype(BF16), perm,
                        preferred_element_type=F32)
            wg_s[:, c * LANES:(c + 1) * LANES] = t[:, 0:LANES].astype(BF16)
            wl_s[:, c * LANES:(c + 1) * LANES] = t[:, LANES:2 * LANES].astype(BF16)
        wd_s[...] = wdn_ref[0, 0].astype(BF16)

    def step(x_cur, x_nxt, y_cur, y_nxt, g_cur, g_nxt, s_cur, s_nxt):
        wait_gather(x_cur, g_cur)
        for r in range(bm):
            gather_row(tokn_ref[r], x_nxt, g_nxt, r).start(priority=r % 2)
            scatter_row(slot_ref[r], y_nxt, s_nxt, r).start(priority=(r + 1) % 2)

        @pl.when(i > 0)
        def _():
            wait_scatter(y_cur, s_cur)

        x = _unpack_bf16_pairs(x_cur[...])
        hg = jnp.dot(x, wg_s[...], preferred_element_type=F32) + bg_ref[0, 0]
        hl = jnp.dot(x, wl_s[...], preferred_element_type=F32) + bl_ref[0, 0]
        hg = jnp.minimum(hg, SWIGLU_LIMIT)
        hl = jnp.clip(hl, -SWIGLU_LIMIT, SWIGLU_LIMIT)
        act = hg * _sigmoid(SWIGLU_ALPHA * hg) * (hl + 1.0)
        y_cur[...] = jnp.dot(act.astype(BF16), wd_s[...], preferred_element_type=F32) + bd_ref[0, 0]

        @pl.when(i == n_used)
        def _():
            wait_gather(x_nxt, g_nxt)
            wait_scatter(y_nxt, s_nxt)

    @pl.when(i > n_used)
    def _():
        @pl.when(i == n_used + 1)
        def _():
            y0[...] = jnp.zeros(y0.shape, F32)

        fill = pltpu.make_async_copy(y0, z_hbm.at[pl.ds(pl.multiple_of((i - 1) * bm, bm), bm)], ssem.at[0])
        fill.start()
        fill.wait()

    @pl.when((i % 2 == 0) & live)
    def _():
        step(x0, x1, y0, y1, gsem.at[0], gsem.at[1], ssem.at[0], ssem.at[1])

    @pl.when((i % 2 == 1) & live)
    def _():
        step(x1, x0, y1, y0, gsem.at[1], gsem.at[0], ssem.at[1], ssem.at[0])


def _experts(layer, block_e, tok, slot_ext, h1, w_up, b_glu, b_lin, w_down, b_down, perm, z_rows):
    bm = MOE_BLOCK
    n_blocks = tok.shape[0] // bm
    last = n_blocks - 1
    by_e = lambda i, be: (layer, be[jnp.minimum(i, last)], 0, 0)
    smem = lambda f: pl.BlockSpec((bm,), f, memory_space=pltpu.SMEM)
    grid_spec = pltpu.PrefetchScalarGridSpec(
        num_scalar_prefetch=1,
        grid=(n_blocks + 1,),
        in_specs=[smem(lambda i, be: (jnp.minimum(i, last),)), smem(lambda i, be: (jnp.minimum(i + 1, last),)),
                  smem(lambda i, be: (i,)),
                  pl.BlockSpec(memory_space=pl.ANY),
                  pl.BlockSpec((1, 1, D_MODEL, 2 * D_EXPERT), by_e),
                  pl.BlockSpec((1, 1, 1, D_EXPERT), by_e), pl.BlockSpec((1, 1, 1, D_EXPERT), by_e),
                  pl.BlockSpec((1, 1, D_EXPERT, D_MODEL), by_e), pl.BlockSpec((1, 1, 1, D_MODEL), by_e),
                  pl.BlockSpec((2 * LANES, 2 * LANES), lambda i, be: (0, 0))],
        out_specs=pl.BlockSpec(memory_space=pl.ANY),
        scratch_shapes=[pltpu.VMEM((bm, D_MODEL // 2), jnp.uint32), pltpu.VMEM((bm, D_MODEL // 2), jnp.uint32),
                        pltpu.VMEM((bm, D_MODEL), F32), pltpu.VMEM((bm, D_MODEL), F32),
                        pltpu.VMEM((D_MODEL, D_EXPERT), BF16), pltpu.VMEM((D_MODEL, D_EXPERT), BF16),
                        pltpu.VMEM((D_EXPERT, D_MODEL), BF16),
                        pltpu.SemaphoreType.DMA((2,)), pltpu.SemaphoreType.DMA((2,))],
    )
    return pl.pallas_call(
        functools.partial(_expert_kernel, n_blocks, layer),
        grid_spec=grid_spec,
        out_shape=jax.ShapeDtypeStruct((z_rows, D_MODEL), F32),
        compiler_params=pltpu.CompilerParams(dimension_semantics=("arbitrary",), vmem_limit_bytes=VMEM_LIMIT,
                                             disable_bounds_checks=True),
        name="experts",
    )(block_e, tok, tok, slot_ext, h1, w_up, b_glu, b_lin, w_down, b_down, perm)


def _combine_kernel(z0_ref, z1_ref, z2_ref, z3_ref, gate_ref, h_ref, lnw_ref, lnb_ref, o_ref):
    gates = gate_ref[...]
    y = gates[:, 0:1] * z0_ref[...]
    for kk, z_ref in enumerate((z1_ref, z2_ref, z3_ref), start=1):
        y = y + gates[:, kk:kk + 1] * z_ref[...]
    o_ref[...] = _layer_norm(DN_ALPHA * h_ref[...] + y, lnw_ref[...], lnb_ref[...])


def _combine(z, gates, h1, ln_w, ln_b):
    m = h1.shape[0]
    tm = ROW_TILE
    n_tiles = m // tm
    row = lambda i: (i, 0)
    fixed = lambda i: (0, 0)
    choice = lambda kk: pl.BlockSpec((tm, D_MODEL), lambda i: (kk * n_tiles + i, 0))
    return pl.pallas_call(
        _combine_kernel,
        grid=(n_tiles,),
        in_specs=[choice(0), choice(1), choice(2), choice(3), pl.BlockSpec((tm, LANES), row),
                  pl.BlockSpec((tm, D_MODEL), row), pl.BlockSpec((1, D_MODEL), fixed), pl.BlockSpec((1, D_MODEL), fixed)],
        out_specs=pl.BlockSpec((tm, D_MODEL), row),
        out_shape=jax.ShapeDtypeStruct((m, D_MODEL), F32),
        compiler_params=_cparams("parallel"),
        name="combine_ln",
    )(z, z, z, z, gates, h1, ln_w, ln_b)


def _routing_tables(top_idx):
    m = top_idx.shape[0]
    n_assign = m * TOP_K
    bm = MOE_BLOCK
    n_blocks = -(-(n_assign + N_EXPERTS * (bm - 1)) // bm)
    p_rows = n_blocks * bm
    e_flat = top_idx.reshape(n_assign)
    order = jnp.argsort(e_flat, stable=True).astype(jnp.int32)
    experts = jnp.arange(N_EXPERTS, dtype=jnp.int32)
    counts = jnp.sum((e_flat[:, None] == experts[None, :]).astype(jnp.int32), axis=0)
    padded = (counts + bm - 1) // bm * bm
    start = jnp.cumsum(counts) - counts
    pend = jnp.cumsum(padded)
    pstart = pend - padded
    block_start = jnp.arange(n_blocks, dtype=jnp.int32) * bm
    block_e = jnp.minimum(jnp.sum((pend[None, :] <= block_start[:, None]).astype(jnp.int32), axis=1), N_EXPERTS - 1)
    per_row = lambda per_expert: jnp.repeat(per_expert[block_e], bm)
    row_pstart, row_start, row_count = per_row(pstart), per_row(start), per_row(counts)
    pos = jnp.arange(p_rows, dtype=jnp.int32) - row_pstart
    is_pad = pos >= row_count
    src_assign = order[jnp.clip(row_start + pos, 0, n_assign - 1)]
    tok = jnp.where(is_pad, 0, src_assign // TOP_K)
    pad_rank = (row_pstart - row_start) + (pos - row_count)
    slot = jnp.where(is_pad, n_assign + pad_rank, (src_assign % TOP_K) * m + src_assign // TOP_K)
    slot_ext = jnp.concatenate([p_rows + jnp.arange(bm, dtype=jnp.int32), slot])
    n_used = (pend[N_EXPERTS - 1] // bm).astype(jnp.int32)
    return tok, slot_ext, jnp.concatenate([block_e, n_used[None]]), p_rows + bm


def _block_ones(n):
    g = np.arange(n) // HEAD_DIM
    return jnp.asarray(g[:, None] == g[None, :], BF16)


def _sw_head_order():
    group = SW_HEADS // SW_KV_HEADS
    heads = [kv * group + g for g in range(group) for kv in range(SW_KV_HEADS)]
    return np.concatenate([np.arange(h * HEAD_DIM, (h + 1) * HEAD_DIM) for h in heads])


def _rope_tables(seq):
    half = HEAD_DIM // 2
    inv = ROPE_THETA ** (-jnp.arange(half, dtype=F32) / half)
    ang = jnp.arange(seq, dtype=F32)[:, None] * inv[None, :]
    cos, sin = jnp.cos(ang), jnp.sin(ang)
    reps = LANES // HEAD_DIM
    cos_t = jnp.tile(jnp.concatenate([cos, cos], axis=-1), (1, reps))
    sin_t = jnp.tile(jnp.concatenate([-sin, sin], axis=-1), (1, reps))
    return cos_t, sin_t


def kernel(x, w_in, b_in, rw_shift_mu, rw_w_up, rw_w0, rw_a_up, rw_a0, rw_g_up, rw_k_k, rw_k_a, rw_r_k, rw_ln_w, rw_ln_b, sw_sinks, ml_conv_w, ml_conv_b, ml_norm_w, w_out, ln1_w, ln1_b, router_w, router_b, exp_w_up, exp_b_up, exp_w_down, exp_b_down, ln2_w, ln2_b):
    batch, seq, d_model = x.shape
    assert d_model == D_MODEL and seq % ROW_TILE == 0 and seq % ML_CHUNK == 0 and seq % WINDOW == 0
    m = batch * seq
    depth = w_in.shape[0]
    row2 = lambda t: t.reshape(1, -1)

    sw_perm = _sw_head_order()
    in_cols = np.concatenate([np.arange(RW_COLS), RW_COLS + sw_perm, np.arange(RW_COLS + SW_DIM, C_IN)])
    ones_rw = _block_ones(RW_DIM)
    ones_ml = _block_ones(ML_DIM)
    cos_t, sin_t = _rope_tables(seq)
    src = np.arange(LANES)[:, None]
    dst = np.arange(2 * ML_DIM)[None, :]
    expand = jnp.asarray(src == (dst // ML_DIM) * ML_HEADS + (dst % ML_DIM) // HEAD_DIM, BF16)
    pj = np.arange(2 * LANES)[:, None]
    pc = np.arange(2 * LANES)[None, :]
    perm = jnp.asarray(pj == 2 * (pc % LANES) + pc // LANES, BF16)
    n_layers = exp_b_up.shape[0]
    b_glu = exp_b_up[:, :, 0::2].reshape(n_layers, N_EXPERTS, 1, D_EXPERT)
    b_lin = exp_b_up[:, :, 1::2].reshape(n_layers, N_EXPERTS, 1, D_EXPERT)
    b_dn = exp_b_down.reshape(n_layers, N_EXPERTS, 1, D_MODEL)

    h = x.reshape(m, D_MODEL)
    for l in range(depth):
        w_l = jnp.pad(w_in[l][:, in_cols], ((0, 0), (0, C_IN_PAD - C_IN))).astype(BF16)
        b_l = jnp.pad(b_in[l][in_cols], (0, C_IN_PAD - C_IN)).reshape(1, C_IN_PAD)
        zero = jnp.zeros((RW_LORA_W, RW_DIM), F32)
        wcomb = jnp.concatenate([jnp.concatenate([rw_w_up[l], zero], axis=1),
                                 jnp.concatenate([zero, rw_a_up[l]], axis=1)], axis=0).astype(BF16)
        w_o = w_out[l]
        w_o_rw = w_o[0:RW_DIM].astype(BF16)
        w_o_sw = w_o[RW_DIM + sw_perm].astype(BF16)
        w_o_ml = w_o[RW_DIM + SW_DIM:].astype(BF16)
        rw_pad = jnp.pad(router_w[l], ((0, 0), (0, LANES - N_EXPERTS)))
        rw_hi = rw_pad.astype(BF16)
        rw_r1 = rw_pad - rw_hi.astype(F32)
        rw_mid = rw_r1.astype(BF16)
        rw_lo = (rw_r1 - rw_mid.astype(F32)).astype(BF16)
        rb_pad = jnp.pad(router_b[l], (0, LANES - N_EXPERTS), constant_values=NEG_BIG).reshape(1, LANES)

        p_rw, p_sw, p_ml = _in_proj(h, w_l, b_l)
        r, lw, k, v, a, b, g = _rw_prep(p_rw, seq, row2(rw_shift_mu[l]), wcomb, rw_g_up[l].astype(BF16),
                                        row2(rw_w0[l]), row2(rw_a0[l]), row2(rw_k_k[l]), row2(rw_k_a[l]), ones_rw)
        m_mat, yh, g_mat, y0 = _rw_chunks(r, lw, k, v, a, b)
        y = _rw_scan(batch, seq, m_mat, yh, g_mat, y0).reshape(m, RW_DIM)
        rw_out = _rw_post(y, r, k, v, g, row2(rw_r_k[l]), row2(rw_ln_w[l]), row2(rw_ln_b[l]), ones_rw)
        sw_out = _swa(p_sw.reshape(batch, seq, SW_COLS), sw_sinks[l], cos_t, sin_t).reshape(m, SW_DIM)
        ml_out = _mlstm(p_ml.reshape(batch, seq, ML_COLS_PAD), ml_conv_w[l], row2(ml_conv_b[l]),
                        row2(ml_norm_w[l]), expand, ones_ml).reshape(m, ML_DIM)
        h1, h1_packed, idx_pad, gates = _out_proj(rw_out, sw_out, ml_out, h, w_o_rw, w_o_sw, w_o_ml, row2(ln1_w[l]),
                                                 row2(ln1_b[l]), (rw_hi, rw_mid, rw_lo), rb_pad)

        tok, slot_ext, block_e, z_rows = _routing_tables(idx_pad[:, 0:TOP_K])
        z = _experts(l, block_e, tok, slot_ext, h1_packed, exp_w_up, b_glu, b_lin, exp_w_down, b_dn, perm, z_rows)
        h = _combine(z, gates, h1, row2(ln2_w[l]), row2(ln2_b[l]))
    return h.reshape(batch, seq, D_MODEL)
```

```python
import functools

import numpy as np
import jax
import jax.numpy as jnp
from jax import lax
from jax.experimental import pallas as pl
from jax.experimental.pallas import tpu as pltpu

F32 = jnp.float32
BF16 = jnp.bfloat16

D_MODEL = 1024
HEAD_DIM = 64
RW_HEADS = 6
SW_HEADS = 6
SW_KV_HEADS = 2
ML_HEADS = 4
RW_DIM = RW_HEADS * HEAD_DIM
SW_DIM = SW_HEADS * HEAD_DIM
SW_KV_DIM = SW_KV_HEADS * HEAD_DIM
ML_DIM = ML_HEADS * HEAD_DIM
RW_LORA_W = 64
RW_LORA_A = 64
RW_LORA_G = 128
RW_COLS = 3 * RW_DIM + RW_LORA_W + RW_LORA_A + RW_LORA_G
SW_COLS = SW_DIM + 2 * SW_KV_DIM
ML_COLS = 4 * ML_DIM + 2 * ML_HEADS
C_IN = RW_COLS + SW_COLS + ML_COLS
WINDOW = 128
ROPE_THETA = 10000.0
CONV_WIDTH = 4
N_EXPERTS = 32
TOP_K = 4
D_EXPERT = 1024
SWIGLU_ALPHA = 1.702
SWIGLU_LIMIT = 7.0
LN_EPS = 1e-5
RW_GN_EPS = 64e-5
ML_GN_EPS = 1e-6
DEPTH = 4
DN_ALPHA = (2 * DEPTH) ** 0.25

LANES = 128
SUBLANES = 8
ML_COLS_PAD = 1152
C_IN_PAD = RW_COLS + SW_COLS + ML_COLS_PAD
ROW_TILE = 512
RW_CHUNK = 64
RW_PAIR = 2 * HEAD_DIM
RW_NPAIR = RW_HEADS // 2
ML_CHUNK = 256
MOE_BLOCK = 512
NEG_BIG = -1e30
VMEM_LIMIT = 56 * 1024 * 1024


def _cparams(*sem):
    return pltpu.CompilerParams(dimension_semantics=sem, vmem_limit_bytes=VMEM_LIMIT)


def _dot(a, b):
    return jnp.dot(a.astype(BF16), b.astype(BF16), preferred_element_type=F32)


def _dot_nt(a, b):
    return lax.dot_general(a.astype(BF16), b.astype(BF16), (((1,), (1,)), ((), ())),
                           preferred_element_type=F32)


def _split3(x):
    hi = x.astype(BF16)
    r1 = x - hi.astype(F32)
    mid = r1.astype(BF16)
    lo = (r1 - mid.astype(F32)).astype(BF16)
    return hi, mid, lo


def _dot_xl(x, w):
    hi = x.astype(BF16)
    mid = (x - hi.astype(F32)).astype(BF16)
    return jnp.dot(hi, w, preferred_element_type=F32) + jnp.dot(mid, w, preferred_element_type=F32)


def _dot_lx(w, x):
    hi, mid, lo = _split3(x)
    return (jnp.dot(w, hi, preferred_element_type=F32) + jnp.dot(w, mid, preferred_element_type=F32)
            + jnp.dot(w, lo, preferred_element_type=F32))


def _softplus(z):
    return jnp.maximum(z, 0.0) + jnp.log(1.0 + jnp.exp(-jnp.abs(z)))


def _sigmoid(z):
    return 1.0 / (1.0 + jnp.exp(-z))


def _shift_rows(x, prev8, j):
    cat = jnp.concatenate([prev8, x], axis=0)
    return pltpu.roll(cat, j, 0)[SUBLANES:, :]


def _layer_norm(x, w, b):
    mu = jnp.mean(x, axis=-1, keepdims=True)
    d = x - mu
    var = jnp.mean(d * d, axis=-1, keepdims=True)
    return d * lax.rsqrt(var + LN_EPS) * w + b


def _pack_bf16_pairs(x):
    c = x.shape[1] // 2
    bits = pltpu.bitcast(x.astype(BF16).astype(F32), jnp.uint32)
    return lax.shift_right_logical(bits[:, 0:c], jnp.uint32(16)) | (bits[:, c:2 * c] & jnp.uint32(0xFFFF0000))


def _unpack_bf16_pairs(w):
    lo = pltpu.bitcast(lax.shift_left(w, jnp.uint32(16)), F32)
    hi = pltpu.bitcast(w & jnp.uint32(0xFFFF0000), F32)
    return jnp.concatenate([lo, hi], axis=1).astype(BF16)


def _head_norm(y, ones_bd, eps):
    inv = 1.0 / HEAD_DIM
    mu = _dot_xl(y, ones_bd) * inv
    d = y - mu
    var = _dot_xl(d * d, ones_bd) * inv
    return d * lax.rsqrt(var + eps)


def _inproj_kernel(x_ref, w_ref, b_ref, rw_ref, sw_ref, ml_ref):
    _project_in(x_ref[...], w_ref, b_ref, rw_ref, sw_ref, ml_ref)


def _project_in(x, w_ref, b_ref, rw_ref, sw_ref, ml_ref):
    x = x.astype(BF16)
    c1, c2 = RW_COLS, RW_COLS + SW_COLS
    rw_ref[...] = jnp.dot(x, w_ref[:, 0:c1], preferred_element_type=F32) + b_ref[:, 0:c1]
    sw_ref[...] = jnp.dot(x, w_ref[:, c1:c2], preferred_element_type=F32) + b_ref[:, c1:c2]
    ml_ref[...] = jnp.dot(x, w_ref[:, c2:C_IN_PAD], preferred_element_type=F32) + b_ref[:, c2:C_IN_PAD]


def _in_proj(h2d, w, b):
    m = h2d.shape[0]
    tm = ROW_TILE
    row = lambda i: (i, 0)
    fixed = lambda i: (0, 0)
    return pl.pallas_call(
        _inproj_kernel,
        grid=(m // tm,),
        in_specs=[pl.BlockSpec((tm, D_MODEL), row), pl.BlockSpec((D_MODEL, C_IN_PAD), fixed),
                  pl.BlockSpec((1, C_IN_PAD), fixed)],
        out_specs=[pl.BlockSpec((tm, RW_COLS), row), pl.BlockSpec((tm, SW_COLS), row),
                   pl.BlockSpec((tm, ML_COLS_PAD), row)],
        out_shape=[jax.ShapeDtypeStruct((m, RW_COLS), F32), jax.ShapeDtypeStruct((m, SW_COLS), F32),
                   jax.ShapeDtypeStruct((m, ML_COLS_PAD), F32)],
        compiler_params=_cparams("parallel"),
        name="in_proj",
    )(h2d, w, b)


def _rw_prep_kernel(tiles_per_seq, p_ref, pprev_ref, mu_ref, wcomb_ref, gup_ref, w0_ref, a0_ref, kk_ref,
                    ka_ref, ones_ref, r_o, lw_o, k_o, v_o, a_o, b_o, g_o):
    i = pl.program_id(0)
    x = p_ref[...]
    first = (i % tiles_per_seq) == 0
    prev8 = jnp.where(first, 0.0, pprev_ref[...])
    xs = x + (_shift_rows(x, prev8, 1) - x) * mu_ref[...]
    d = RW_DIM
    r = xs[:, 0:d]
    k = xs[:, d:2 * d]
    v = xs[:, 2 * d:3 * d]
    slab = xs[:, 3 * d:3 * d + LANES]
    lane = lax.broadcasted_iota(jnp.int32, slab.shape, 1)
    slab = jnp.where(lane < RW_LORA_W, jnp.tanh(slab), slab)
    wa = _dot(slab, wcomb_ref[...])
    w_pre = w0_ref[...] + wa[:, 0:d]
    a_pre = a0_ref[...] + wa[:, d:2 * d]
    w_log = -_softplus(-w_pre) - 0.5
    log_decay = -jnp.exp(w_log)
    a_sig = _sigmoid(a_pre)
    g = _dot(_sigmoid(xs[:, 3 * d + LANES:3 * d + 2 * LANES]), gup_ref[...])
    kk = k * kk_ref[...]
    ss = _dot_xl(kk * kk, ones_ref[...])
    kk = kk / jnp.maximum(jnp.sqrt(ss), 1e-12)
    r_o[...] = r
    lw_o[...] = log_decay
    k_o[...] = k * (1.0 + (a_sig - 1.0) * ka_ref[...])
    v_o[...] = v
    a_o[...] = -kk
    b_o[...] = kk * a_sig
    g_o[...] = g


def _rw_prep(p_rw, seq, mu, wcomb, gup, w0, a0, k_k, k_a, ones_bd):
    m = p_rw.shape[0]
    tm = ROW_TILE
    row = lambda i: (i, 0)
    fixed = lambda i: (0, 0)
    prev = lambda i: (jnp.maximum(i * (tm // SUBLANES) - 1, 0), 0)
    vec = pl.BlockSpec((1, RW_DIM), fixed)
    out = jax.ShapeDtypeStruct((m, RW_DIM), F32)
    return pl.pallas_call(
        functools.partial(_rw_prep_kernel, seq // tm),
        grid=(m // tm,),
        in_specs=[pl.BlockSpec((tm, RW_COLS), row), pl.BlockSpec((SUBLANES, RW_COLS), prev),
                  pl.BlockSpec((1, RW_COLS), fixed), pl.BlockSpec((LANES, 2 * RW_DIM), fixed),
                  pl.BlockSpec((RW_LORA_G, RW_DIM), fixed), vec, vec, vec, vec,
                  pl.BlockSpec((RW_DIM, RW_DIM), fixed)],
        out_specs=[pl.BlockSpec((tm, RW_DIM), row)] * 7,
        out_shape=[out] * 7,
        compiler_params=_cparams("parallel"),
        name="rw_prep",
    )(p_rw, p_rw, mu, wcomb, gup, w0, a0, k_k, k_a, ones_bd)


def _rw_chunk_kernel(n_chunk, r_ref, lw_ref, k_ref, v_ref, a_ref, b_ref, m_o, yh_o, g_o, y0_o):
    L = RW_CHUNK
    ti = lax.broadcasted_iota(jnp.int32, (L, L), 0)
    si = lax.broadcasted_iota(jnp.int32, (L, L), 1)
    tri = jnp.where(ti >= si, 1.0, 0.0).astype(BF16)
    n2 = 2 * L
    ri = lax.broadcasted_iota(jnp.int32, (n2, n2), 0)
    ci = lax.broadcasted_iota(jnp.int32, (n2, n2), 1)
    rr = jnp.where(ri >= L, ri - L, ri)
    cc = jnp.where(ci >= L, ci - L, ci)
    strict = rr > cc
    incl = rr >= cc
    eye = ri == ci
    eye_f = jnp.where(eye, 1.0, 0.0)
    lane = lax.broadcasted_iota(jnp.int32, (L, RW_PAIR), 1)
    head0 = lane < HEAD_DIM

    def bmm(x, y):
        return jnp.einsum('bmk,bkn->bmn', x.astype(BF16), y.astype(BF16), preferred_element_type=F32)

    def bmm_nt(x, y):
        return jnp.einsum('bmk,bnk->bmn', x.astype(BF16), y.astype(BF16), preferred_element_type=F32)

    def bmm_tn(x, y):
        return bmm(jnp.swapaxes(x, 1, 2), y)

    def stack(x):
        return jnp.concatenate([jnp.where(head0, x, 0.0), jnp.where(head0, 0.0, x)], axis=1)

    def chunked(ref, cols):
        return ref[:, cols].reshape(n_chunk, L, RW_PAIR)

    tri_b = jnp.broadcast_to(tri, (n_chunk, L, L))
    for p in range(RW_NPAIR):
        cols = slice(p * RW_PAIR, (p + 1) * RW_PAIR)
        lw = chunked(lw_ref, cols)
        hi, mid, lo = _split3(lw)
        tsum = lambda part: jnp.einsum('bts,bsn->btn', tri_b, part, preferred_element_type=F32)
        lc = tsum(hi) + tsum(mid) + tsum(lo)
        lcl = lc[:, L - 1:L, :]
        gam = jnp.exp(lc)
        gam_prev = jnp.exp(lc - lw)
        inv = jnp.exp(-lc)
        to_end = jnp.exp(lcl - lc)
        r = chunked(r_ref, cols)
        k = chunked(k_ref, cols)
        v = chunked(v_ref, cols)
        a = chunked(a_ref, cols)
        b = chunked(b_ref, cols)
        x1 = stack(a * gam_prev)
        r1 = stack(r * gam)
        x2 = stack(b * inv)
        k2 = stack(k * inv)
        vs = stack(v)
        bs = stack(b * to_end)
        ks = stack(k * to_end)
        aa = bmm_nt(jnp.concatenate([x1, r1], axis=1), jnp.concatenate([x2, k2], axis=1))
        a_ab = jnp.where(strict, aa[:, 0:n2, 0:n2], 0.0)
        a_ak = jnp.where(strict, aa[:, 0:n2, n2:2 * n2], 0.0)
        a_rb = jnp.where(incl, aa[:, n2:2 * n2, 0:n2], 0.0)
        a_rk = jnp.where(incl, aa[:, n2:2 * n2, n2:2 * n2], 0.0)
        apow = a_ab
        t_inv = eye_f + a_ab
        for _ in range(5):
            apow = bmm(apow, apow)
            t_inv = t_inv + bmm(t_inv, apow)
        akv = bmm(a_ak, vs)
        pw = bmm(t_inv, jnp.concatenate([x1, akv], axis=2))
        pmat = pw[:, :, 0:RW_PAIR]
        wmat = pw[:, :, RW_PAIR:2 * RW_PAIR]
        gl_row = jnp.exp(lcl)
        m_mat = bmm_tn(bs, pmat) + jnp.where(eye, gl_row, 0.0)
        g_mat = bmm_tn(jnp.concatenate([bs, ks], axis=1), jnp.concatenate([wmat, vs], axis=1))
        yy = bmm(a_rb, pw)
        yh = r1 + yy[:, :, 0:RW_PAIR]
        y0 = yy[:, :, RW_PAIR:2 * RW_PAIR] + bmm(a_rk, vs)
        m_o[:, p] = m_mat.astype(BF16)
        yh_o[:, p] = yh.astype(BF16)
        g_o[:, p] = g_mat
        y0_o[:, p] = y0


def _rw_chunks(r, lw, k, v, a, b):
    m = r.shape[0]
    tm = ROW_TILE
    nc = tm // RW_CHUNK
    row = lambda i: (i, 0)
    blk = lambda i: (i, 0, 0, 0)
    n2 = 2 * RW_CHUNK
    n_tot = m // RW_CHUNK
    ospec = pl.BlockSpec((nc, RW_NPAIR, n2, RW_PAIR), blk)
    return pl.pallas_call(
        functools.partial(_rw_chunk_kernel, nc),
        grid=(m // tm,),
        in_specs=[pl.BlockSpec((tm, RW_DIM), row)] * 6,
        out_specs=[ospec] * 4,
        out_shape=[jax.ShapeDtypeStruct((n_tot, RW_NPAIR, n2, RW_PAIR), BF16),
                   jax.ShapeDtypeStruct((n_tot, RW_NPAIR, n2, RW_PAIR), BF16),
                   jax.ShapeDtypeStruct((n_tot, RW_NPAIR, n2, RW_PAIR), F32),
                   jax.ShapeDtypeStruct((n_tot, RW_NPAIR, n2, RW_PAIR), F32)],
        compiler_params=_cparams("parallel"),
        name="rw_chunks",
    )(r, lw, k, v, a, b)


def _rw_scan_kernel(batch, m_ref, yh_ref, g_ref, y0_ref, y_o, h_ref):
    c = pl.program_id(0)

    @pl.when(c == 0)
    def _():
        h_ref[...] = jnp.zeros_like(h_ref)

    L = RW_CHUNK
    for bi in range(batch):
        for p in range(RW_NPAIR):
            hb = h_ref[bi, p].astype(BF16)
            yrows = jnp.dot(yh_ref[bi, 0, p], hb, preferred_element_type=F32) + y0_ref[bi, 0, p]
            y_o[bi, :, p * RW_PAIR:(p + 1) * RW_PAIR] = yrows[0:L, :] + yrows[L:2 * L, :]
            h_ref[bi, p] = jnp.dot(m_ref[bi, 0, p], hb, preferred_element_type=F32) + g_ref[bi, 0, p]


def _rw_scan(batch, seq, m_mat, yh, g_mat, y0):
    nc = seq // RW_CHUNK
    n2 = 2 * RW_CHUNK
    shp = (batch, nc, RW_NPAIR, n2, RW_PAIR)
    args = [t.reshape(shp) for t in (m_mat, yh, g_mat, y0)]
    ispec = pl.BlockSpec((batch, 1, RW_NPAIR, n2, RW_PAIR), lambda c: (0, c, 0, 0, 0))
    return pl.pallas_call(
        functools.partial(_rw_scan_kernel, batch),
        grid=(nc,),
        in_specs=[ispec] * 4,
        out_specs=pl.BlockSpec((batch, RW_CHUNK, RW_DIM), lambda c: (0, c, 0)),
        out_shape=jax.ShapeDtypeStruct((batch, seq, RW_DIM), F32),
        scratch_shapes=[pltpu.VMEM((batch, RW_NPAIR, RW_PAIR, RW_PAIR), F32)],
        compiler_params=_cparams("arbitrary"),
        name="rw_scan",
    )(*args)


def _rw_post_kernel(y_ref, r_ref, k_ref, v_ref, g_ref, rk_ref, lnw_ref, lnb_ref, ones_ref, o_ref):
    ones_bd = ones_ref[...]
    y = _head_norm(y_ref[...], ones_bd, RW_GN_EPS) * lnw_ref[...] + lnb_ref[...]
    bonus = _dot_xl(r_ref[...] * k_ref[...] * rk_ref[...], ones_bd) * v_ref[...]
    o_ref[...] = (y + bonus) * g_ref[...]


def _rw_post(y, r, k, v, g, r_k, ln_w, ln_b, ones_bd):
    m = y.shape[0]
    tm = ROW_TILE
    row = lambda i: (i, 0)
    fixed = lambda i: (0, 0)
    vec = pl.BlockSpec((1, RW_DIM), fixed)
    return pl.pallas_call(
        _rw_post_kernel,
        grid=(m // tm,),
        in_specs=[pl.BlockSpec((tm, RW_DIM), row)] * 5 + [vec, vec, vec, pl.BlockSpec((RW_DIM, RW_DIM), fixed)],
        out_specs=pl.BlockSpec((tm, RW_DIM), row),
        out_shape=jax.ShapeDtypeStruct((m, RW_DIM), F32),
        compiler_params=_cparams("parallel"),
        name="rw_post",
    )(y, r, k, v, g, r_k, ln_w, ln_b, ones_bd)


def _rope(x, cos, sin_signed):
    lane = lax.broadcasted_iota(jnp.int32, x.shape, 1)
    half = HEAD_DIM // 2
    first_half = (lane % HEAD_DIM) < half
    rot = jnp.where(first_half, pltpu.roll(x, LANES - half, 1), pltpu.roll(x, half, 1))
    return x * cos + rot * sin_signed


def _swa_kernel(sink_ref, q_ref, kc_ref, vc_ref, kp_ref, vp_ref, cosc_ref, sinc_ref, cosp_ref, sinp_ref, o_ref):
    j = pl.program_id(1)
    W = WINDOW
    cos_c, sin_c = cosc_ref[...], sinc_ref[...]
    k_cat = jnp.concatenate([_rope(kp_ref[0], cosp_ref[...], sinp_ref[...]), _rope(kc_ref[0], cos_c, sin_c)], axis=0)
    v_cat = jnp.concatenate([vp_ref[0], vc_ref[0]], axis=0).astype(BF16)
    k_cat = k_cat.astype(BF16)
    qi = lax.broadcasted_iota(jnp.int32, (W, 2 * W), 0)
    kj = lax.broadcasted_iota(jnp.int32, (W, 2 * W), 1)
    lo = jnp.where(j > 0, qi, jnp.maximum(qi, W - 1))
    bias = jnp.where((kj > lo) & (kj <= qi + W), 0.0, NEG_BIG)
    lane = lax.broadcasted_iota(jnp.int32, (W, LANES), 1)
    kv0 = lane < HEAD_DIM
    scale = HEAD_DIM ** -0.5
    group = SW_HEADS // SW_KV_HEADS
    for g in range(group):
        q = _rope(q_ref[0, :, g * LANES:(g + 1) * LANES], cos_c, sin_c) * scale
        outs = []
        for kv in range(SW_KV_HEADS):
            qm = jnp.where(kv0, q, 0.0) if kv == 0 else jnp.where(kv0, 0.0, q)
            s = _dot_nt(qm, k_cat) + bias
            sink = sink_ref[kv * group + g]
            mx = jnp.maximum(jnp.max(s, axis=-1, keepdims=True), sink)
            pr = jnp.exp(s - mx)
            den = jnp.sum(pr, axis=-1, keepdims=True) + jnp.exp(sink - mx)
            outs.append(jnp.dot(pr.astype(BF16), v_cat, preferred_element_type=F32) / den)
        o_ref[0, :, g * LANES:(g + 1) * LANES] = jnp.where(kv0, outs[0], outs[1])


def _swa(p_sw3, sinks, cos_t, sin_t):
    batch, seq, _ = p_sw3.shape
    W = WINDOW
    nb = seq // W
    kcol = SW_DIM // LANES
    cur = lambda c: (lambda b, j: (b, j, c))
    prv = lambda c: (lambda b, j: (b, jnp.maximum(j - 1, 0), c))
    tab_c = pl.BlockSpec((W, LANES), lambda b, j: (j, 0))
    tab_p = pl.BlockSpec((W, LANES), lambda b, j: (jnp.maximum(j - 1, 0), 0))
    kv_blk = lambda f: pl.BlockSpec((1, W, LANES), f)
    return pl.pallas_call(
        _swa_kernel,
        grid=(batch, nb),
        in_specs=[pl.BlockSpec(memory_space=pltpu.SMEM),
                  pl.BlockSpec((1, W, SW_DIM), cur(0)),
                  kv_blk(cur(kcol)), kv_blk(cur(kcol + 1)), kv_blk(prv(kcol)), kv_blk(prv(kcol + 1)),
                  tab_c, tab_c, tab_p, tab_p],
        out_specs=pl.BlockSpec((1, W, SW_DIM), cur(0)),
        out_shape=jax.ShapeDtypeStruct((batch, seq, SW_DIM), F32),
        compiler_params=_cparams("parallel", "parallel"),
        name="swa",
    )(sinks, p_sw3, p_sw3, p_sw3, p_sw3, p_sw3, cos_t, sin_t, cos_t, sin_t)


def _mlstm_kernel(p_ref, pprev_ref, cw_ref, cb_ref, nw_ref, expand_ref, ones_ref, o_ref, c_ref, nv_ref, m_ref):
    j = pl.program_id(1)
    L = ML_CHUNK
    d = ML_DIM

    @pl.when(j == 0)
    def _():
        c_ref[...] = jnp.zeros_like(c_ref)
        nv_ref[...] = jnp.zeros_like(nv_ref)
        m_ref[...] = jnp.zeros_like(m_ref)

    x = p_ref[0]
    qk_pre = x[:, 0:2 * d]
    prev8 = jnp.where(j == 0, 0.0, pprev_ref[0][:, 0:2 * d])
    conv = cb_ref[...] + cw_ref[CONV_WIDTH - 1:CONV_WIDTH, :] * qk_pre
    for s in range(1, CONV_WIDTH):
        conv = conv + cw_ref[CONV_WIDTH - 1 - s:CONV_WIDTH - s, :] * _shift_rows(qk_pre, prev8, s)
    qk = conv * _sigmoid(conv)
    q = qk[:, 0:d]
    k = qk[:, d:2 * d] * (HEAD_DIM ** -0.5)
    v = x[:, 2 * d:3 * d]
    o_gate = x[:, 3 * d:4 * d]
    gates = _dot_xl(x[:, 4 * d:4 * d + LANES], expand_ref[...])
    i_full = gates[:, 0:d]
    f_full = gates[:, d:2 * d]
    lf_full = jnp.minimum(f_full, 0.0) - jnp.log(1.0 + jnp.exp(-jnp.abs(f_full)))
    ti = lax.broadcasted_iota(jnp.int32, (L, L), 0)
    si = lax.broadcasted_iota(jnp.int32, (L, L), 1)
    causal = ti >= si
    tri = jnp.where(causal, 1.0, 0.0).astype(BF16)
    bc_full = _dot_lx(tri, lf_full)
    u_t = (i_full - bc_full).T
    lane = lax.broadcasted_iota(jnp.int32, (L, d), 1)
    lane_row = lax.broadcasted_iota(jnp.int32, (1, d), 1)
    q_b = q.astype(BF16)
    k_b = k.astype(BF16)
    m_prev_row = m_ref[...]
    nv_row = nv_ref[...]
    qn = q * nv_row
    num = jnp.zeros((L, d), F32)
    inter_full = jnp.zeros((L, d), F32)
    den_full = jnp.zeros((L, d), F32)
    mnew_row = jnp.zeros((1, d), F32)
    for h in range(ML_HEADS):
        c0 = h * HEAD_DIM
        in_head = (lane >= c0) & (lane < c0 + HEAD_DIM)
        in_head_row = (lane_row >= c0) & (lane_row < c0 + HEAD_DIM)
        bc_col = bc_full[:, c0:c0 + 1]
        dmat = jnp.where(causal, bc_col + u_t[c0:c0 + 1, :], NEG_BIG)
        m_inter = bc_col + m_prev_row[:, c0:c0 + 1]
        m_t = jnp.maximum(m_inter, jnp.max(dmat, axis=-1, keepdims=True))
        inter = jnp.exp(m_inter - m_t)
        e = jnp.exp(dmat - m_t)
        sm = _dot_nt(jnp.where(in_head, q, 0.0), k_b) * e
        num = num + _dot(sm, jnp.where(in_head, v, 0.0))
        nq = jnp.sum(sm, axis=-1, keepdims=True) + inter * jnp.sum(jnp.where(in_head, qn, 0.0), axis=-1, keepdims=True)
        den = jnp.maximum(jnp.abs(nq), jnp.exp(-m_t))
        inter_full = jnp.where(in_head, inter, inter_full)
        den_full = jnp.where(in_head, den, den_full)
        mnew_row = jnp.where(in_head_row, m_t[L - 1:L, :], mnew_row)
    c_mat = c_ref[...]
    num = num + inter_full * jnp.dot(q_b, c_mat.astype(BF16), preferred_element_type=F32)
    hout = num / den_full
    hn = _head_norm(hout, ones_ref[...], ML_GN_EPS) * nw_ref[...]
    o_ref[0] = _sigmoid(o_gate) * hn
    bcl_row = bc_full[L - 1:L, :]
    wst = jnp.exp(bcl_row - bc_full + i_full - mnew_row)
    dec_row = jnp.exp(bcl_row + m_prev_row - mnew_row)
    kw = k * wst
    ri = lax.broadcasted_iota(jnp.int32, (d, d), 0) // HEAD_DIM
    ci = lax.broadcasted_iota(jnp.int32, (d, d), 1) // HEAD_DIM
    c_ref[...] = dec_row * c_mat + jnp.where(ri == ci, _dot(kw.T, v), 0.0)
    nv_ref[...] = dec_row * nv_row + jnp.sum(kw, axis=0, keepdims=True)
    m_ref[...] = mnew_row


def _mlstm(p_ml3, conv_w, conv_b, norm_w, expand, ones_bd):
    batch, seq, _ = p_ml3.shape
    L = ML_CHUNK
    fixed = lambda b, j: (0, 0)
    return pl.pallas_call(
        _mlstm_kernel,
        grid=(batch, seq // L),
        in_specs=[pl.BlockSpec((1, L, ML_COLS_PAD), lambda b, j: (b, j, 0)),
                  pl.BlockSpec((1, SUBLANES, ML_COLS_PAD), lambda b, j: (b, jnp.maximum(j * (L // SUBLANES) - 1, 0), 0)),
                  pl.BlockSpec((CONV_WIDTH, 2 * ML_DIM), fixed), pl.BlockSpec((1, 2 * ML_DIM), fixed),
                  pl.BlockSpec((1, ML_DIM), fixed), pl.BlockSpec((LANES, 2 * ML_DIM), fixed),
                  pl.BlockSpec((ML_DIM, ML_DIM), fixed)],
        out_specs=pl.BlockSpec((1, L, ML_DIM), lambda b, j: (b, j, 0)),
        out_shape=jax.ShapeDtypeStruct((batch, seq, ML_DIM), F32),
        scratch_shapes=[pltpu.VMEM((ML_DIM, ML_DIM), F32), pltpu.VMEM((1, ML_DIM), F32),
                        pltpu.VMEM((1, ML_DIM), F32)],
        compiler_params=_cparams("parallel", "arbitrary"),
        name="mlstm",
    )(p_ml3, p_ml3, conv_w, conv_b, norm_w, expand, ones_bd)


def _outproj_kernel(rw_ref, sw_ref, ml_ref, h_ref, wrw_ref, wsw_ref, wml_ref, lnw_ref, lnb_ref,
                    rwh_ref, rwm_ref, rwl_ref, rb_ref, h1_o, h1p_o, idx_o, gate_o):
    mix = (_dot(rw_ref[...], wrw_ref[...]) + _dot(sw_ref[...], wsw_ref[...]) + _dot(ml_ref[...], wml_ref[...]))
    h1 = _layer_norm(DN_ALPHA * h_ref[...] + mix, lnw_ref[...], lnb_ref[...])
    h1_o[...] = h1
    h1p_o[...] = _pack_bf16_pairs(h1)
    xh, xm, xl = _split3(h1)
    wh, wm, wl = rwh_ref[...], rwm_ref[...], rwl_ref[...]
    dd = lambda a, b: jnp.dot(a, b, preferred_element_type=F32)
    logits = (dd(xh, wh) + (dd(xh, wm) + dd(xm, wh)) + (dd(xh, wl) + dd(xm, wm) + dd(xl, wh))) + rb_ref[...]
    lane = lax.broadcasted_iota(jnp.int32, logits.shape, 1).astype(F32)
    vals, idxs = [], []
    cur = logits
    for _ in range(TOP_K):
        mx = jnp.max(cur, axis=-1, keepdims=True)
        ix = jnp.min(jnp.where(cur == mx, lane, float(LANES)), axis=-1, keepdims=True)
        vals.append(mx)
        idxs.append(ix)
        cur = jnp.where(lane == ix, NEG_BIG * 2.0, cur)
    es = [jnp.exp(vv - vals[0]) for vv in vals]
    den = es[0] + es[1] + es[2] + es[3]
    idx_out = jnp.zeros_like(logits)
    gate_out = jnp.zeros_like(logits)
    for kk in range(TOP_K):
        idx_out = jnp.where(lane == float(kk), idxs[kk], idx_out)
        gate_out = jnp.where(lane == float(kk), es[kk] / den, gate_out)
    idx_o[...] = idx_out.astype(jnp.int32)
    gate_o[...] = gate_out


def _out_proj(rw_out, sw_out, ml_out, h2d, w_rw, w_sw, w_ml, ln_w, ln_b, rw_parts, rb):
    m = h2d.shape[0]
    tm = ROW_TILE
    row = lambda i: (i, 0)
    fixed = lambda i: (0, 0)
    full = lambda a: pl.BlockSpec(a.shape, fixed)
    return pl.pallas_call(
        _outproj_kernel,
        grid=(m // tm,),
        in_specs=[pl.BlockSpec((tm, RW_DIM), row), pl.BlockSpec((tm, SW_DIM), row), pl.BlockSpec((tm, ML_DIM), row),
                  pl.BlockSpec((tm, D_MODEL), row), full(w_rw), full(w_sw), full(w_ml), full(ln_w), full(ln_b),
                  full(rw_parts[0]), full(rw_parts[1]), full(rw_parts[2]), full(rb)],
        out_specs=[pl.BlockSpec((tm, D_MODEL), row), pl.BlockSpec((tm, D_MODEL // 2), row),
                   pl.BlockSpec((tm, LANES), row), pl.BlockSpec((tm, LANES), row)],
        out_shape=[jax.ShapeDtypeStruct((m, D_MODEL), F32), jax.ShapeDtypeStruct((m, D_MODEL // 2), jnp.uint32),
                   jax.ShapeDtypeStruct((m, LANES), jnp.int32), jax.ShapeDtypeStruct((m, LANES), F32)],
        compiler_params=_cparams("parallel"),
        name="out_proj_router",
    )(rw_out, sw_out, ml_out, h2d, w_rw, w_sw, w_ml, ln_w, ln_b, *rw_parts, rb)


def _expert_kernel(n_blocks, layer, be_ref, tokc_ref, tokn_ref, slot_ref, h_hbm, wup_ref, bg_ref, bl_ref,
                   wdn_ref, bd_ref, perm_ref, z_hbm, x0, x1, y0, y1, wg_s, wl_s, wd_s, gsem, ssem):
    del layer
    i = pl.program_id(0)
    bm = MOE_BLOCK

    def gather_row(tok, xdst, sem, r):
        return pltpu.make_async_copy(h_hbm.at[pl.ds(tok, 1)], xdst.at[pl.ds(r, 1)], sem)

    def scatter_row(dst, ysrc, sem, r):
        return pltpu.make_async_copy(ysrc.at[pl.ds(r, 1)], z_hbm.at[pl.ds(dst, 1)], sem)

    def wait_gather(xdst, sem):
        pltpu.make_async_copy(h_hbm.at[pl.ds(0, bm)], xdst, sem).wait()

    def wait_scatter(ysrc, sem):
        pltpu.make_async_copy(ysrc, z_hbm.at[pl.ds(0, bm)], sem).wait()

    @pl.when(i == 0)
    def _():
        y1[...] = jnp.zeros(y1.shape, F32)

        def prime(r, c):
            gather_row(tokc_ref[r], x0, gsem.at[0], r).start()
            return c

        lax.fori_loop(0, bm, prime, 0, unroll=8)

    n_used = be_ref[n_blocks]
    live = i <= n_used
    blk = jnp.minimum(i, n_blocks - 1)
    e_now = be_ref[blk]
    e_prev = be_ref[jnp.maximum(blk - 1, 0)]

    @pl.when(((i == 0) | (e_now != e_prev)) & live)
    def _():
        perm = perm_ref[...]
        for c in range(D_EXPERT // LANES):
            t = jnp.dot(wup_ref[0, 0, :, c * 2 * LANES:(c + 1) * 2 * LANES].astype(BF16), perm,
                        preferred_element_type=F32)
            wg_s[:, c * LANES:(c + 1) * LANES] = t[:, 0:LANES].astype(BF16)
            wl_s[:, c * LANES:(c + 1) * LANES] = t[:, LANES:2 * LANES].astype(BF16)
        wd_s[...] = wdn_ref[0, 0].astype(BF16)

    def step(x_cur, x_nxt, y_cur, y_nxt, g_cur, g_nxt, s_cur, s_nxt):
        wait_gather(x_cur, g_cur)
        for r in range(bm):
            gather_row(tokn_ref[r], x_nxt, g_nxt, r).start(priority=r % 2)
            scatter_row(slot_ref[r], y_nxt, s_nxt, r).start(priority=(r + 1) % 2)

        @pl.when(i > 0)
        def _():
            wait_scatter(y_cur, s_cur)

        x = _unpack_bf16_pairs(x_cur[...])
        hg = jnp.dot(x, wg_s[...], preferred_element_type=F32) + bg_ref[0, 0]
        hl = jnp.dot(x, wl_s[...], preferred_element_type=F32) + bl_ref[0, 0]
        hg = jnp.minimum(hg, SWIGLU_LIMIT)
        hl = jnp.clip(hl, -SWIGLU_LIMIT, SWIGLU_LIMIT)
        act = hg * _sigmoid(SWIGLU_ALPHA * hg) * (hl + 1.0)
        y_cur[...] = jnp.dot(act.astype(BF16), wd_s[...], preferred_element_type=F32) + bd_ref[0, 0]

        @pl.when(i == n_used)
        def _():
            wait_gather(x_nxt, g_nxt)
            wait_scatter(y_nxt, s_nxt)

    @pl.when(i > n_used)
    def _():
        @pl.when(i == n_used + 1)
        def _():
            y0[...] = jnp.zeros(y0.shape, F32)

        fill = pltpu.make_async_copy(y0, z_hbm.at[pl.ds(pl.multiple_of((i - 1) * bm, bm), bm)], ssem.at[0])
        fill.start()
        fill.wait()

    @pl.when((i % 2 == 0) & live)
    def _():
        step(x0, x1, y0, y1, gsem.at[0], gsem.at[1], ssem.at[0], ssem.at[1])

    @pl.when((i % 2 == 1) & live)
    def _():
        step(x1, x0, y1, y0, gsem.at[1], gsem.at[0], ssem.at[1], ssem.at[0])


def _experts(layer, block_e, tok, slot_ext, h1, w_up, b_glu, b_lin, w_down, b_down, perm, z_rows):
    bm = MOE_BLOCK
    n_blocks = tok.shape[0] // bm
    last = n_blocks - 1
    by_e = lambda i, be: (layer, be[jnp.minimum(i, last)], 0, 0)
    smem = lambda f: pl.BlockSpec((bm,), f, memory_space=pltpu.SMEM)
    grid_spec = pltpu.PrefetchScalarGridSpec(
        num_scalar_prefetch=1,
        grid=(n_blocks + 1,),
        in_specs=[smem(lambda i, be: (jnp.minimum(i, last),)), smem(lambda i, be: (jnp.minimum(i + 1, last),)),
                  smem(lambda i, be: (i,)),
                  pl.BlockSpec(memory_space=pl.ANY),
                  pl.BlockSpec((1, 1, D_MODEL, 2 * D_EXPERT), by_e),
                  pl.BlockSpec((1, 1, 1, D_EXPERT), by_e), pl.BlockSpec((1, 1, 1, D_EXPERT), by_e),
                  pl.BlockSpec((1, 1, D_EXPERT, D_MODEL), by_e), pl.BlockSpec((1, 1, 1, D_MODEL), by_e),
                  pl.BlockSpec((2 * LANES, 2 * LANES), lambda i, be: (0, 0))],
        out_specs=pl.BlockSpec(memory_space=pl.ANY),
        scratch_shapes=[pltpu.VMEM((bm, D_MODEL // 2), jnp.uint32), pltpu.VMEM((bm, D_MODEL // 2), jnp.uint32),
                        pltpu.VMEM((bm, D_MODEL), F32), pltpu.VMEM((bm, D_MODEL), F32),
                        pltpu.VMEM((D_MODEL, D_EXPERT), BF16), pltpu.VMEM((D_MODEL, D_EXPERT), BF16),
                        pltpu.VMEM((D_EXPERT, D_MODEL), BF16),
                        pltpu.SemaphoreType.DMA((2,)), pltpu.SemaphoreType.DMA((2,))],
    )
    return pl.pallas_call(
        functools.partial(_expert_kernel, n_blocks, layer),
        grid_spec=grid_spec,
        out_shape=jax.ShapeDtypeStruct((z_rows, D_MODEL), F32),
        compiler_params=pltpu.CompilerParams(dimension_semantics=("arbitrary",), vmem_limit_bytes=VMEM_LIMIT,
                                             disable_bounds_checks=True),
        name="experts",
    )(block_e, tok, tok, slot_ext, h1, w_up, b_glu, b_lin, w_down, b_down, perm)


def _combine_kernel(z0_ref, z1_ref, z2_ref, z3_ref, gate_ref, h_ref, lnw_ref, lnb_ref, *rest):
    gates = gate_ref[...]
    y = gates[:, 0:1] * z0_ref[...]
    for kk, z_ref in enumerate((z1_ref, z2_ref, z3_ref), start=1):
        y = y + gates[:, kk:kk + 1] * z_ref[...]
    h = _layer_norm(DN_ALPHA * h_ref[...] + y, lnw_ref[...], lnb_ref[...])
    if len(rest) == 1:
        rest[0][...] = h
    else:
        w_ref, b_ref, o_ref, rw_ref, sw_ref, ml_ref = rest
        o_ref[...] = h
        _project_in(h, w_ref, b_ref, rw_ref, sw_ref, ml_ref)


def _combine(z, gates, h1, ln_w, ln_b, w_next=None, b_next=None):
    m = h1.shape[0]
    fused = w_next is not None
    tm = ROW_TILE // 2 if fused else ROW_TILE
    n_tiles = m // tm
    row = lambda i: (i, 0)
    fixed = lambda i: (0, 0)
    choice = lambda kk: pl.BlockSpec((tm, D_MODEL), lambda i: (kk * n_tiles + i, 0))
    in_specs = [choice(0), choice(1), choice(2), choice(3), pl.BlockSpec((tm, LANES), row),
                pl.BlockSpec((tm, D_MODEL), row), pl.BlockSpec((1, D_MODEL), fixed), pl.BlockSpec((1, D_MODEL), fixed)]
    out_specs = [pl.BlockSpec((tm, D_MODEL), row)]
    out_shape = [jax.ShapeDtypeStruct((m, D_MODEL), F32)]
    args = [z, z, z, z, gates, h1, ln_w, ln_b]
    if fused:
        in_specs += [pl.BlockSpec((D_MODEL, C_IN_PAD), fixed), pl.BlockSpec((1, C_IN_PAD), fixed)]
        out_specs += [pl.BlockSpec((tm, RW_COLS), row), pl.BlockSpec((tm, SW_COLS), row),
                      pl.BlockSpec((tm, ML_COLS_PAD), row)]
        out_shape += [jax.ShapeDtypeStruct((m, RW_COLS), F32), jax.ShapeDtypeStruct((m, SW_COLS), F32),
                      jax.ShapeDtypeStruct((m, ML_COLS_PAD), F32)]
        args += [w_next, b_next]
    out = pl.pallas_call(
        _combine_kernel,
        grid=(n_tiles,),
        in_specs=in_specs,
        out_specs=out_specs,
        out_shape=out_shape,
        compiler_params=_cparams("parallel"),
        name="combine_ln_inproj" if fused else "combine_ln",
    )(*args)
    return out if fused else out[0]


def _routing_tables(top_idx):
    m = top_idx.shape[0]
    n_assign = m * TOP_K
    bm = MOE_BLOCK
    n_blocks = -(-(n_assign + N_EXPERTS * (bm - 1)) // bm)
    p_rows = n_blocks * bm
    e_flat = top_idx.reshape(n_assign)
    order = jnp.argsort(e_flat, stable=True).astype(jnp.int32)
    experts = jnp.arange(N_EXPERTS, dtype=jnp.int32)
    counts = jnp.sum((e_flat[:, None] == experts[None, :]).astype(jnp.int32), axis=0)
    padded = (counts + bm - 1) // bm * bm
    start = jnp.cumsum(counts) - counts
    pend = jnp.cumsum(padded)
    pstart = pend - padded
    block_start = jnp.arange(n_blocks, dtype=jnp.int32) * bm
    block_e = jnp.minimum(jnp.sum((pend[None, :] <= block_start[:, None]).astype(jnp.int32), axis=1), N_EXPERTS - 1)
    per_row = lambda per_expert: jnp.repeat(per_expert[block_e], bm)
    row_pstart, row_start, row_count = per_row(pstart), per_row(start), per_row(counts)
    pos = jnp.arange(p_rows, dtype=jnp.int32) - row_pstart
    is_pad = pos >= row_count
    src_assign = order[jnp.clip(row_start + pos, 0, n_assign - 1)]
    tok = jnp.where(is_pad, 0, src_assign // TOP_K)
    pad_rank = (row_pstart - row_start) + (pos - row_count)
    slot = jnp.where(is_pad, n_assign + pad_rank, (src_assign % TOP_K) * m + src_assign // TOP_K)
    slot_ext = jnp.concatenate([p_rows + jnp.arange(bm, dtype=jnp.int32), slot])
    n_used = (pend[N_EXPERTS - 1] // bm).astype(jnp.int32)
    return tok, slot_ext, jnp.concatenate([block_e, n_used[None]]), p_rows + bm


def _block_ones(n):
    g = np.arange(n) // HEAD_DIM
    return jnp.asarray(g[:, None] == g[None, :], BF16)


def _sw_head_order():
    group = SW_HEADS // SW_KV_HEADS
    heads = [kv * group + g for g in range(group) for kv in range(SW_KV_HEADS)]
    return np.concatenate([np.arange(h * HEAD_DIM, (h + 1) * HEAD_DIM) for h in heads])


def _rope_tables(seq):
    half = HEAD_DIM // 2
    inv = ROPE_THETA ** (-jnp.arange(half, dtype=F32) / half)
    ang = jnp.arange(seq, dtype=F32)[:, None] * inv[None, :]
    cos, sin = jnp.cos(ang), jnp.sin(ang)
    reps = LANES // HEAD_DIM
    cos_t = jnp.tile(jnp.concatenate([cos, cos], axis=-1), (1, reps))
    sin_t = jnp.tile(jnp.concatenate([-sin, sin], axis=-1), (1, reps))
    return cos_t, sin_t


def kernel(x, w_in, b_in, rw_shift_mu, rw_w_up, rw_w0, rw_a_up, rw_a0, rw_g_up, rw_k_k, rw_k_a, rw_r_k, rw_ln_w, rw_ln_b, sw_sinks, ml_conv_w, ml_conv_b, ml_norm_w, w_out, ln1_w, ln1_b, router_w, router_b, exp_w_up, exp_b_up, exp_w_down, exp_b_down, ln2_w, ln2_b):
    batch, seq, d_model = x.shape
    assert d_model == D_MODEL and seq % ROW_TILE == 0 and seq % ML_CHUNK == 0 and seq % WINDOW == 0
    m = batch * seq
    depth = w_in.shape[0]
    row2 = lambda t: t.reshape(1, -1)

    sw_perm = _sw_head_order()
    in_cols = np.concatenate([np.arange(RW_COLS), RW_COLS + sw_perm, np.arange(RW_COLS + SW_DIM, C_IN)])
    ones_rw = _block_ones(RW_DIM)
    ones_ml = _block_ones(ML_DIM)
    cos_t, sin_t = _rope_tables(seq)
    src = np.arange(LANES)[:, None]
    dst = np.arange(2 * ML_DIM)[None, :]
    expand = jnp.asarray(src == (dst // ML_DIM) * ML_HEADS + (dst % ML_DIM) // HEAD_DIM, BF16)
    pj = np.arange(2 * LANES)[:, None]
    pc = np.arange(2 * LANES)[None, :]
    perm = jnp.asarray(pj == 2 * (pc % LANES) + pc // LANES, BF16)
    n_layers = exp_b_up.shape[0]
    b_glu = exp_b_up[:, :, 0::2].reshape(n_layers, N_EXPERTS, 1, D_EXPERT)
    b_lin = exp_b_up[:, :, 1::2].reshape(n_layers, N_EXPERTS, 1, D_EXPERT)
    b_dn = exp_b_down.reshape(n_layers, N_EXPERTS, 1, D_MODEL)

    w_in_l = [jnp.pad(w_in[l][:, in_cols], ((0, 0), (0, C_IN_PAD - C_IN))).astype(BF16) for l in range(depth)]
    b_in_l = [jnp.pad(b_in[l][in_cols], (0, C_IN_PAD - C_IN)).reshape(1, C_IN_PAD) for l in range(depth)]

    h = x.reshape(m, D_MODEL)
    p_rw, p_sw, p_ml = _in_proj(h, w_in_l[0], b_in_l[0])
    for l in range(depth):
        zero = jnp.zeros((RW_LORA_W, RW_DIM), F32)
        wcomb = jnp.concatenate([jnp.concatenate([rw_w_up[l], zero], axis=1),
                                 jnp.concatenate([zero, rw_a_up[l]], axis=1)], axis=0).astype(BF16)
        w_o = w_out[l]
        w_o_rw = w_o[0:RW_DIM].astype(BF16)
        w_o_sw = w_o[RW_DIM + sw_perm].astype(BF16)
        w_o_ml = w_o[RW_DIM + SW_DIM:].astype(BF16)
        rw_pad = jnp.pad(router_w[l], ((0, 0), (0, LANES - N_EXPERTS)))
        rw_hi = rw_pad.astype(BF16)
        rw_r1 = rw_pad - rw_hi.astype(F32)
        rw_mid = rw_r1.astype(BF16)
        rw_lo = (rw_r1 - rw_mid.astype(F32)).astype(BF16)
        rb_pad = jnp.pad(router_b[l], (0, LANES - N_EXPERTS), constant_values=NEG_BIG).reshape(1, LANES)

        r, lw, k, v, a, b, g = _rw_prep(p_rw, seq, row2(rw_shift_mu[l]), wcomb, rw_g_up[l].astype(BF16),
                                        row2(rw_w0[l]), row2(rw_a0[l]), row2(rw_k_k[l]), row2(rw_k_a[l]), ones_rw)
        m_mat, yh, g_mat, y0 = _rw_chunks(r, lw, k, v, a, b)
        y = _rw_scan(batch, seq, m_mat, yh, g_mat, y0).reshape(m, RW_DIM)
        rw_out = _rw_post(y, r, k, v, g, row2(rw_r_k[l]), row2(rw_ln_w[l]), row2(rw_ln_b[l]), ones_rw)
        sw_out = _swa(p_sw.reshape(batch, seq, SW_COLS), sw_sinks[l], cos_t, sin_t).reshape(m, SW_DIM)
        ml_out = _mlstm(p_ml.reshape(batch, seq, ML_COLS_PAD), ml_conv_w[l], row2(ml_conv_b[l]),
                        row2(ml_norm_w[l]), expand, ones_ml).reshape(m, ML_DIM)
        h1, h1_packed, idx_pad, gates = _out_proj(rw_out, sw_out, ml_out, h, w_o_rw, w_o_sw, w_o_ml, row2(ln1_w[l]),
                                                 row2(ln1_b[l]), (rw_hi, rw_mid, rw_lo), rb_pad)

        tok, slot_ext, block_e, z_rows = _routing_tables(idx_pad[:, 0:TOP_K])
        z = _experts(l, block_e, tok, slot_ext, h1_packed, exp_w_up, b_glu, b_lin, exp_w_down, b_dn, perm, z_rows)
        if l + 1 < depth:
            h, p_rw, p_sw, p_ml = _combine(z, gates, h1, row2(ln2_w[l]), row2(ln2_b[l]), w_in_l[l + 1], b_in_l[l + 1])
        else:
            h = _combine(z, gates, h1, row2(ln2_w[l]), row2(ln2_b[l]))
    return h.reshape(batch, seq, D_MODEL)
```

```python
import functools

import numpy as np
import jax
import jax.numpy as jnp
from jax import lax
from jax.experimental import pallas as pl
from jax.experimental.pallas import tpu as pltpu

F32 = jnp.float32
BF16 = jnp.bfloat16

D_MODEL = 1024
HEAD_DIM = 64
RW_HEADS = 6
SW_HEADS = 6
SW_KV_HEADS = 2
ML_HEADS = 4
RW_DIM = RW_HEADS * HEAD_DIM
SW_DIM = SW_HEADS * HEAD_DIM
SW_KV_DIM = SW_KV_HEADS * HEAD_DIM
ML_DIM = ML_HEADS * HEAD_DIM
RW_LORA_W = 64
RW_LORA_A = 64
RW_LORA_G = 128
RW_COLS = 3 * RW_DIM + RW_LORA_W + RW_LORA_A + RW_LORA_G
SW_COLS = SW_DIM + 2 * SW_KV_DIM
ML_COLS = 4 * ML_DIM + 2 * ML_HEADS
C_IN = RW_COLS + SW_COLS + ML_COLS
WINDOW = 128
ROPE_THETA = 10000.0
CONV_WIDTH = 4
N_EXPERTS = 32
TOP_K = 4
D_EXPERT = 1024
SWIGLU_ALPHA = 1.702
SWIGLU_LIMIT = 7.0
LN_EPS = 1e-5
RW_GN_EPS = 64e-5
ML_GN_EPS = 1e-6
DEPTH = 4
DN_ALPHA = (2 * DEPTH) ** 0.25

LANES = 128
SUBLANES = 8
ML_COLS_PAD = 1152
C_IN_PAD = RW_COLS + SW_COLS + ML_COLS_PAD
ROW_TILE = 512
RW_CHUNK = 64
RW_PAIR = 2 * HEAD_DIM
RW_NPAIR = RW_HEADS // 2
ML_CHUNK = 256
MOE_BLOCK = 512
NEG_BIG = -1e30
VMEM_LIMIT = 56 * 1024 * 1024


def _cparams(*sem):
    return pltpu.CompilerParams(dimension_semantics=sem, vmem_limit_bytes=VMEM_LIMIT)


def _dot(a, b):
    return jnp.dot(a.astype(BF16), b.astype(BF16), preferred_element_type=F32)


def _dot_nt(a, b):
    return lax.dot_general(a.astype(BF16), b.astype(BF16), (((1,), (1,)), ((), ())),
                           preferred_element_type=F32)


def _split3(x):
    hi = x.astype(BF16)
    r1 = x - hi.astype(F32)
    mid = r1.astype(BF16)
    lo = (r1 - mid.astype(F32)).astype(BF16)
    return hi, mid, lo


def _dot_xl(x, w):
    hi = x.astype(BF16)
    mid = (x - hi.astype(F32)).astype(BF16)
    return jnp.dot(hi, w, preferred_element_type=F32) + jnp.dot(mid, w, preferred_element_type=F32)


def _dot_lx(w, x):
    hi, mid, lo = _split3(x)
    return (jnp.dot(w, hi, preferred_element_type=F32) + jnp.dot(w, mid, preferred_element_type=F32)
            + jnp.dot(w, lo, preferred_element_type=F32))


def _softplus(z):
    return jnp.maximum(z, 0.0) + jnp.log(1.0 + jnp.exp(-jnp.abs(z)))


def _sigmoid(z):
    return 1.0 / (1.0 + jnp.exp(-z))


def _shift_rows(x, prev8, j):
    cat = jnp.concatenate([prev8, x], axis=0)
    return pltpu.roll(cat, j, 0)[SUBLANES:, :]


def _layer_norm(x, w, b):
    mu = jnp.mean(x, axis=-1, keepdims=True)
    d = x - mu
    var = jnp.mean(d * d, axis=-1, keepdims=True)
    return d * lax.rsqrt(var + LN_EPS) * w + b


def _pack_bf16_pairs(x):
    c = x.shape[1] // 2
    bits = pltpu.bitcast(x.astype(BF16).astype(F32), jnp.uint32)
    return lax.shift_right_logical(bits[:, 0:c], jnp.uint32(16)) | (bits[:, c:2 * c] & jnp.uint32(0xFFFF0000))


def _unpack_bf16_pairs(w):
    lo = pltpu.bitcast(lax.shift_left(w, jnp.uint32(16)), F32)
    hi = pltpu.bitcast(w & jnp.uint32(0xFFFF0000), F32)
    return jnp.concatenate([lo, hi], axis=1).astype(BF16)


def _head_norm(y, ones_bd, eps):
    inv = 1.0 / HEAD_DIM
    mu = _dot_xl(y, ones_bd) * inv
    d = y - mu
    var = _dot_xl(d * d, ones_bd) * inv
    return d * lax.rsqrt(var + eps)


def _inproj_kernel(x_ref, w_ref, b_ref, rw_ref, sw_ref, ml_ref):
    _project_in(x_ref[...], w_ref, b_ref, rw_ref, sw_ref, ml_ref)


def _project_in(x, w_ref, b_ref, rw_ref, sw_ref, ml_ref):
    x = x.astype(BF16)
    c1, c2 = RW_COLS, RW_COLS + SW_COLS
    rw_ref[...] = jnp.dot(x, w_ref[:, 0:c1], preferred_element_type=F32) + b_ref[:, 0:c1]
    sw_ref[...] = jnp.dot(x, w_ref[:, c1:c2], preferred_element_type=F32) + b_ref[:, c1:c2]
    ml_ref[...] = jnp.dot(x, w_ref[:, c2:C_IN_PAD], preferred_element_type=F32) + b_ref[:, c2:C_IN_PAD]


def _in_proj(h2d, w, b):
    m = h2d.shape[0]
    tm = ROW_TILE
    row = lambda i: (i, 0)
    fixed = lambda i: (0, 0)
    return pl.pallas_call(
        _inproj_kernel,
        grid=(m // tm,),
        in_specs=[pl.BlockSpec((tm, D_MODEL), row), pl.BlockSpec((D_MODEL, C_IN_PAD), fixed),
                  pl.BlockSpec((1, C_IN_PAD), fixed)],
        out_specs=[pl.BlockSpec((tm, RW_COLS), row), pl.BlockSpec((tm, SW_COLS), row),
                   pl.BlockSpec((tm, ML_COLS_PAD), row)],
        out_shape=[jax.ShapeDtypeStruct((m, RW_COLS), F32), jax.ShapeDtypeStruct((m, SW_COLS), F32),
                   jax.ShapeDtypeStruct((m, ML_COLS_PAD), F32)],
        compiler_params=_cparams("parallel"),
        name="in_proj",
    )(h2d, w, b)


def _rw_prep_kernel(tiles_per_seq, p_ref, pprev_ref, mu_ref, wcomb_ref, gup_ref, w0_ref, a0_ref, kk_ref,
                    ka_ref, ones_ref, r_o, lw_o, k_o, v_o, a_o, b_o, g_o):
    i = pl.program_id(0)
    x = p_ref[...]
    first = (i % tiles_per_seq) == 0
    prev8 = jnp.where(first, 0.0, pprev_ref[...])
    xs = x + (_shift_rows(x, prev8, 1) - x) * mu_ref[...]
    d = RW_DIM
    r = xs[:, 0:d]
    k = xs[:, d:2 * d]
    v = xs[:, 2 * d:3 * d]
    slab = xs[:, 3 * d:3 * d + LANES]
    lane = lax.broadcasted_iota(jnp.int32, slab.shape, 1)
    slab = jnp.where(lane < RW_LORA_W, jnp.tanh(slab), slab)
    wa = _dot(slab, wcomb_ref[...])
    w_pre = w0_ref[...] + wa[:, 0:d]
    a_pre = a0_ref[...] + wa[:, d:2 * d]
    w_log = -_softplus(-w_pre) - 0.5
    log_decay = -jnp.exp(w_log)
    a_sig = _sigmoid(a_pre)
    g = _dot(_sigmoid(xs[:, 3 * d + LANES:3 * d + 2 * LANES]), gup_ref[...])
    kk = k * kk_ref[...]
    ss = _dot_xl(kk * kk, ones_ref[...])
    kk = kk / jnp.maximum(jnp.sqrt(ss), 1e-12)
    r_o[...] = r
    lw_o[...] = log_decay
    k_o[...] = k * (1.0 + (a_sig - 1.0) * ka_ref[...])
    v_o[...] = v
    a_o[...] = -kk
    b_o[...] = kk * a_sig
    g_o[...] = g


def _rw_prep(p_rw, seq, mu, wcomb, gup, w0, a0, k_k, k_a, ones_bd):
    m = p_rw.shape[0]
    tm = ROW_TILE
    row = lambda i: (i, 0)
    fixed = lambda i: (0, 0)
    prev = lambda i: (jnp.maximum(i * (tm // SUBLANES) - 1, 0), 0)
    vec = pl.BlockSpec((1, RW_DIM), fixed)
    out = jax.ShapeDtypeStruct((m, RW_DIM), F32)
    return pl.pallas_call(
        functools.partial(_rw_prep_kernel, seq // tm),
        grid=(m // tm,),
        in_specs=[pl.BlockSpec((tm, RW_COLS), row), pl.BlockSpec((SUBLANES, RW_COLS), prev),
                  pl.BlockSpec((1, RW_COLS), fixed), pl.BlockSpec((LANES, 2 * RW_DIM), fixed),
                  pl.BlockSpec((RW_LORA_G, RW_DIM), fixed), vec, vec, vec, vec,
                  pl.BlockSpec((RW_DIM, RW_DIM), fixed)],
        out_specs=[pl.BlockSpec((tm, RW_DIM), row)] * 7,
        out_shape=[out] * 7,
        compiler_params=_cparams("parallel"),
        name="rw_prep",
    )(p_rw, p_rw, mu, wcomb, gup, w0, a0, k_k, k_a, ones_bd)


def _rw_chunk_kernel(n_chunk, r_ref, lw_ref, k_ref, v_ref, a_ref, b_ref, m_o, yh_o, g_o, y0_o):
    L = RW_CHUNK
    ti = lax.broadcasted_iota(jnp.int32, (L, L), 0)
    si = lax.broadcasted_iota(jnp.int32, (L, L), 1)
    tri = jnp.where(ti >= si, 1.0, 0.0).astype(BF16)
    n2 = 2 * L
    ri = lax.broadcasted_iota(jnp.int32, (n2, n2), 0)
    ci = lax.broadcasted_iota(jnp.int32, (n2, n2), 1)
    rr = jnp.where(ri >= L, ri - L, ri)
    cc = jnp.where(ci >= L, ci - L, ci)
    strict = rr > cc
    incl = rr >= cc
    eye = ri == ci
    eye_f = jnp.where(eye, 1.0, 0.0)
    lane = lax.broadcasted_iota(jnp.int32, (L, RW_PAIR), 1)
    head0 = lane < HEAD_DIM

    def bmm(x, y):
        return jnp.einsum('bmk,bkn->bmn', x.astype(BF16), y.astype(BF16), preferred_element_type=F32)

    def bmm_nt(x, y):
        return jnp.einsum('bmk,bnk->bmn', x.astype(BF16), y.astype(BF16), preferred_element_type=F32)

    def bmm_tn(x, y):
        return bmm(jnp.swapaxes(x, 1, 2), y)

    def stack(x):
        return jnp.concatenate([jnp.where(head0, x, 0.0), jnp.where(head0, 0.0, x)], axis=1)

    def chunked(ref, cols):
        return ref[:, cols].reshape(n_chunk, L, RW_PAIR)

    tri_b = jnp.broadcast_to(tri, (n_chunk, L, L))
    for p in range(RW_NPAIR):
        cols = slice(p * RW_PAIR, (p + 1) * RW_PAIR)
        lw = chunked(lw_ref, cols)
        hi, mid, lo = _split3(lw)
        tsum = lambda part: jnp.einsum('bts,bsn->btn', tri_b, part, preferred_element_type=F32)
        lc = tsum(hi) + tsum(mid) + tsum(lo)
        lcl = lc[:, L - 1:L, :]
        gam = jnp.exp(lc)
        gam_prev = jnp.exp(lc - lw)
        inv = jnp.exp(-lc)
        to_end = jnp.exp(lcl - lc)
        r = chunked(r_ref, cols)
        k = chunked(k_ref, cols)
        v = chunked(v_ref, cols)
        a = chunked(a_ref, cols)
        b = chunked(b_ref, cols)
        x1 = stack(a * gam_prev)
        r1 = stack(r * gam)
        x2 = stack(b * inv)
        k2 = stack(k * inv)
        vs = stack(v)
        bs = stack(b * to_end)
        ks = stack(k * to_end)
        aa = bmm_nt(jnp.concatenate([x1, r1], axis=1), jnp.concatenate([x2, k2], axis=1))
        a_ab = jnp.where(strict, aa[:, 0:n2, 0:n2], 0.0)
        a_ak = jnp.where(strict, aa[:, 0:n2, n2:2 * n2], 0.0)
        a_rb = jnp.where(incl, aa[:, n2:2 * n2, 0:n2], 0.0)
        a_rk = jnp.where(incl, aa[:, n2:2 * n2, n2:2 * n2], 0.0)
        apow = a_ab
        t_inv = eye_f + a_ab
        for _ in range(5):
            apow = bmm(apow, apow)
            t_inv = t_inv + bmm(t_inv, apow)
        akv = bmm(a_ak, vs)
        pw = bmm(t_inv, jnp.concatenate([x1, akv], axis=2))
        pmat = pw[:, :, 0:RW_PAIR]
        wmat = pw[:, :, RW_PAIR:2 * RW_PAIR]
        gl_row = jnp.exp(lcl)
        m_mat = bmm_tn(bs, pmat) + jnp.where(eye, gl_row, 0.0)
        g_mat = bmm_tn(jnp.concatenate([bs, ks], axis=1), jnp.concatenate([wmat, vs], axis=1))
        yy = bmm(a_rb, pw)
        yh = r1 + yy[:, :, 0:RW_PAIR]
        y0 = yy[:, :, RW_PAIR:2 * RW_PAIR] + bmm(a_rk, vs)
        m_o[:, p] = m_mat.astype(BF16)
        yh_o[:, p] = yh.astype(BF16)
        g_o[:, p] = g_mat
        y0_o[:, p] = y0


def _rw_chunks(r, lw, k, v, a, b):
    m = r.shape[0]
    tm = ROW_TILE
    nc = tm // RW_CHUNK
    row = lambda i: (i, 0)
    blk = lambda i: (i, 0, 0, 0)
    n2 = 2 * RW_CHUNK
    n_tot = m // RW_CHUNK
    ospec = pl.BlockSpec((nc, RW_NPAIR, n2, RW_PAIR), blk)
    return pl.pallas_call(
        functools.partial(_rw_chunk_kernel, nc),
        grid=(m // tm,),
        in_specs=[pl.BlockSpec((tm, RW_DIM), row)] * 6,
        out_specs=[ospec] * 4,
        out_shape=[jax.ShapeDtypeStruct((n_tot, RW_NPAIR, n2, RW_PAIR), BF16),
                   jax.ShapeDtypeStruct((n_tot, RW_NPAIR, n2, RW_PAIR), BF16),
                   jax.ShapeDtypeStruct((n_tot, RW_NPAIR, n2, RW_PAIR), F32),
                   jax.ShapeDtypeStruct((n_tot, RW_NPAIR, n2, RW_PAIR), F32)],
        compiler_params=_cparams("parallel"),
        name="rw_chunks",
    )(r, lw, k, v, a, b)


def _rw_scan_kernel(batch, m_ref, yh_ref, g_ref, y0_ref, r_ref, k_ref, v_ref, gate_ref, rk_ref, lnw_ref, lnb_ref,
                    ones_ref, o_ref, h_ref, y_s):
    c = pl.program_id(0)

    @pl.when(c == 0)
    def _():
        h_ref[...] = jnp.zeros_like(h_ref)

    L = RW_CHUNK
    for bi in range(batch):
        for p in range(RW_NPAIR):
            hb = h_ref[bi, p].astype(BF16)
            yrows = jnp.dot(yh_ref[bi, 0, p], hb, preferred_element_type=F32) + y0_ref[bi, 0, p]
            y_s[bi, :, p * RW_PAIR:(p + 1) * RW_PAIR] = yrows[0:L, :] + yrows[L:2 * L, :]
            h_ref[bi, p] = jnp.dot(m_ref[bi, 0, p], hb, preferred_element_type=F32) + g_ref[bi, 0, p]

    rows = batch * L
    flat = lambda ref: ref[...].reshape(rows, RW_DIM)
    ones_bd = ones_ref[...]
    y = _head_norm(flat(y_s), ones_bd, RW_GN_EPS) * lnw_ref[...] + lnb_ref[...]
    v = flat(v_ref)
    bonus = _dot_xl(flat(r_ref) * flat(k_ref) * rk_ref[...], ones_bd) * v
    o_ref[...] = ((y + bonus) * flat(gate_ref)).reshape(batch, L, RW_DIM)


def _rw_scan(batch, seq, m_mat, yh, g_mat, y0, r, k, v, g, r_k, ln_w, ln_b, ones_bd):
    nc = seq // RW_CHUNK
    n2 = 2 * RW_CHUNK
    shp = (batch, nc, RW_NPAIR, n2, RW_PAIR)
    chunk_args = [t.reshape(shp) for t in (m_mat, yh, g_mat, y0)]
    row_args = [t.reshape(batch, seq, RW_DIM) for t in (r, k, v, g)]
    ispec = pl.BlockSpec((batch, 1, RW_NPAIR, n2, RW_PAIR), lambda c: (0, c, 0, 0, 0))
    rspec = pl.BlockSpec((batch, RW_CHUNK, RW_DIM), lambda c: (0, c, 0))
    vec = pl.BlockSpec((1, RW_DIM), lambda c: (0, 0))
    return pl.pallas_call(
        functools.partial(_rw_scan_kernel, batch),
        grid=(nc,),
        in_specs=[ispec] * 4 + [rspec] * 4 + [vec, vec, vec, pl.BlockSpec((RW_DIM, RW_DIM), lambda c: (0, 0))],
        out_specs=rspec,
        out_shape=jax.ShapeDtypeStruct((batch, seq, RW_DIM), F32),
        scratch_shapes=[pltpu.VMEM((batch, RW_NPAIR, RW_PAIR, RW_PAIR), F32),
                        pltpu.VMEM((batch, RW_CHUNK, RW_DIM), F32)],
        compiler_params=_cparams("arbitrary"),
        name="rw_scan_post",
    )(*chunk_args, *row_args, r_k, ln_w, ln_b, ones_bd)


def _rope(x, cos, sin_signed):
    lane = lax.broadcasted_iota(jnp.int32, x.shape, 1)
    half = HEAD_DIM // 2
    first_half = (lane % HEAD_DIM) < half
    rot = jnp.where(first_half, pltpu.roll(x, LANES - half, 1), pltpu.roll(x, half, 1))
    return x * cos + rot * sin_signed


def _swa_kernel(sink_ref, q_ref, kc_ref, vc_ref, kp_ref, vp_ref, cosc_ref, sinc_ref, cosp_ref, sinp_ref, o_ref):
    j = pl.program_id(1)
    W = WINDOW
    cos_c, sin_c = cosc_ref[...], sinc_ref[...]
    k_cat = jnp.concatenate([_rope(kp_ref[0], cosp_ref[...], sinp_ref[...]), _rope(kc_ref[0], cos_c, sin_c)], axis=0)
    v_cat = jnp.concatenate([vp_ref[0], vc_ref[0]], axis=0).astype(BF16)
    k_cat = k_cat.astype(BF16)
    qi = lax.broadcasted_iota(jnp.int32, (W, 2 * W), 0)
    kj = lax.broadcasted_iota(jnp.int32, (W, 2 * W), 1)
    lo = jnp.where(j > 0, qi, jnp.maximum(qi, W - 1))
    bias = jnp.where((kj > lo) & (kj <= qi + W), 0.0, NEG_BIG)
    lane = lax.broadcasted_iota(jnp.int32, (W, LANES), 1)
    kv0 = lane < HEAD_DIM
    scale = HEAD_DIM ** -0.5
    group = SW_HEADS // SW_KV_HEADS
    for g in range(group):
        q = _rope(q_ref[0, :, g * LANES:(g + 1) * LANES], cos_c, sin_c) * scale
        outs = []
        for kv in range(SW_KV_HEADS):
            qm = jnp.where(kv0, q, 0.0) if kv == 0 else jnp.where(kv0, 0.0, q)
            s = _dot_nt(qm, k_cat) + bias
            sink = sink_ref[kv * group + g]
            mx = jnp.maximum(jnp.max(s, axis=-1, keepdims=True), sink)
            pr = jnp.exp(s - mx)
            den = jnp.sum(pr, axis=-1, keepdims=True) + jnp.exp(sink - mx)
            outs.append(jnp.dot(pr.astype(BF16), v_cat, preferred_element_type=F32) / den)
        o_ref[0, :, g * LANES:(g + 1) * LANES] = jnp.where(kv0, outs[0], outs[1])


def _swa(p_sw3, sinks, cos_t, sin_t):
    batch, seq, _ = p_sw3.shape
    W = WINDOW
    nb = seq // W
    kcol = SW_DIM // LANES
    cur = lambda c: (lambda b, j: (b, j, c))
    prv = lambda c: (lambda b, j: (b, jnp.maximum(j - 1, 0), c))
    tab_c = pl.BlockSpec((W, LANES), lambda b, j: (j, 0))
    tab_p = pl.BlockSpec((W, LANES), lambda b, j: (jnp.maximum(j - 1, 0), 0))
    kv_blk = lambda f: pl.BlockSpec((1, W, LANES), f)
    return pl.pallas_call(
        _swa_kernel,
        grid=(batch, nb),
        in_specs=[pl.BlockSpec(memory_space=pltpu.SMEM),
                  pl.BlockSpec((1, W, SW_DIM), cur(0)),
                  kv_blk(cur(kcol)), kv_blk(cur(kcol + 1)), kv_blk(prv(kcol)), kv_blk(prv(kcol + 1)),
                  tab_c, tab_c, tab_p, tab_p],
        out_specs=pl.BlockSpec((1, W, SW_DIM), cur(0)),
        out_shape=jax.ShapeDtypeStruct((batch, seq, SW_DIM), F32),
        compiler_params=_cparams("parallel", "parallel"),
        name="swa",
    )(sinks, p_sw3, p_sw3, p_sw3, p_sw3, p_sw3, cos_t, sin_t, cos_t, sin_t)


def _mlstm_kernel(p_ref, pprev_ref, cw_ref, cb_ref, nw_ref, expand_ref, ones_ref, o_ref, c_ref, nv_ref, m_ref):
    j = pl.program_id(1)
    L = ML_CHUNK
    d = ML_DIM

    @pl.when(j == 0)
    def _():
        c_ref[...] = jnp.zeros_like(c_ref)
        nv_ref[...] = jnp.zeros_like(nv_ref)
        m_ref[...] = jnp.zeros_like(m_ref)

    x = p_ref[0]
    qk_pre = x[:, 0:2 * d]
    prev8 = jnp.where(j == 0, 0.0, pprev_ref[0][:, 0:2 * d])
    conv = cb_ref[...] + cw_ref[CONV_WIDTH - 1:CONV_WIDTH, :] * qk_pre
    for s in range(1, CONV_WIDTH):
        conv = conv + cw_ref[CONV_WIDTH - 1 - s:CONV_WIDTH - s, :] * _shift_rows(qk_pre, prev8, s)
    qk = conv * _sigmoid(conv)
    q = qk[:, 0:d]
    k = qk[:, d:2 * d] * (HEAD_DIM ** -0.5)
    v = x[:, 2 * d:3 * d]
    o_gate = x[:, 3 * d:4 * d]
    gates = _dot_xl(x[:, 4 * d:4 * d + LANES], expand_ref[...])
    i_full = gates[:, 0:d]
    f_full = gates[:, d:2 * d]
    lf_full = jnp.minimum(f_full, 0.0) - jnp.log(1.0 + jnp.exp(-jnp.abs(f_full)))
    ti = lax.broadcasted_iota(jnp.int32, (L, L), 0)
    si = lax.broadcasted_iota(jnp.int32, (L, L), 1)
    causal = ti >= si
    tri = jnp.where(causal, 1.0, 0.0).astype(BF16)
    bc_full = _dot_lx(tri, lf_full)
    u_t = (i_full - bc_full).T
    lane = lax.broadcasted_iota(jnp.int32, (L, d), 1)
    lane_row = lax.broadcasted_iota(jnp.int32, (1, d), 1)
    q_b = q.astype(BF16)
    k_b = k.astype(BF16)
    m_prev_row = m_ref[...]
    nv_row = nv_ref[...]
    qn = q * nv_row
    num = jnp.zeros((L, d), F32)
    inter_full = jnp.zeros((L, d), F32)
    den_full = jnp.zeros((L, d), F32)
    mnew_row = jnp.zeros((1, d), F32)
    for h in range(ML_HEADS):
        c0 = h * HEAD_DIM
        in_head = (lane >= c0) & (lane < c0 + HEAD_DIM)
        in_head_row = (lane_row >= c0) & (lane_row < c0 + HEAD_DIM)
        bc_col = bc_full[:, c0:c0 + 1]
        dmat = jnp.where(causal, bc_col + u_t[c0:c0 + 1, :], NEG_BIG)
        m_inter = bc_col + m_prev_row[:, c0:c0 + 1]
        m_t = jnp.maximum(m_inter, jnp.max(dmat, axis=-1, keepdims=True))
        inter = jnp.exp(m_inter - m_t)
        e = jnp.exp(dmat - m_t)
        sm = _dot_nt(jnp.where(in_head, q, 0.0), k_b) * e
        num = num + _dot(sm, jnp.where(in_head, v, 0.0))
        nq = jnp.sum(sm, axis=-1, keepdims=True) + inter * jnp.sum(jnp.where(in_head, qn, 0.0), axis=-1, keepdims=True)
        den = jnp.maximum(jnp.abs(nq), jnp.exp(-m_t))
        inter_full = jnp.where(in_head, inter, inter_full)
        den_full = jnp.where(in_head, den, den_full)
        mnew_row = jnp.where(in_head_row, m_t[L - 1:L, :], mnew_row)
    c_mat = c_ref[...]
    num = num + inter_full * jnp.dot(q_b, c_mat.astype(BF16), preferred_element_type=F32)
    hout = num / den_full
    hn = _head_norm(hout, ones_ref[...], ML_GN_EPS) * nw_ref[...]
    o_ref[0] = _sigmoid(o_gate) * hn
    bcl_row = bc_full[L - 1:L, :]
    wst = jnp.exp(bcl_row - bc_full + i_full - mnew_row)
    dec_row = jnp.exp(bcl_row + m_prev_row - mnew_row)
    kw = k * wst
    ri = lax.broadcasted_iota(jnp.int32, (d, d), 0) // HEAD_DIM
    ci = lax.broadcasted_iota(jnp.int32, (d, d), 1) // HEAD_DIM
    c_ref[...] = dec_row * c_mat + jnp.where(ri == ci, _dot(kw.T, v), 0.0)
    nv_ref[...] = dec_row * nv_row + jnp.sum(kw, axis=0, keepdims=True)
    m_ref[...] = mnew_row


def _mlstm(p_ml3, conv_w, conv_b, norm_w, expand, ones_bd):
    batch, seq, _ = p_ml3.shape
    L = ML_CHUNK
    fixed = lambda b, j: (0, 0)
    return pl.pallas_call(
        _mlstm_kernel,
        grid=(batch, seq // L),
        in_specs=[pl.BlockSpec((1, L, ML_COLS_PAD), lambda b, j: (b, j, 0)),
                  pl.BlockSpec((1, SUBLANES, ML_COLS_PAD), lambda b, j: (b, jnp.maximum(j * (L // SUBLANES) - 1, 0), 0)),
                  pl.BlockSpec((CONV_WIDTH, 2 * ML_DIM), fixed), pl.BlockSpec((1, 2 * ML_DIM), fixed),
                  pl.BlockSpec((1, ML_DIM), fixed), pl.BlockSpec((LANES, 2 * ML_DIM), fixed),
                  pl.BlockSpec((ML_DIM, ML_DIM), fixed)],
        out_specs=pl.BlockSpec((1, L, ML_DIM), lambda b, j: (b, j, 0)),
        out_shape=jax.ShapeDtypeStruct((batch, seq, ML_DIM), F32),
        scratch_shapes=[pltpu.VMEM((ML_DIM, ML_DIM), F32), pltpu.VMEM((1, ML_DIM), F32),
                        pltpu.VMEM((1, ML_DIM), F32)],
        compiler_params=_cparams("parallel", "arbitrary"),
        name="mlstm",
    )(p_ml3, p_ml3, conv_w, conv_b, norm_w, expand, ones_bd)


def _outproj_kernel(rw_ref, sw_ref, ml_ref, h_ref, wrw_ref, wsw_ref, wml_ref, lnw_ref, lnb_ref,
                    rwh_ref, rwm_ref, rwl_ref, rb_ref, h1_o, h1p_o, idx_o, gate_o):
    mix = (_dot(rw_ref[...], wrw_ref[...]) + _dot(sw_ref[...], wsw_ref[...]) + _dot(ml_ref[...], wml_ref[...]))
    h1 = _layer_norm(DN_ALPHA * h_ref[...] + mix, lnw_ref[...], lnb_ref[...])
    h1_o[...] = h1
    h1p_o[...] = _pack_bf16_pairs(h1)
    xh, xm, xl = _split3(h1)
    wh, wm, wl = rwh_ref[...], rwm_ref[...], rwl_ref[...]
    dd = lambda a, b: jnp.dot(a, b, preferred_element_type=F32)
    logits = (dd(xh, wh) + (dd(xh, wm) + dd(xm, wh)) + (dd(xh, wl) + dd(xm, wm) + dd(xl, wh))) + rb_ref[...]
    lane = lax.broadcasted_iota(jnp.int32, logits.shape, 1).astype(F32)
    vals, idxs = [], []
    cur = logits
    for _ in range(TOP_K):
        mx = jnp.max(cur, axis=-1, keepdims=True)
        ix = jnp.min(jnp.where(cur == mx, lane, float(LANES)), axis=-1, keepdims=True)
        vals.append(mx)
        idxs.append(ix)
        cur = jnp.where(lane == ix, NEG_BIG * 2.0, cur)
    es = [jnp.exp(vv - vals[0]) for vv in vals]
    den = es[0] + es[1] + es[2] + es[3]
    idx_out = jnp.zeros_like(logits)
    gate_out = jnp.zeros_like(logits)
    for kk in range(TOP_K):
        idx_out = jnp.where(lane == float(kk), idxs[kk], idx_out)
        gate_out = jnp.where(lane == float(kk), es[kk] / den, gate_out)
    idx_o[...] = idx_out.astype(jnp.int32)
    gate_o[...] = gate_out


def _out_proj(rw_out, sw_out, ml_out, h2d, w_rw, w_sw, w_ml, ln_w, ln_b, rw_parts, rb):
    m = h2d.shape[0]
    tm = ROW_TILE
    row = lambda i: (i, 0)
    fixed = lambda i: (0, 0)
    full = lambda a: pl.BlockSpec(a.shape, fixed)
    return pl.pallas_call(
        _outproj_kernel,
        grid=(m // tm,),
        in_specs=[pl.BlockSpec((tm, RW_DIM), row), pl.BlockSpec((tm, SW_DIM), row), pl.BlockSpec((tm, ML_DIM), row),
                  pl.BlockSpec((tm, D_MODEL), row), full(w_rw), full(w_sw), full(w_ml), full(ln_w), full(ln_b),
                  full(rw_parts[0]), full(rw_parts[1]), full(rw_parts[2]), full(rb)],
        out_specs=[pl.BlockSpec((tm, D_MODEL), row), pl.BlockSpec((tm, D_MODEL // 2), row),
                   pl.BlockSpec((tm, LANES), row), pl.BlockSpec((tm, LANES), row)],
        out_shape=[jax.ShapeDtypeStruct((m, D_MODEL), F32), jax.ShapeDtypeStruct((m, D_MODEL // 2), jnp.uint32),
                   jax.ShapeDtypeStruct((m, LANES), jnp.int32), jax.ShapeDtypeStruct((m, LANES), F32)],
        compiler_params=_cparams("parallel"),
        name="out_proj_router",
    )(rw_out, sw_out, ml_out, h2d, w_rw, w_sw, w_ml, ln_w, ln_b, *rw_parts, rb)


def _expert_kernel(n_blocks, layer, be_ref, tokc_ref, tokn_ref, slot_ref, h_hbm, wup_ref, bg_ref, bl_ref,
                   wdn_ref, bd_ref, perm_ref, z_hbm, x0, x1, y0, y1, wg_s, wl_s, wd_s, gsem, ssem):
    del layer
    i = pl.program_id(0)
    bm = MOE_BLOCK

    def gather_row(tok, xdst, sem, r):
        return pltpu.make_async_copy(h_hbm.at[pl.ds(tok, 1)], xdst.at[pl.ds(r, 1)], sem)

    def scatter_row(dst, ysrc, sem, r):
        return pltpu.make_async_copy(ysrc.at[pl.ds(r, 1)], z_hbm.at[pl.ds(dst, 1)], sem)

    def wait_gather(xdst, sem):
        pltpu.make_async_copy(h_hbm.at[pl.ds(0, bm)], xdst, sem).wait()

    def wait_scatter(ysrc, sem):
        pltpu.make_async_copy(ysrc, z_hbm.at[pl.ds(0, bm)], sem).wait()

    @pl.when(i == 0)
    def _():
        y1[...] = jnp.zeros(y1.shape, F32)

        def prime(r, c):
            gather_row(tokc_ref[r], x0, gsem.at[0], r).start()
            return c

        lax.fori_loop(0, bm, prime, 0, unroll=8)

    n_used = be_ref[n_blocks]
    live = i <= n_used
    blk = jnp.minimum(i, n_blocks - 1)
    e_now = be_ref[blk]
    e_prev = be_ref[jnp.maximum(blk - 1, 0)]

    @pl.when(((i == 0) | (e_now != e_prev)) & live)
    def _():
        perm = perm_ref[...]
        for c in range(D_EXPERT // LANES):
            t = jnp.dot(wup_ref[0, 0, :, c * 2 * LANES:(c + 1) * 2 * LANES].astype(BF16), perm,
                        preferred_element_type=F32)
            wg_s[:, c * LANES:(c + 1) * LANES] = t[:, 0:LANES].astype(BF16)
            wl_s[:, c * LANES:(c + 1) * LANES] = t[:, LANES:2 * LANES].astype(BF16)
        wd_s[...] = wdn_ref[0, 0].astype(BF16)

    def step(x_cur, x_nxt, y_cur, y_nxt, g_cur, g_nxt, s_cur, s_nxt):
        wait_gather(x_cur, g_cur)
        for r in range(bm):
            gather_row(tokn_ref[r], x_nxt, g_nxt, r).start(priority=r % 2)
            scatter_row(slot_ref[r], y_nxt, s_nxt, r).start(priority=(r + 1) % 2)

        @pl.when(i > 0)
        def _():
            wait_scatter(y_cur, s_cur)

        x = _unpack_bf16_pairs(x_cur[...])
        hg = jnp.dot(x, wg_s[...], preferred_element_type=F32) + bg_ref[0, 0]
        hl = jnp.dot(x, wl_s[...], preferred_element_type=F32) + bl_ref[0, 0]
        hg = jnp.minimum(hg, SWIGLU_LIMIT)
        hl = jnp.clip(hl, -SWIGLU_LIMIT, SWIGLU_LIMIT)
        act = hg * _sigmoid(SWIGLU_ALPHA * hg) * (hl + 1.0)
        y_cur[...] = jnp.dot(act.astype(BF16), wd_s[...], preferred_element_type=F32) + bd_ref[0, 0]

        @pl.when(i == n_used)
        def _():
            wait_gather(x_nxt, g_nxt)
            wait_scatter(y_nxt, s_nxt)

    @pl.when(i > n_used)
    def _():
        @pl.when(i == n_used + 1)
        def _():
            y0[...] = jnp.zeros(y0.shape, F32)

        fill = pltpu.make_async_copy(y0, z_hbm.at[pl.ds(pl.multiple_of((i - 1) * bm, bm), bm)], ssem.at[0])
        fill.start()
        fill.wait()

    @pl.when((i % 2 == 0) & live)
    def _():
        step(x0, x1, y0, y1, gsem.at[0], gsem.at[1], ssem.at[0], ssem.at[1])

    @pl.when((i % 2 == 1) & live)
    def _():
        step(x1, x0, y1, y0, gsem.at[1], gsem.at[0], ssem.at[1], ssem.at[0])


def _experts(layer, block_e, tok, slot_ext, h1, w_up, b_glu, b_lin, w_down, b_down, perm, z_rows):
    bm = MOE_BLOCK
    n_blocks = tok.shape[0] // bm
    last = n_blocks - 1
    by_e = lambda i, be: (layer, be[jnp.minimum(i, last)], 0, 0)
    smem = lambda f: pl.BlockSpec((bm,), f, memory_space=pltpu.SMEM)
    grid_spec = pltpu.PrefetchScalarGridSpec(
        num_scalar_prefetch=1,
        grid=(n_blocks + 1,),
        in_specs=[smem(lambda i, be: (jnp.minimum(i, last),)), smem(lambda i, be: (jnp.minimum(i + 1, last),)),
                  smem(lambda i, be: (i,)),
                  pl.BlockSpec(memory_space=pl.ANY),
                  pl.BlockSpec((1, 1, D_MODEL, 2 * D_EXPERT), by_e),
                  pl.BlockSpec((1, 1, 1, D_EXPERT), by_e), pl.BlockSpec((1, 1, 1, D_EXPERT), by_e),
                  pl.BlockSpec((1, 1, D_EXPERT, D_MODEL), by_e), pl.BlockSpec((1, 1, 1, D_MODEL), by_e),
                  pl.BlockSpec((2 * LANES, 2 * LANES), lambda i, be: (0, 0))],
        out_specs=pl.BlockSpec(memory_space=pl.ANY),
        scratch_shapes=[pltpu.VMEM((bm, D_MODEL // 2), jnp.uint32), pltpu.VMEM((bm, D_MODEL // 2), jnp.uint32),
                        pltpu.VMEM((bm, D_MODEL), F32), pltpu.VMEM((bm, D_MODEL), F32),
                        pltpu.VMEM((D_MODEL, D_EXPERT), BF16), pltpu.VMEM((D_MODEL, D_EXPERT), BF16),
                        pltpu.VMEM((D_EXPERT, D_MODEL), BF16),
                        pltpu.SemaphoreType.DMA((2,)), pltpu.SemaphoreType.DMA((2,))],
    )
    return pl.pallas_call(
        functools.partial(_expert_kernel, n_blocks, layer),
        grid_spec=grid_spec,
        out_shape=jax.ShapeDtypeStruct((z_rows, D_MODEL), F32),
        compiler_params=pltpu.CompilerParams(dimension_semantics=("arbitrary",), vmem_limit_bytes=VMEM_LIMIT,
                                             disable_bounds_checks=True),
        name="experts",
    )(block_e, tok, tok, slot_ext, h1, w_up, b_glu, b_lin, w_down, b_down, perm)


def _combine_kernel(z0_ref, z1_ref, z2_ref, z3_ref, gate_ref, h_ref, lnw_ref, lnb_ref, *rest):
    gates = gate_ref[...]
    y = gates[:, 0:1] * z0_ref[...]
    for kk, z_ref in enumerate((z1_ref, z2_ref, z3_ref), start=1):
        y = y + gates[:, kk:kk + 1] * z_ref[...]
    h = _layer_norm(DN_ALPHA * h_ref[...] + y, lnw_ref[...], lnb_ref[...])
    if len(rest) == 1:
        rest[0][...] = h
    else:
        w_ref, b_ref, o_ref, rw_ref, sw_ref, ml_ref = rest
        o_ref[...] = h
        _project_in(h, w_ref, b_ref, rw_ref, sw_ref, ml_ref)


def _combine(z, gates, h1, ln_w, ln_b, w_next=None, b_next=None):
    m = h1.shape[0]
    fused = w_next is not None
    tm = ROW_TILE // 2 if fused else ROW_TILE
    n_tiles = m // tm
    row = lambda i: (i, 0)
    fixed = lambda i: (0, 0)
    choice = lambda kk: pl.BlockSpec((tm, D_MODEL), lambda i: (kk * n_tiles + i, 0))
    in_specs = [choice(0), choice(1), choice(2), choice(3), pl.BlockSpec((tm, LANES), row),
                pl.BlockSpec((tm, D_MODEL), row), pl.BlockSpec((1, D_MODEL), fixed), pl.BlockSpec((1, D_MODEL), fixed)]
    out_specs = [pl.BlockSpec((tm, D_MODEL), row)]
    out_shape = [jax.ShapeDtypeStruct((m, D_MODEL), F32)]
    args = [z, z, z, z, gates, h1, ln_w, ln_b]
    if fused:
        in_specs += [pl.BlockSpec((D_MODEL, C_IN_PAD), fixed), pl.BlockSpec((1, C_IN_PAD), fixed)]
        out_specs += [pl.BlockSpec((tm, RW_COLS), row), pl.BlockSpec((tm, SW_COLS), row),
                      pl.BlockSpec((tm, ML_COLS_PAD), row)]
        out_shape += [jax.ShapeDtypeStruct((m, RW_COLS), F32), jax.ShapeDtypeStruct((m, SW_COLS), F32),
                      jax.ShapeDtypeStruct((m, ML_COLS_PAD), F32)]
        args += [w_next, b_next]
    out = pl.pallas_call(
        _combine_kernel,
        grid=(n_tiles,),
        in_specs=in_specs,
        out_specs=out_specs,
        out_shape=out_shape,
        compiler_params=_cparams("parallel"),
        name="combine_ln_inproj" if fused else "combine_ln",
    )(*args)
    return out if fused else out[0]


def _routing_tables(top_idx):
    m = top_idx.shape[0]
    n_assign = m * TOP_K
    bm = MOE_BLOCK
    n_blocks = -(-(n_assign + N_EXPERTS * (bm - 1)) // bm)
    p_rows = n_blocks * bm
    e_flat = top_idx.reshape(n_assign)
    order = jnp.argsort(e_flat, stable=True).astype(jnp.int32)
    experts = jnp.arange(N_EXPERTS, dtype=jnp.int32)
    counts = jnp.sum((e_flat[:, None] == experts[None, :]).astype(jnp.int32), axis=0)
    padded = (counts + bm - 1) // bm * bm
    start = jnp.cumsum(counts) - counts
    pend = jnp.cumsum(padded)
    pstart = pend - padded
    block_start = jnp.arange(n_blocks, dtype=jnp.int32) * bm
    block_e = jnp.minimum(jnp.sum((pend[None, :] <= block_start[:, None]).astype(jnp.int32), axis=1), N_EXPERTS - 1)
    per_row = lambda per_expert: jnp.repeat(per_expert[block_e], bm)
    row_pstart, row_start, row_count = per_row(pstart), per_row(start), per_row(counts)
    pos = jnp.arange(p_rows, dtype=jnp.int32) - row_pstart
    is_pad = pos >= row_count
    src_assign = order[jnp.clip(row_start + pos, 0, n_assign - 1)]
    tok = jnp.where(is_pad, 0, src_assign // TOP_K)
    pad_rank = (row_pstart - row_start) + (pos - row_count)
    slot = jnp.where(is_pad, n_assign + pad_rank, (src_assign % TOP_K) * m + src_assign // TOP_K)
    slot_ext = jnp.concatenate([p_rows + jnp.arange(bm, dtype=jnp.int32), slot])
    n_used = (pend[N_EXPERTS - 1] // bm).astype(jnp.int32)
    return tok, slot_ext, jnp.concatenate([block_e, n_used[None]]), p_rows + bm


def _block_ones(n):
    g = np.arange(n) // HEAD_DIM
    return jnp.asarray(g[:, None] == g[None, :], BF16)


def _sw_head_order():
    group = SW_HEADS // SW_KV_HEADS
    heads = [kv * group + g for g in range(group) for kv in range(SW_KV_HEADS)]
    return np.concatenate([np.arange(h * HEAD_DIM, (h + 1) * HEAD_DIM) for h in heads])


def _rope_tables(seq):
    half = HEAD_DIM // 2
    inv = ROPE_THETA ** (-jnp.arange(half, dtype=F32) / half)
    ang = jnp.arange(seq, dtype=F32)[:, None] * inv[None, :]
    cos, sin = jnp.cos(ang), jnp.sin(ang)
    reps = LANES // HEAD_DIM
    cos_t = jnp.tile(jnp.concatenate([cos, cos], axis=-1), (1, reps))
    sin_t = jnp.tile(jnp.concatenate([-sin, sin], axis=-1), (1, reps))
    return cos_t, sin_t


def kernel(x, w_in, b_in, rw_shift_mu, rw_w_up, rw_w0, rw_a_up, rw_a0, rw_g_up, rw_k_k, rw_k_a, rw_r_k, rw_ln_w, rw_ln_b, sw_sinks, ml_conv_w, ml_conv_b, ml_norm_w, w_out, ln1_w, ln1_b, router_w, router_b, exp_w_up, exp_b_up, exp_w_down, exp_b_down, ln2_w, ln2_b):
    batch, seq, d_model = x.shape
    assert d_model == D_MODEL and seq % ROW_TILE == 0 and seq % ML_CHUNK == 0 and seq % WINDOW == 0
    m = batch * seq
    depth = w_in.shape[0]
    row2 = lambda t: t.reshape(1, -1)

    sw_perm = _sw_head_order()
    in_cols = np.concatenate([np.arange(RW_COLS), RW_COLS + sw_perm, np.arange(RW_COLS + SW_DIM, C_IN)])
    ones_rw = _block_ones(RW_DIM)
    ones_ml = _block_ones(ML_DIM)
    cos_t, sin_t = _rope_tables(seq)
    src = np.arange(LANES)[:, None]
    dst = np.arange(2 * ML_DIM)[None, :]
    expand = jnp.asarray(src == (dst // ML_DIM) * ML_HEADS + (dst % ML_DIM) // HEAD_DIM, BF16)
    pj = np.arange(2 * LANES)[:, None]
    pc = np.arange(2 * LANES)[None, :]
    perm = jnp.asarray(pj == 2 * (pc % LANES) + pc // LANES, BF16)
    n_layers = exp_b_up.shape[0]
    b_glu = exp_b_up[:, :, 0::2].reshape(n_layers, N_EXPERTS, 1, D_EXPERT)
    b_lin = exp_b_up[:, :, 1::2].reshape(n_layers, N_EXPERTS, 1, D_EXPERT)
    b_dn = exp_b_down.reshape(n_layers, N_EXPERTS, 1, D_MODEL)

    w_in_l = [jnp.pad(w_in[l][:, in_cols], ((0, 0), (0, C_IN_PAD - C_IN))).astype(BF16) for l in range(depth)]
    b_in_l = [jnp.pad(b_in[l][in_cols], (0, C_IN_PAD - C_IN)).reshape(1, C_IN_PAD) for l in range(depth)]

    h = x.reshape(m, D_MODEL)
    p_rw, p_sw, p_ml = _in_proj(h, w_in_l[0], b_in_l[0])
    for l in range(depth):
        zero = jnp.zeros((RW_LORA_W, RW_DIM), F32)
        wcomb = jnp.concatenate([jnp.concatenate([rw_w_up[l], zero], axis=1),
                                 jnp.concatenate([zero, rw_a_up[l]], axis=1)], axis=0).astype(BF16)
        w_o = w_out[l]
        w_o_rw = w_o[0:RW_DIM].astype(BF16)
        w_o_sw = w_o[RW_DIM + sw_perm].astype(BF16)
        w_o_ml = w_o[RW_DIM + SW_DIM:].astype(BF16)
        rw_pad = jnp.pad(router_w[l], ((0, 0), (0, LANES - N_EXPERTS)))
        rw_hi = rw_pad.astype(BF16)
        rw_r1 = rw_pad - rw_hi.astype(F32)
        rw_mid = rw_r1.astype(BF16)
        rw_lo = (rw_r1 - rw_mid.astype(F32)).astype(BF16)
        rb_pad = jnp.pad(router_b[l], (0, LANES - N_EXPERTS), constant_values=NEG_BIG).reshape(1, LANES)

        r, lw, k, v, a, b, g = _rw_prep(p_rw, seq, row2(rw_shift_mu[l]), wcomb, rw_g_up[l].astype(BF16),
                                        row2(rw_w0[l]), row2(rw_a0[l]), row2(rw_k_k[l]), row2(rw_k_a[l]), ones_rw)
        m_mat, yh, g_mat, y0 = _rw_chunks(r, lw, k, v, a, b)
        rw_out = _rw_scan(batch, seq, m_mat, yh, g_mat, y0, r, k, v, g, row2(rw_r_k[l]), row2(rw_ln_w[l]),
                          row2(rw_ln_b[l]), ones_rw).reshape(m, RW_DIM)
        sw_out = _swa(p_sw.reshape(batch, seq, SW_COLS), sw_sinks[l], cos_t, sin_t).reshape(m, SW_DIM)
        ml_out = _mlstm(p_ml.reshape(batch, seq, ML_COLS_PAD), ml_conv_w[l], row2(ml_conv_b[l]),
                        row2(ml_norm_w[l]), expand, ones_ml).reshape(m, ML_DIM)
        h1, h1_packed, idx_pad, gates = _out_proj(rw_out, sw_out, ml_out, h, w_o_rw, w_o_sw, w_o_ml, row2(ln1_w[l]),
                                                 row2(ln1_b[l]), (rw_hi, rw_mid, rw_lo), rb_pad)

        tok, slot_ext, block_e, z_rows = _routing_tables(idx_pad[:, 0:TOP_K])
        z = _experts(l, block_e, tok, slot_ext, h1_packed, exp_w_up, b_glu, b_lin, exp_w_down, b_dn, perm, z_rows)
        if l + 1 < depth:
            h, p_rw, p_sw, p_ml = _combine(z, gates, h1, row2(ln2_w[l]), row2(ln2_b[l]), w_in_l[l + 1], b_in_l[l + 1])
        else:
            h = _combine(z, gates, h1, row2(ln2_w[l]), row2(ln2_b[l]))
    return h.reshape(batch, seq, D_MODEL)
```

```python
import functools

import numpy as np
import jax
import jax.numpy as jnp
from jax import lax
from jax.experimental import pallas as pl
from jax.experimental.pallas import tpu as pltpu

F32 = jnp.float32
BF16 = jnp.bfloat16

D_MODEL = 1024
HEAD_DIM = 64
RW_HEADS = 6
SW_HEADS = 6
SW_KV_HEADS = 2
ML_HEADS = 4
RW_DIM = RW_HEADS * HEAD_DIM
SW_DIM = SW_HEADS * HEAD_DIM
SW_KV_DIM = SW_KV_HEADS * HEAD_DIM
ML_DIM = ML_HEADS * HEAD_DIM
RW_LORA_W = 64
RW_LORA_A = 64
RW_LORA_G = 128
RW_COLS = 3 * RW_DIM + RW_LORA_W + RW_LORA_A + RW_LORA_G
SW_COLS = SW_DIM + 2 * SW_KV_DIM
ML_COLS = 4 * ML_DIM + 2 * ML_HEADS
C_IN = RW_COLS + SW_COLS + ML_COLS
WINDOW = 128
ROPE_THETA = 10000.0
CONV_WIDTH = 4
N_EXPERTS = 32
TOP_K = 4
D_EXPERT = 1024
SWIGLU_ALPHA = 1.702
SWIGLU_LIMIT = 7.0
LN_EPS = 1e-5
RW_GN_EPS = 64e-5
ML_GN_EPS = 1e-6
DEPTH = 4
DN_ALPHA = (2 * DEPTH) ** 0.25

LANES = 128
SUBLANES = 8
ML_COLS_PAD = 1152
C_IN_PAD = RW_COLS + SW_COLS + ML_COLS_PAD
ROW_TILE = 512
RW_CHUNK = 64
RW_PAIR = 2 * HEAD_DIM
RW_NPAIR = RW_HEADS // 2
SWA_BLOCKS = 4
ML_CHUNK = 256
MOE_BLOCK = 512
NEG_BIG = -1e30
VMEM_LIMIT = 56 * 1024 * 1024


def _cparams(*sem):
    return pltpu.CompilerParams(dimension_semantics=sem, vmem_limit_bytes=VMEM_LIMIT)


def _dot(a, b):
    return jnp.dot(a.astype(BF16), b.astype(BF16), preferred_element_type=F32)


def _dot_nt(a, b):
    return lax.dot_general(a.astype(BF16), b.astype(BF16), (((1,), (1,)), ((), ())),
                           preferred_element_type=F32)


def _split3(x):
    hi = x.astype(BF16)
    r1 = x - hi.astype(F32)
    mid = r1.astype(BF16)
    lo = (r1 - mid.astype(F32)).astype(BF16)
    return hi, mid, lo


def _dot_xl(x, w):
    hi = x.astype(BF16)
    mid = (x - hi.astype(F32)).astype(BF16)
    return jnp.dot(hi, w, preferred_element_type=F32) + jnp.dot(mid, w, preferred_element_type=F32)


def _dot_lx(w, x):
    hi, mid, lo = _split3(x)
    return (jnp.dot(w, hi, preferred_element_type=F32) + jnp.dot(w, mid, preferred_element_type=F32)
            + jnp.dot(w, lo, preferred_element_type=F32))


def _softplus(z):
    return jnp.maximum(z, 0.0) + jnp.log(1.0 + jnp.exp(-jnp.abs(z)))


def _sigmoid(z):
    return 1.0 / (1.0 + jnp.exp(-z))


def _shift_rows(x, prev8, j):
    cat = jnp.concatenate([prev8, x], axis=0)
    return pltpu.roll(cat, j, 0)[SUBLANES:, :]


def _layer_norm(x, w, b):
    mu = jnp.mean(x, axis=-1, keepdims=True)
    d = x - mu
    var = jnp.mean(d * d, axis=-1, keepdims=True)
    return d * lax.rsqrt(var + LN_EPS) * w + b


def _pack_bf16_pairs(x):
    c = x.shape[1] // 2
    bits = pltpu.bitcast(x.astype(BF16).astype(F32), jnp.uint32)
    return lax.shift_right_logical(bits[:, 0:c], jnp.uint32(16)) | (bits[:, c:2 * c] & jnp.uint32(0xFFFF0000))


def _unpack_bf16_pairs(w):
    lo = pltpu.bitcast(lax.shift_left(w, jnp.uint32(16)), F32)
    hi = pltpu.bitcast(w & jnp.uint32(0xFFFF0000), F32)
    return jnp.concatenate([lo, hi], axis=1).astype(BF16)


def _head_norm(y, ones_bd, eps):
    inv = 1.0 / HEAD_DIM
    mu = _dot_xl(y, ones_bd) * inv
    d = y - mu
    var = _dot_xl(d * d, ones_bd) * inv
    return d * lax.rsqrt(var + eps)


def _inproj_kernel(x_ref, w_ref, b_ref, rw_ref, sw_ref, ml_ref):
    _project_in(x_ref[...], w_ref, b_ref, rw_ref, sw_ref, ml_ref)


def _project_in(x, w_ref, b_ref, rw_ref, sw_ref, ml_ref):
    x = x.astype(BF16)
    c1, c2 = RW_COLS, RW_COLS + SW_COLS
    rw_ref[...] = jnp.dot(x, w_ref[:, 0:c1], preferred_element_type=F32) + b_ref[:, 0:c1]
    sw_ref[...] = jnp.dot(x, w_ref[:, c1:c2], preferred_element_type=F32) + b_ref[:, c1:c2]
    ml_ref[...] = jnp.dot(x, w_ref[:, c2:C_IN_PAD], preferred_element_type=F32) + b_ref[:, c2:C_IN_PAD]


def _in_proj(h2d, w, b):
    m = h2d.shape[0]
    tm = ROW_TILE
    row = lambda i: (i, 0)
    fixed = lambda i: (0, 0)
    return pl.pallas_call(
        _inproj_kernel,
        grid=(m // tm,),
        in_specs=[pl.BlockSpec((tm, D_MODEL), row), pl.BlockSpec((D_MODEL, C_IN_PAD), fixed),
                  pl.BlockSpec((1, C_IN_PAD), fixed)],
        out_specs=[pl.BlockSpec((tm, RW_COLS), row), pl.BlockSpec((tm, SW_COLS), row),
                   pl.BlockSpec((tm, ML_COLS_PAD), row)],
        out_shape=[jax.ShapeDtypeStruct((m, RW_COLS), F32), jax.ShapeDtypeStruct((m, SW_COLS), F32),
                   jax.ShapeDtypeStruct((m, ML_COLS_PAD), F32)],
        compiler_params=_cparams("parallel"),
        name="in_proj",
    )(h2d, w, b)


def _rw_prep_kernel(tiles_per_seq, p_ref, pprev_ref, mu_ref, wcomb_ref, gup_ref, w0_ref, a0_ref, kk_ref,
                    ka_ref, ones_ref, r_o, lw_o, k_o, v_o, a_o, b_o, g_o):
    i = pl.program_id(0)
    x = p_ref[...]
    first = (i % tiles_per_seq) == 0
    prev8 = jnp.where(first, 0.0, pprev_ref[...])
    xs = x + (_shift_rows(x, prev8, 1) - x) * mu_ref[...]
    d = RW_DIM
    r = xs[:, 0:d]
    k = xs[:, d:2 * d]
    v = xs[:, 2 * d:3 * d]
    slab = xs[:, 3 * d:3 * d + LANES]
    lane = lax.broadcasted_iota(jnp.int32, slab.shape, 1)
    slab = jnp.where(lane < RW_LORA_W, jnp.tanh(slab), slab)
    wa = _dot(slab, wcomb_ref[...])
    w_pre = w0_ref[...] + wa[:, 0:d]
    a_pre = a0_ref[...] + wa[:, d:2 * d]
    w_log = -_softplus(-w_pre) - 0.5
    log_decay = -jnp.exp(w_log)
    a_sig = _sigmoid(a_pre)
    g = _dot(_sigmoid(xs[:, 3 * d + LANES:3 * d + 2 * LANES]), gup_ref[...])
    kk = k * kk_ref[...]
    ss = _dot_xl(kk * kk, ones_ref[...])
    kk = kk / jnp.maximum(jnp.sqrt(ss), 1e-12)
    r_o[...] = r
    lw_o[...] = log_decay
    k_o[...] = k * (1.0 + (a_sig - 1.0) * ka_ref[...])
    v_o[...] = v
    a_o[...] = -kk
    b_o[...] = kk * a_sig
    g_o[...] = g


def _rw_prep(p_rw, seq, mu, wcomb, gup, w0, a0, k_k, k_a, ones_bd):
    m = p_rw.shape[0]
    tm = ROW_TILE
    row = lambda i: (i, 0)
    fixed = lambda i: (0, 0)
    prev = lambda i: (jnp.maximum(i * (tm // SUBLANES) - 1, 0), 0)
    vec = pl.BlockSpec((1, RW_DIM), fixed)
    out = jax.ShapeDtypeStruct((m, RW_DIM), F32)
    return pl.pallas_call(
        functools.partial(_rw_prep_kernel, seq // tm),
        grid=(m // tm,),
        in_specs=[pl.BlockSpec((tm, RW_COLS), row), pl.BlockSpec((SUBLANES, RW_COLS), prev),
                  pl.BlockSpec((1, RW_COLS), fixed), pl.BlockSpec((LANES, 2 * RW_DIM), fixed),
                  pl.BlockSpec((RW_LORA_G, RW_DIM), fixed), vec, vec, vec, vec,
                  pl.BlockSpec((RW_DIM, RW_DIM), fixed)],
        out_specs=[pl.BlockSpec((tm, RW_DIM), row)] * 7,
        out_shape=[out] * 7,
        compiler_params=_cparams("parallel"),
        name="rw_prep",
    )(p_rw, p_rw, mu, wcomb, gup, w0, a0, k_k, k_a, ones_bd)


def _rw_chunk_kernel(n_chunk, r_ref, lw_ref, k_ref, v_ref, a_ref, b_ref, m_o, yh_o, g_o, y0_o):
    L = RW_CHUNK
    ti = lax.broadcasted_iota(jnp.int32, (L, L), 0)
    si = lax.broadcasted_iota(jnp.int32, (L, L), 1)
    tri = jnp.where(ti >= si, 1.0, 0.0).astype(BF16)
    n2 = 2 * L
    ri = lax.broadcasted_iota(jnp.int32, (n2, n2), 0)
    ci = lax.broadcasted_iota(jnp.int32, (n2, n2), 1)
    rr = jnp.where(ri >= L, ri - L, ri)
    cc = jnp.where(ci >= L, ci - L, ci)
    strict = rr > cc
    incl = rr >= cc
    eye = ri == ci
    eye_f = jnp.where(eye, 1.0, 0.0)
    lane = lax.broadcasted_iota(jnp.int32, (L, RW_PAIR), 1)
    head0 = lane < HEAD_DIM

    def bmm(x, y):
        return jnp.einsum('bmk,bkn->bmn', x.astype(BF16), y.astype(BF16), preferred_element_type=F32)

    def bmm_nt(x, y):
        return jnp.einsum('bmk,bnk->bmn', x.astype(BF16), y.astype(BF16), preferred_element_type=F32)

    def bmm_tn(x, y):
        return bmm(jnp.swapaxes(x, 1, 2), y)

    def stack(x):
        return jnp.concatenate([jnp.where(head0, x, 0.0), jnp.where(head0, 0.0, x)], axis=1)

    def chunked(ref, cols):
        return ref[:, cols].reshape(n_chunk, L, RW_PAIR)

    tri_b = jnp.broadcast_to(tri, (n_chunk, L, L))
    for p in range(RW_NPAIR):
        cols = slice(p * RW_PAIR, (p + 1) * RW_PAIR)
        lw = chunked(lw_ref, cols)
        hi, mid, lo = _split3(lw)
        tsum = lambda part: jnp.einsum('bts,bsn->btn', tri_b, part, preferred_element_type=F32)
        lc = tsum(hi) + tsum(mid) + tsum(lo)
        lcl = lc[:, L - 1:L, :]
        gam = jnp.exp(lc)
        gam_prev = jnp.exp(lc - lw)
        inv = jnp.exp(-lc)
        to_end = jnp.exp(lcl - lc)
        r = chunked(r_ref, cols)
        k = chunked(k_ref, cols)
        v = chunked(v_ref, cols)
        a = chunked(a_ref, cols)
        b = chunked(b_ref, cols)
        x1 = stack(a * gam_prev)
        r1 = stack(r * gam)
        x2 = stack(b * inv)
        k2 = stack(k * inv)
        vs = stack(v)
        bs = stack(b * to_end)
        ks = stack(k * to_end)
        aa = bmm_nt(jnp.concatenate([x1, r1], axis=1), jnp.concatenate([x2, k2], axis=1))
        a_ab = jnp.where(strict, aa[:, 0:n2, 0:n2], 0.0)
        a_ak = jnp.where(strict, aa[:, 0:n2, n2:2 * n2], 0.0)
        a_rb = jnp.where(incl, aa[:, n2:2 * n2, 0:n2], 0.0)
        a_rk = jnp.where(incl, aa[:, n2:2 * n2, n2:2 * n2], 0.0)
        apow = a_ab
        t_inv = eye_f + a_ab
        for _ in range(5):
            apow = bmm(apow, apow)
            t_inv = t_inv + bmm(t_inv, apow)
        akv = bmm(a_ak, vs)
        pw = bmm(t_inv, jnp.concatenate([x1, akv], axis=2))
        pmat = pw[:, :, 0:RW_PAIR]
        wmat = pw[:, :, RW_PAIR:2 * RW_PAIR]
        gl_row = jnp.exp(lcl)
        m_mat = bmm_tn(bs, pmat) + jnp.where(eye, gl_row, 0.0)
        g_mat = bmm_tn(jnp.concatenate([bs, ks], axis=1), jnp.concatenate([wmat, vs], axis=1))
        yy = bmm(a_rb, pw)
        yh = r1 + yy[:, :, 0:RW_PAIR]
        y0 = yy[:, :, RW_PAIR:2 * RW_PAIR] + bmm(a_rk, vs)
        m_o[:, p] = m_mat.astype(BF16)
        yh_o[:, p] = yh.astype(BF16)
        g_o[:, p] = g_mat
        y0_o[:, p] = y0


def _rw_chunks(r, lw, k, v, a, b):
    m = r.shape[0]
    tm = ROW_TILE
    nc = tm // RW_CHUNK
    row = lambda i: (i, 0)
    blk = lambda i: (i, 0, 0, 0)
    n2 = 2 * RW_CHUNK
    n_tot = m // RW_CHUNK
    ospec = pl.BlockSpec((nc, RW_NPAIR, n2, RW_PAIR), blk)
    return pl.pallas_call(
        functools.partial(_rw_chunk_kernel, nc),
        grid=(m // tm,),
        in_specs=[pl.BlockSpec((tm, RW_DIM), row)] * 6,
        out_specs=[ospec] * 4,
        out_shape=[jax.ShapeDtypeStruct((n_tot, RW_NPAIR, n2, RW_PAIR), BF16),
                   jax.ShapeDtypeStruct((n_tot, RW_NPAIR, n2, RW_PAIR), BF16),
                   jax.ShapeDtypeStruct((n_tot, RW_NPAIR, n2, RW_PAIR), F32),
                   jax.ShapeDtypeStruct((n_tot, RW_NPAIR, n2, RW_PAIR), F32)],
        compiler_params=_cparams("parallel"),
        name="rw_chunks",
    )(r, lw, k, v, a, b)


def _rw_scan_kernel(batch, m_ref, yh_ref, g_ref, y0_ref, r_ref, k_ref, v_ref, gate_ref, rk_ref, lnw_ref, lnb_ref,
                    ones_ref, o_ref, h_ref, y_s):
    c = pl.program_id(0)

    @pl.when(c == 0)
    def _():
        h_ref[...] = jnp.zeros_like(h_ref)

    L = RW_CHUNK
    for bi in range(batch):
        for p in range(RW_NPAIR):
            hb = h_ref[bi, p].astype(BF16)
            yrows = jnp.dot(yh_ref[bi, 0, p], hb, preferred_element_type=F32) + y0_ref[bi, 0, p]
            y_s[bi, :, p * RW_PAIR:(p + 1) * RW_PAIR] = yrows[0:L, :] + yrows[L:2 * L, :]
            h_ref[bi, p] = jnp.dot(m_ref[bi, 0, p], hb, preferred_element_type=F32) + g_ref[bi, 0, p]

    rows = batch * L
    flat = lambda ref: ref[...].reshape(rows, RW_DIM)
    ones_bd = ones_ref[...]
    y = _head_norm(flat(y_s), ones_bd, RW_GN_EPS) * lnw_ref[...] + lnb_ref[...]
    v = flat(v_ref)
    bonus = _dot_xl(flat(r_ref) * flat(k_ref) * rk_ref[...], ones_bd) * v
    o_ref[...] = ((y + bonus) * flat(gate_ref)).reshape(batch, L, RW_DIM)


def _rw_scan(batch, seq, m_mat, yh, g_mat, y0, r, k, v, g, r_k, ln_w, ln_b, ones_bd):
    nc = seq // RW_CHUNK
    n2 = 2 * RW_CHUNK
    shp = (batch, nc, RW_NPAIR, n2, RW_PAIR)
    chunk_args = [t.reshape(shp) for t in (m_mat, yh, g_mat, y0)]
    row_args = [t.reshape(batch, seq, RW_DIM) for t in (r, k, v, g)]
    ispec = pl.BlockSpec((batch, 1, RW_NPAIR, n2, RW_PAIR), lambda c: (0, c, 0, 0, 0))
    rspec = pl.BlockSpec((batch, RW_CHUNK, RW_DIM), lambda c: (0, c, 0))
    vec = pl.BlockSpec((1, RW_DIM), lambda c: (0, 0))
    return pl.pallas_call(
        functools.partial(_rw_scan_kernel, batch),
        grid=(nc,),
        in_specs=[ispec] * 4 + [rspec] * 4 + [vec, vec, vec, pl.BlockSpec((RW_DIM, RW_DIM), lambda c: (0, 0))],
        out_specs=rspec,
        out_shape=jax.ShapeDtypeStruct((batch, seq, RW_DIM), F32),
        scratch_shapes=[pltpu.VMEM((batch, RW_NPAIR, RW_PAIR, RW_PAIR), F32),
                        pltpu.VMEM((batch, RW_CHUNK, RW_DIM), F32)],
        compiler_params=_cparams("arbitrary"),
        name="rw_scan_post",
    )(*chunk_args, *row_args, r_k, ln_w, ln_b, ones_bd)


def _rope(x, cos, sin_signed):
    lane = lax.broadcasted_iota(jnp.int32, x.shape, 1)
    half = HEAD_DIM // 2
    first_half = (lane % HEAD_DIM) < half
    rot = jnp.where(first_half, pltpu.roll(x, LANES - half, 1), pltpu.roll(x, half, 1))
    return x * cos + rot * sin_signed


def _swa_kernel(sink_ref, q_ref, kc_ref, vc_ref, kp_ref, vp_ref, cosc_ref, sinc_ref, cosp_ref, sinp_ref, o_ref):
    j = pl.program_id(1)
    W = WINDOW
    cos_c, sin_c = cosc_ref[...], sinc_ref[...]
    k_all = jnp.concatenate([_rope(kp_ref[0], cosp_ref[...], sinp_ref[...]), _rope(kc_ref[0], cos_c, sin_c)],
                            axis=0).astype(BF16)
    v_all = jnp.concatenate([vp_ref[0], vc_ref[0]], axis=0).astype(BF16)
    qi = lax.broadcasted_iota(jnp.int32, (W, 2 * W), 0)
    kj = lax.broadcasted_iota(jnp.int32, (W, 2 * W), 1)
    band = (kj > qi) & (kj <= qi + W)
    bias_inner = jnp.where(band, 0.0, NEG_BIG)
    bias_first = jnp.where(band & (kj >= W), 0.0, NEG_BIG)
    lane = lax.broadcasted_iota(jnp.int32, (W, LANES), 1)
    kv0 = lane < HEAD_DIM
    scale = HEAD_DIM ** -0.5
    group = SW_HEADS // SW_KV_HEADS
    for t in range(SWA_BLOCKS):
        rows = slice(t * W, (t + 1) * W)
        bias = jnp.where(j > 0, bias_inner, bias_first) if t == 0 else bias_inner
        k_cat = k_all[t * W:(t + 2) * W]
        v_cat = v_all[t * W:(t + 2) * W]
        for g in range(group):
            q = _rope(q_ref[0, rows, g * LANES:(g + 1) * LANES], cos_c[rows], sin_c[rows]) * scale
            outs = []
            for kv in range(SW_KV_HEADS):
                qm = jnp.where(kv0, q, 0.0) if kv == 0 else jnp.where(kv0, 0.0, q)
                s = _dot_nt(qm, k_cat) + bias
                sink = sink_ref[kv * group + g]
                mx = jnp.maximum(jnp.max(s, axis=-1, keepdims=True), sink)
                pr = jnp.exp(s - mx)
                den = jnp.sum(pr, axis=-1, keepdims=True) + jnp.exp(sink - mx)
                outs.append(jnp.dot(pr.astype(BF16), v_cat, preferred_element_type=F32) / den)
            o_ref[0, rows, g * LANES:(g + 1) * LANES] = jnp.where(kv0, outs[0], outs[1])


def _swa(p_sw3, sinks, cos_t, sin_t):
    batch, seq, _ = p_sw3.shape
    W = WINDOW
    nq = SWA_BLOCKS
    nb = seq // (nq * W)
    kcol = SW_DIM // LANES
    cur = lambda c: (lambda b, j: (b, j, c))
    prv = lambda c: (lambda b, j: (b, jnp.maximum(nq * j - 1, 0), c))
    tab_c = pl.BlockSpec((nq * W, LANES), lambda b, j: (j, 0))
    tab_p = pl.BlockSpec((W, LANES), lambda b, j: (jnp.maximum(nq * j - 1, 0), 0))
    kv_cur = lambda f: pl.BlockSpec((1, nq * W, LANES), f)
    kv_prv = lambda f: pl.BlockSpec((1, W, LANES), f)
    return pl.pallas_call(
        _swa_kernel,
        grid=(batch, nb),
        in_specs=[pl.BlockSpec(memory_space=pltpu.SMEM),
                  pl.BlockSpec((1, nq * W, SW_DIM), cur(0)),
                  kv_cur(cur(kcol)), kv_cur(cur(kcol + 1)), kv_prv(prv(kcol)), kv_prv(prv(kcol + 1)),
                  tab_c, tab_c, tab_p, tab_p],
        out_specs=pl.BlockSpec((1, nq * W, SW_DIM), cur(0)),
        out_shape=jax.ShapeDtypeStruct((batch, seq, SW_DIM), F32),
        compiler_params=_cparams("parallel", "parallel"),
        name="swa",
    )(sinks, p_sw3, p_sw3, p_sw3, p_sw3, p_sw3, cos_t, sin_t, cos_t, sin_t)


def _mlstm_kernel(p_ref, pprev_ref, cw_ref, cb_ref, nw_ref, expand_ref, ones_ref, o_ref, c_ref, nv_ref, m_ref):
    j = pl.program_id(1)
    L = ML_CHUNK
    d = ML_DIM

    @pl.when(j == 0)
    def _():
        c_ref[...] = jnp.zeros_like(c_ref)
        nv_ref[...] = jnp.zeros_like(nv_ref)
        m_ref[...] = jnp.zeros_like(m_ref)

    x = p_ref[0]
    qk_pre = x[:, 0:2 * d]
    prev8 = jnp.where(j == 0, 0.0, pprev_ref[0][:, 0:2 * d])
    conv = cb_ref[...] + cw_ref[CONV_WIDTH - 1:CONV_WIDTH, :] * qk_pre
    for s in range(1, CONV_WIDTH):
        conv = conv + cw_ref[CONV_WIDTH - 1 - s:CONV_WIDTH - s, :] * _shift_rows(qk_pre, prev8, s)
    qk = conv * _sigmoid(conv)
    q = qk[:, 0:d]
    k = qk[:, d:2 * d] * (HEAD_DIM ** -0.5)
    v = x[:, 2 * d:3 * d]
    o_gate = x[:, 3 * d:4 * d]
    gates = _dot_xl(x[:, 4 * d:4 * d + LANES], expand_ref[...])
    i_full = gates[:, 0:d]
    f_full = gates[:, d:2 * d]
    lf_full = jnp.minimum(f_full, 0.0) - jnp.log(1.0 + jnp.exp(-jnp.abs(f_full)))
    ti = lax.broadcasted_iota(jnp.int32, (L, L), 0)
    si = lax.broadcasted_iota(jnp.int32, (L, L), 1)
    causal = ti >= si
    tri = jnp.where(causal, 1.0, 0.0).astype(BF16)
    bc_full = _dot_lx(tri, lf_full)
    u_t = (i_full - bc_full).T
    lane = lax.broadcasted_iota(jnp.int32, (L, d), 1)
    lane_row = lax.broadcasted_iota(jnp.int32, (1, d), 1)
    q_b = q.astype(BF16)
    k_b = k.astype(BF16)
    m_prev_row = m_ref[...]
    nv_row = nv_ref[...]
    qn = q * nv_row
    num = jnp.zeros((L, d), F32)
    inter_full = jnp.zeros((L, d), F32)
    den_full = jnp.zeros((L, d), F32)
    mnew_row = jnp.zeros((1, d), F32)
    for h in range(ML_HEADS):
        c0 = h * HEAD_DIM
        in_head = (lane >= c0) & (lane < c0 + HEAD_DIM)
        in_head_row = (lane_row >= c0) & (lane_row < c0 + HEAD_DIM)
        bc_col = bc_full[:, c0:c0 + 1]
        dmat = jnp.where(causal, bc_col + u_t[c0:c0 + 1, :], NEG_BIG)
        m_inter = bc_col + m_prev_row[:, c0:c0 + 1]
        m_t = jnp.maximum(m_inter, jnp.max(dmat, axis=-1, keepdims=True))
        inter = jnp.exp(m_inter - m_t)
        e = jnp.exp(dmat - m_t)
        sm = _dot_nt(jnp.where(in_head, q, 0.0), k_b) * e
        num = num + _dot(sm, jnp.where(in_head, v, 0.0))
        nq = jnp.sum(sm, axis=-1, keepdims=True) + inter * jnp.sum(jnp.where(in_head, qn, 0.0), axis=-1, keepdims=True)
        den = jnp.maximum(jnp.abs(nq), jnp.exp(-m_t))
        inter_full = jnp.where(in_head, inter, inter_full)
        den_full = jnp.where(in_head, den, den_full)
        mnew_row = jnp.where(in_head_row, m_t[L - 1:L, :], mnew_row)
    c_mat = c_ref[...]
    num = num + inter_full * jnp.dot(q_b, c_mat.astype(BF16), preferred_element_type=F32)
    hout = num / den_full
    hn = _head_norm(hout, ones_ref[...], ML_GN_EPS) * nw_ref[...]
    o_ref[0] = _sigmoid(o_gate) * hn
    bcl_row = bc_full[L - 1:L, :]
    wst = jnp.exp(bcl_row - bc_full + i_full - mnew_row)
    dec_row = jnp.exp(bcl_row + m_prev_row - mnew_row)
    kw = k * wst
    ri = lax.broadcasted_iota(jnp.int32, (d, d), 0) // HEAD_DIM
    ci = lax.broadcasted_iota(jnp.int32, (d, d), 1) // HEAD_DIM
    c_ref[...] = dec_row * c_mat + jnp.where(ri == ci, _dot(kw.T, v), 0.0)
    nv_ref[...] = dec_row * nv_row + jnp.sum(kw, axis=0, keepdims=True)
    m_ref[...] = mnew_row


def _mlstm(p_ml3, conv_w, conv_b, norm_w, expand, ones_bd):
    batch, seq, _ = p_ml3.shape
    L = ML_CHUNK
    fixed = lambda b, j: (0, 0)
    return pl.pallas_call(
        _mlstm_kernel,
        grid=(batch, seq // L),
        in_specs=[pl.BlockSpec((1, L, ML_COLS_PAD), lambda b, j: (b, j, 0)),
                  pl.BlockSpec((1, SUBLANES, ML_COLS_PAD), lambda b, j: (b, jnp.maximum(j * (L // SUBLANES) - 1, 0), 0)),
                  pl.BlockSpec((CONV_WIDTH, 2 * ML_DIM), fixed), pl.BlockSpec((1, 2 * ML_DIM), fixed),
                  pl.BlockSpec((1, ML_DIM), fixed), pl.BlockSpec((LANES, 2 * ML_DIM), fixed),
                  pl.BlockSpec((ML_DIM, ML_DIM), fixed)],
        out_specs=pl.BlockSpec((1, L, ML_DIM), lambda b, j: (b, j, 0)),
        out_shape=jax.ShapeDtypeStruct((batch, seq, ML_DIM), F32),
        scratch_shapes=[pltpu.VMEM((ML_DIM, ML_DIM), F32), pltpu.VMEM((1, ML_DIM), F32),
                        pltpu.VMEM((1, ML_DIM), F32)],
        compiler_params=_cparams("parallel", "arbitrary"),
        name="mlstm",
    )(p_ml3, p_ml3, conv_w, conv_b, norm_w, expand, ones_bd)


def _outproj_kernel(rw_ref, sw_ref, ml_ref, h_ref, wrw_ref, wsw_ref, wml_ref, lnw_ref, lnb_ref,
                    rwh_ref, rwm_ref, rwl_ref, rb_ref, h1_o, h1p_o, idx_o, gate_o):
    mix = (_dot(rw_ref[...], wrw_ref[...]) + _dot(sw_ref[...], wsw_ref[...]) + _dot(ml_ref[...], wml_ref[...]))
    h1 = _layer_norm(DN_ALPHA * h_ref[...] + mix, lnw_ref[...], lnb_ref[...])
    h1_o[...] = h1
    h1p_o[...] = _pack_bf16_pairs(h1)
    xh, xm, xl = _split3(h1)
    wh, wm, wl = rwh_ref[...], rwm_ref[...], rwl_ref[...]
    dd = lambda a, b: jnp.dot(a, b, preferred_element_type=F32)
    logits = (dd(xh, wh) + (dd(xh, wm) + dd(xm, wh)) + (dd(xh, wl) + dd(xm, wm) + dd(xl, wh))) + rb_ref[...]
    lane = lax.broadcasted_iota(jnp.int32, logits.shape, 1).astype(F32)
    vals, idxs = [], []
    cur = logits
    for _ in range(TOP_K):
        mx = jnp.max(cur, axis=-1, keepdims=True)
        ix = jnp.min(jnp.where(cur == mx, lane, float(LANES)), axis=-1, keepdims=True)
        vals.append(mx)
        idxs.append(ix)
        cur = jnp.where(lane == ix, NEG_BIG * 2.0, cur)
    es = [jnp.exp(vv - vals[0]) for vv in vals]
    den = es[0] + es[1] + es[2] + es[3]
    idx_out = jnp.zeros_like(logits)
    gate_out = jnp.zeros_like(logits)
    for kk in range(TOP_K):
        idx_out = jnp.where(lane == float(kk), idxs[kk], idx_out)
        gate_out = jnp.where(lane == float(kk), es[kk] / den, gate_out)
    idx_o[...] = idx_out.astype(jnp.int32)
    gate_o[...] = gate_out


def _out_proj(rw_out, sw_out, ml_out, h2d, w_rw, w_sw, w_ml, ln_w, ln_b, rw_parts, rb):
    m = h2d.shape[0]
    tm = ROW_TILE
    row = lambda i: (i, 0)
    fixed = lambda i: (0, 0)
    full = lambda a: pl.BlockSpec(a.shape, fixed)
    return pl.pallas_call(
        _outproj_kernel,
        grid=(m // tm,),
        in_specs=[pl.BlockSpec((tm, RW_DIM), row), pl.BlockSpec((tm, SW_DIM), row), pl.BlockSpec((tm, ML_DIM), row),
                  pl.BlockSpec((tm, D_MODEL), row), full(w_rw), full(w_sw), full(w_ml), full(ln_w), full(ln_b),
                  full(rw_parts[0]), full(rw_parts[1]), full(rw_parts[2]), full(rb)],
        out_specs=[pl.BlockSpec((tm, D_MODEL), row), pl.BlockSpec((tm, D_MODEL // 2), row),
                   pl.BlockSpec((tm, LANES), row), pl.BlockSpec((tm, LANES), row)],
        out_shape=[jax.ShapeDtypeStruct((m, D_MODEL), F32), jax.ShapeDtypeStruct((m, D_MODEL // 2), jnp.uint32),
                   jax.ShapeDtypeStruct((m, LANES), jnp.int32), jax.ShapeDtypeStruct((m, LANES), F32)],
        compiler_params=_cparams("parallel"),
        name="out_proj_router",
    )(rw_out, sw_out, ml_out, h2d, w_rw, w_sw, w_ml, ln_w, ln_b, *rw_parts, rb)


def _expert_kernel(n_blocks, layer, be_ref, tokc_ref, tokn_ref, slot_ref, h_hbm, wup_ref, bg_ref, bl_ref,
                   wdn_ref, bd_ref, perm_ref, z_hbm, x0, x1, y0, y1, wg_s, wl_s, wd_s, gsem, ssem):
    del layer
    i = pl.program_id(0)
    bm = MOE_BLOCK

    def gather_row(tok, xdst, sem, r):
        return pltpu.make_async_copy(h_hbm.at[pl.ds(tok, 1)], xdst.at[pl.ds(r, 1)], sem)

    def scatter_row(dst, ysrc, sem, r):
        return pltpu.make_async_copy(ysrc.at[pl.ds(r, 1)], z_hbm.at[pl.ds(dst, 1)], sem)

    def wait_gather(xdst, sem):
        pltpu.make_async_copy(h_hbm.at[pl.ds(0, bm)], xdst, sem).wait()

    def wait_scatter(ysrc, sem):
        pltpu.make_async_copy(ysrc, z_hbm.at[pl.ds(0, bm)], sem).wait()

    @pl.when(i == 0)
    def _():
        y1[...] = jnp.zeros(y1.shape, F32)

        def prime(r, c):
            gather_row(tokc_ref[r], x0, gsem.at[0], r).start()
            return c

        lax.fori_loop(0, bm, prime, 0, unroll=8)

    n_used = be_ref[n_blocks]
    live = i <= n_used
    blk = jnp.minimum(i, n_blocks - 1)
    e_now = be_ref[blk]
    e_prev = be_ref[jnp.maximum(blk - 1, 0)]

    @pl.when(((i == 0) | (e_now != e_prev)) & live)
    def _():
        perm = perm_ref[...]
        for c in range(D_EXPERT // LANES):
            t = jnp.dot(wup_ref[0, 0, :, c * 2 * LANES:(c + 1) * 2 * LANES].astype(BF16), perm,
                        preferred_element_type=F32)
            wg_s[:, c * LANES:(c + 1) * LANES] = t[:, 0:LANES].astype(BF16)
            wl_s[:, c * LANES:(c + 1) * LANES] = t[:, LANES:2 * LANES].astype(BF16)
        wd_s[...] = wdn_ref[0, 0].astype(BF16)

    def step(x_cur, x_nxt, y_cur, y_nxt, g_cur, g_nxt, s_cur, s_nxt):
        wait_gather(x_cur, g_cur)
        for r in range(bm):
            gather_row(tokn_ref[r], x_nxt, g_nxt, r).start(priority=r % 2)
            scatter_row(slot_ref[r], y_nxt, s_nxt, r).start(priority=(r + 1) % 2)

        @pl.when(i > 0)
        def _():
            wait_scatter(y_cur, s_cur)

        x = _unpack_bf16_pairs(x_cur[...])
        hg = jnp.dot(x, wg_s[...], preferred_element_type=F32) + bg_ref[0, 0]
        hl = jnp.dot(x, wl_s[...], preferred_element_type=F32) + bl_ref[0, 0]
        hg = jnp.minimum(hg, SWIGLU_LIMIT)
        hl = jnp.clip(hl, -SWIGLU_LIMIT, SWIGLU_LIMIT)
        act = hg * _sigmoid(SWIGLU_ALPHA * hg) * (hl + 1.0)
        y_cur[...] = jnp.dot(act.astype(BF16), wd_s[...], preferred_element_type=F32) + bd_ref[0, 0]

        @pl.when(i == n_used)
        def _():
            wait_gather(x_nxt, g_nxt)
            wait_scatter(y_nxt, s_nxt)

    @pl.when(i > n_used)
    def _():
        @pl.when(i == n_used + 1)
        def _():
            y0[...] = jnp.zeros(y0.shape, F32)

        fill = pltpu.make_async_copy(y0, z_hbm.at[pl.ds(pl.multiple_of((i - 1) * bm, bm), bm)], ssem.at[0])
        fill.start()
        fill.wait()

    @pl.when((i % 2 == 0) & live)
    def _():
        step(x0, x1, y0, y1, gsem.at[0], gsem.at[1], ssem.at[0], ssem.at[1])

    @pl.when((i % 2 == 1) & live)
    def _():
        step(x1, x0, y1, y0, gsem.at[1], gsem.at[0], ssem.at[1], ssem.at[0])


def _experts(layer, block_e, tok, slot_ext, h1, w_up, b_glu, b_lin, w_down, b_down, perm, z_rows):
    bm = MOE_BLOCK
    n_blocks = tok.shape[0] // bm
    last = n_blocks - 1
    by_e = lambda i, be: (layer, be[jnp.minimum(i, last)], 0, 0)
    smem = lambda f: pl.BlockSpec((bm,), f, memory_space=pltpu.SMEM)
    grid_spec = pltpu.PrefetchScalarGridSpec(
        num_scalar_prefetch=1,
        grid=(n_blocks + 1,),
        in_specs=[smem(lambda i, be: (jnp.minimum(i, last),)), smem(lambda i, be: (jnp.minimum(i + 1, last),)),
                  smem(lambda i, be: (i,)),
                  pl.BlockSpec(memory_space=pl.ANY),
                  pl.BlockSpec((1, 1, D_MODEL, 2 * D_EXPERT), by_e),
                  pl.BlockSpec((1, 1, 1, D_EXPERT), by_e), pl.BlockSpec((1, 1, 1, D_EXPERT), by_e),
                  pl.BlockSpec((1, 1, D_EXPERT, D_MODEL), by_e), pl.BlockSpec((1, 1, 1, D_MODEL), by_e),
                  pl.BlockSpec((2 * LANES, 2 * LANES), lambda i, be: (0, 0))],
        out_specs=pl.BlockSpec(memory_space=pl.ANY),
        scratch_shapes=[pltpu.VMEM((bm, D_MODEL // 2), jnp.uint32), pltpu.VMEM((bm, D_MODEL // 2), jnp.uint32),
                        pltpu.VMEM((bm, D_MODEL), F32), pltpu.VMEM((bm, D_MODEL), F32),
                        pltpu.VMEM((D_MODEL, D_EXPERT), BF16), pltpu.VMEM((D_MODEL, D_EXPERT), BF16),
                        pltpu.VMEM((D_EXPERT, D_MODEL), BF16),
                        pltpu.SemaphoreType.DMA((2,)), pltpu.SemaphoreType.DMA((2,))],
    )
    return pl.pallas_call(
        functools.partial(_expert_kernel, n_blocks, layer),
        grid_spec=grid_spec,
        out_shape=jax.ShapeDtypeStruct((z_rows, D_MODEL), F32),
        compiler_params=pltpu.CompilerParams(dimension_semantics=("arbitrary",), vmem_limit_bytes=VMEM_LIMIT,
                                             disable_bounds_checks=True),
        name="experts",
    )(block_e, tok, tok, slot_ext, h1, w_up, b_glu, b_lin, w_down, b_down, perm)


def _combine_kernel(z0_ref, z1_ref, z2_ref, z3_ref, gate_ref, h_ref, lnw_ref, lnb_ref, *rest):
    gates = gate_ref[...]
    y = gates[:, 0:1] * z0_ref[...]
    for kk, z_ref in enumerate((z1_ref, z2_ref, z3_ref), start=1):
        y = y + gates[:, kk:kk + 1] * z_ref[...]
    h = _layer_norm(DN_ALPHA * h_ref[...] + y, lnw_ref[...], lnb_ref[...])
    if len(rest) == 1:
        rest[0][...] = h
    else:
        w_ref, b_ref, o_ref, rw_ref, sw_ref, ml_ref = rest
        o_ref[...] = h
        _project_in(h, w_ref, b_ref, rw_ref, sw_ref, ml_ref)


def _combine(z, gates, h1, ln_w, ln_b, w_next=None, b_next=None):
    m = h1.shape[0]
    fused = w_next is not None
    tm = ROW_TILE // 2 if fused else ROW_TILE
    n_tiles = m // tm
    row = lambda i: (i, 0)
    fixed = lambda i: (0, 0)
    choice = lambda kk: pl.BlockSpec((tm, D_MODEL), lambda i: (kk * n_tiles + i, 0))
    in_specs = [choice(0), choice(1), choice(2), choice(3), pl.BlockSpec((tm, LANES), row),
                pl.BlockSpec((tm, D_MODEL), row), pl.BlockSpec((1, D_MODEL), fixed), pl.BlockSpec((1, D_MODEL), fixed)]
    out_specs = [pl.BlockSpec((tm, D_MODEL), row)]
    out_shape = [jax.ShapeDtypeStruct((m, D_MODEL), F32)]
    args = [z, z, z, z, gates, h1, ln_w, ln_b]
    if fused:
        in_specs += [pl.BlockSpec((D_MODEL, C_IN_PAD), fixed), pl.BlockSpec((1, C_IN_PAD), fixed)]
        out_specs += [pl.BlockSpec((tm, RW_COLS), row), pl.BlockSpec((tm, SW_COLS), row),
                      pl.BlockSpec((tm, ML_COLS_PAD), row)]
        out_shape += [jax.ShapeDtypeStruct((m, RW_COLS), F32), jax.ShapeDtypeStruct((m, SW_COLS), F32),
                      jax.ShapeDtypeStruct((m, ML_COLS_PAD), F32)]
        args += [w_next, b_next]
    out = pl.pallas_call(
        _combine_kernel,
        grid=(n_tiles,),
        in_specs=in_specs,
        out_specs=out_specs,
        out_shape=out_shape,
        compiler_params=_cparams("parallel"),
        name="combine_ln_inproj" if fused else "combine_ln",
    )(*args)
    return out if fused else out[0]


def _routing_tables(top_idx):
    m = top_idx.shape[0]
    n_assign = m * TOP_K
    bm = MOE_BLOCK
    n_blocks = -(-(n_assign + N_EXPERTS * (bm - 1)) // bm)
    p_rows = n_blocks * bm
    e_flat = top_idx.reshape(n_assign)
    order = jnp.argsort(e_flat, stable=True).astype(jnp.int32)
    experts = jnp.arange(N_EXPERTS, dtype=jnp.int32)
    counts = jnp.sum((e_flat[:, None] == experts[None, :]).astype(jnp.int32), axis=0)
    padded = (counts + bm - 1) // bm * bm
    start = jnp.cumsum(counts) - counts
    pend = jnp.cumsum(padded)
    pstart = pend - padded
    block_start = jnp.arange(n_blocks, dtype=jnp.int32) * bm
    block_e = jnp.minimum(jnp.sum((pend[None, :] <= block_start[:, None]).astype(jnp.int32), axis=1), N_EXPERTS - 1)
    per_row = lambda per_expert: jnp.repeat(per_expert[block_e], bm)
    row_pstart, row_start, row_count = per_row(pstart), per_row(start), per_row(counts)
    pos = jnp.arange(p_rows, dtype=jnp.int32) - row_pstart
    is_pad = pos >= row_count
    src_assign = order[jnp.clip(row_start + pos, 0, n_assign - 1)]
    tok = jnp.where(is_pad, 0, src_assign // TOP_K)
    pad_rank = (row_pstart - row_start) + (pos - row_count)
    slot = jnp.where(is_pad, n_assign + pad_rank, (src_assign % TOP_K) * m + src_assign // TOP_K)
    slot_ext = jnp.concatenate([p_rows + jnp.arange(bm, dtype=jnp.int32), slot])
    n_used = (pend[N_EXPERTS - 1] // bm).astype(jnp.int32)
    return tok, slot_ext, jnp.concatenate([block_e, n_used[None]]), p_rows + bm


def _block_ones(n):
    g = np.arange(n) // HEAD_DIM
    return jnp.asarray(g[:, None] == g[None, :], BF16)


def _sw_head_order():
    group = SW_HEADS // SW_KV_HEADS
    heads = [kv * group + g for g in range(group) for kv in range(SW_KV_HEADS)]
    return np.concatenate([np.arange(h * HEAD_DIM, (h + 1) * HEAD_DIM) for h in heads])


def _rope_tables(seq):
    half = HEAD_DIM // 2
    inv = ROPE_THETA ** (-jnp.arange(half, dtype=F32) / half)
    ang = jnp.arange(seq, dtype=F32)[:, None] * inv[None, :]
    cos, sin = jnp.cos(ang), jnp.sin(ang)
    reps = LANES // HEAD_DIM
    cos_t = jnp.tile(jnp.concatenate([cos, cos], axis=-1), (1, reps))
    sin_t = jnp.tile(jnp.concatenate([-sin, sin], axis=-1), (1, reps))
    return cos_t, sin_t


def kernel(x, w_in, b_in, rw_shift_mu, rw_w_up, rw_w0, rw_a_up, rw_a0, rw_g_up, rw_k_k, rw_k_a, rw_r_k, rw_ln_w, rw_ln_b, sw_sinks, ml_conv_w, ml_conv_b, ml_norm_w, w_out, ln1_w, ln1_b, router_w, router_b, exp_w_up, exp_b_up, exp_w_down, exp_b_down, ln2_w, ln2_b):
    batch, seq, d_model = x.shape
    assert d_model == D_MODEL and seq % ROW_TILE == 0 and seq % ML_CHUNK == 0 and seq % WINDOW == 0
    m = batch * seq
    depth = w_in.shape[0]
    row2 = lambda t: t.reshape(1, -1)

    sw_perm = _sw_head_order()
    in_cols = np.concatenate([np.arange(RW_COLS), RW_COLS + sw_perm, np.arange(RW_COLS + SW_DIM, C_IN)])
    ones_rw = _block_ones(RW_DIM)
    ones_ml = _block_ones(ML_DIM)
    cos_t, sin_t = _rope_tables(seq)
    src = np.arange(LANES)[:, None]
    dst = np.arange(2 * ML_DIM)[None, :]
    expand = jnp.asarray(src == (dst // ML_DIM) * ML_HEADS + (dst % ML_DIM) // HEAD_DIM, BF16)
    pj = np.arange(2 * LANES)[:, None]
    pc = np.arange(2 * LANES)[None, :]
    perm = jnp.asarray(pj == 2 * (pc % LANES) + pc // LANES, BF16)
    n_layers = exp_b_up.shape[0]
    b_glu = exp_b_up[:, :, 0::2].reshape(n_layers, N_EXPERTS, 1, D_EXPERT)
    b_lin = exp_b_up[:, :, 1::2].reshape(n_layers, N_EXPERTS, 1, D_EXPERT)
    b_dn = exp_b_down.reshape(n_layers, N_EXPERTS, 1, D_MODEL)

    w_in_l = [jnp.pad(w_in[l][:, in_cols], ((0, 0), (0, C_IN_PAD - C_IN))).astype(BF16) for l in range(depth)]
    b_in_l = [jnp.pad(b_in[l][in_cols], (0, C_IN_PAD - C_IN)).reshape(1, C_IN_PAD) for l in range(depth)]

    h = x.reshape(m, D_MODEL)
    p_rw, p_sw, p_ml = _in_proj(h, w_in_l[0], b_in_l[0])
    for l in range(depth):
        zero = jnp.zeros((RW_LORA_W, RW_DIM), F32)
        wcomb = jnp.concatenate([jnp.concatenate([rw_w_up[l], zero], axis=1),
                                 jnp.concatenate([zero, rw_a_up[l]], axis=1)], axis=0).astype(BF16)
        w_o = w_out[l]
        w_o_rw = w_o[0:RW_DIM].astype(BF16)
        w_o_sw = w_o[RW_DIM + sw_perm].astype(BF16)
        w_o_ml = w_o[RW_DIM + SW_DIM:].astype(BF16)
        rw_pad = jnp.pad(router_w[l], ((0, 0), (0, LANES - N_EXPERTS)))
        rw_hi = rw_pad.astype(BF16)
        rw_r1 = rw_pad - rw_hi.astype(F32)
        rw_mid = rw_r1.astype(BF16)
        rw_lo = (rw_r1 - rw_mid.astype(F32)).astype(BF16)
        rb_pad = jnp.pad(router_b[l], (0, LANES - N_EXPERTS), constant_values=NEG_BIG).reshape(1, LANES)

        r, lw, k, v, a, b, g = _rw_prep(p_rw, seq, row2(rw_shift_mu[l]), wcomb, rw_g_up[l].astype(BF16),
                                        row2(rw_w0[l]), row2(rw_a0[l]), row2(rw_k_k[l]), row2(rw_k_a[l]), ones_rw)
        m_mat, yh, g_mat, y0 = _rw_chunks(r, lw, k, v, a, b)
        rw_out = _rw_scan(batch, seq, m_mat, yh, g_mat, y0, r, k, v, g, row2(rw_r_k[l]), row2(rw_ln_w[l]),
                          row2(rw_ln_b[l]), ones_rw).reshape(m, RW_DIM)
        sw_out = _swa(p_sw.reshape(batch, seq, SW_COLS), sw_sinks[l], cos_t, sin_t).reshape(m, SW_DIM)
        ml_out = _mlstm(p_ml.reshape(batch, seq, ML_COLS_PAD), ml_conv_w[l], row2(ml_conv_b[l]),
                        row2(ml_norm_w[l]), expand, ones_ml).reshape(m, ML_DIM)
        h1, h1_packed, idx_pad, gates = _out_proj(rw_out, sw_out, ml_out, h, w_o_rw, w_o_sw, w_o_ml, row2(ln1_w[l]),
                                                 row2(ln1_b[l]), (rw_hi, rw_mid, rw_lo), rb_pad)

        tok, slot_ext, block_e, z_rows = _routing_tables(idx_pad[:, 0:TOP_K])
        z = _experts(l, block_e, tok, slot_ext, h1_packed, exp_w_up, b_glu, b_lin, exp_w_down, b_dn, perm, z_rows)
        if l + 1 < depth:
            h, p_rw, p_sw, p_ml = _combine(z, gates, h1, row2(ln2_w[l]), row2(ln2_b[l]), w_in_l[l + 1], b_in_l[l + 1])
        else:
            h = _combine(z, gates, h1, row2(ln2_w[l]), row2(ln2_b[l]))
    return h.reshape(batch, seq, D_MODEL)
```
